```python
import math
import jax
import jax.numpy as jnp
from jax import lax
import numpy as np

D_MODEL = 1024
BATCH = 8
SEQ = 2048
DEPTH = 4

GRID_W = 64
CTX_LEN = 256
N_MIXERS = 4
MIX_DIFF_ATTN = 0
MIX_HYENA = 1
MIX_RETENTION = 2
MIX_SHORTCONV = 3
CTX_READING_MIXERS = (MIX_DIFF_ATTN, MIX_RETENTION)
N_MOD = 9
D_FF = 2816
LN_EPS = 1e-5
ROPE_BASE = 10000.0
DA_HEAD_DIM = 64
DA_HEADS = D_MODEL // (2 * DA_HEAD_DIM)
DA_V_DIM = 2 * DA_HEAD_DIM
Q_BLOCK = 128
HY_EMB = 33
HY_FH = 64
HY_TARGET = 1e-2
HY_FAST_PCT = 0.3
HY_SLOW_PCT = 1.5
RT_HEADS = 4
RT_QK_DIM = D_MODEL // RT_HEADS
RT_V_DIM = 2 * D_MODEL // RT_HEADS
RT_CHUNK = 128

kernel_name = 'hybrid_interleaved_flow_backbone'


def layer_norm(x, g, b):
    xf = x.astype(jnp.float32)
    mu = jnp.mean(xf, -1, keepdims=True)
    var = jnp.mean(jnp.square(xf - mu), -1, keepdims=True)
    y = (xf - mu) * lax.rsqrt(var + LN_EPS) * g.astype(jnp.float32) + b.astype(jnp.float32)
    return y.astype(x.dtype)


def modulation(cond, w, b):
    m = jax.nn.silu(cond) @ w + b
    return jnp.split(m[..., None, :], N_MOD, axis=-1)


def modulate(x, shift, scale):
    return x * (1.0 + scale) + shift


def swiglu(h, wi, wo):
    a, u = jnp.split(h @ wi, 2, axis=-1)
    return (jax.nn.silu(a) * u) @ wo


def half_ffn(x, shift, scale, gate, wi, wo, g, b, alpha):
    y = swiglu(modulate(x, shift, scale), wi, wo)
    return layer_norm(alpha * x + 0.5 * gate * y, g, b)


def axial_rope(n_tokens, dim):
    rows = n_tokens // GRID_W
    row = jnp.repeat(jnp.arange(rows), GRID_W).astype(jnp.float32)
    col = jnp.tile(jnp.arange(GRID_W), rows).astype(jnp.float32)
    n_freq = dim // 4
    inv = ROPE_BASE ** (-jnp.arange(n_freq, dtype=jnp.float32) / n_freq)
    ang = jnp.concatenate([row[:, None] * inv, col[:, None] * inv], axis=-1)
    return jnp.cos(ang), jnp.sin(ang)


def apply_rope(x, cos, sin):
    shape = (x.shape[1],) + (1,) * (x.ndim - 3) + (cos.shape[-1],)
    cos = cos.reshape(shape).astype(x.dtype)
    sin = sin.reshape(shape).astype(x.dtype)
    x1, x2 = jnp.split(x, 2, axis=-1)
    return jnp.concatenate([x1 * cos - x2 * sin, x1 * sin + x2 * cos], axis=-1)


def dwconv3(u, w):
    up = jnp.pad(u, ((0, 0), (1, 1), (0, 0)))
    return up[:, :-2] * w[0] + up[:, 1:-1] * w[1] + up[:, 2:] * w[2]


def diff_attention(h_lat, h_ctx, ctx_out, w_qkv, w_o, lam, subln_g, lam_init):
    def project(h):
        b, n, _ = h.shape
        q, k, v = jnp.split(h @ w_qkv, 3, axis=-1)
        q = q.reshape(b, n, DA_HEADS, 2, DA_HEAD_DIM) * DA_HEAD_DIM ** -0.5
        k = k.reshape(b, n, DA_HEADS, 2, DA_HEAD_DIM)
        v = v.reshape(b, n, DA_HEADS, DA_V_DIM)
        return q, k, v

    lam = lam.astype(jnp.float32)
    lam_full = jnp.exp(jnp.sum(lam[0] * lam[1])) - jnp.exp(jnp.sum(lam[2] * lam[3])) + lam_init

    def attend(q, k, v):
        s = jnp.einsum('bqhmd,bkhmd->bhmqk', q, k).astype(jnp.float32)
        p = jax.nn.softmax(s, axis=-1)
        a = (p[:, :, 0] - lam_full * p[:, :, 1]).astype(v.dtype)
        return jnp.einsum('bhqk,bkhe->bqhe', a, v)

    def finish(o):
        b, n = o.shape[:2]
        of = o.astype(jnp.float32)
        of = of * lax.rsqrt(jnp.mean(of * of, -1, keepdims=True) + LN_EPS)
        of = of * subln_g.astype(jnp.float32) * (1.0 - lam_init)
        return of.astype(o.dtype).reshape(b, n, DA_HEADS * DA_V_DIM) @ w_o

    B, L = h_lat.shape[:2]
    qc, kc, vc = project(h_ctx)
    ql, kl, vl = project(h_lat)
    cos, sin = axial_rope(L, DA_HEAD_DIM)
    ql = apply_rope(ql, cos, sin)
    kl = apply_rope(kl, cos, sin)
    k_all = jnp.concatenate([kc, kl], axis=1)
    v_all = jnp.concatenate([vc, vl], axis=1)
    nb = L // Q_BLOCK
    q_blocks = jnp.moveaxis(ql.reshape(B, nb, Q_BLOCK, DA_HEADS, 2, DA_HEAD_DIM), 1, 0)
    o = lax.map(lambda qb: attend(qb, k_all, v_all), q_blocks)
    o = jnp.moveaxis(o, 0, 1).reshape(B, L, DA_HEADS, DA_V_DIM)
    y_lat = finish(o)
    y_ctx = finish(attend(qc, kc, vc)) if ctx_out else None
    return y_lat, y_ctx


def hyena_filters(n, w1, b1, f1, w2, b2, f2, w3):
    f32 = jnp.float32
    t = jnp.linspace(0.0, 1.0, n, dtype=f32)[:, None]
    bands = (HY_EMB - 1) // 2
    fr = jnp.linspace(1e-4, bands - 1, bands, dtype=f32)[None, :]
    w = 2.0 * math.pi * jnp.arange(n, dtype=f32)[:, None] / n
    z = jnp.concatenate([t, jnp.cos(fr * w), -jnp.sin(fr * w)], axis=-1)
    h = jnp.sin(f1.astype(f32) * (z @ w1.astype(f32) + b1.astype(f32)))
    h = jnp.sin(f2.astype(f32) * (h @ w2.astype(f32) + b2.astype(f32)))
    h = h @ w3.astype(f32)
    max_decay = math.log(HY_TARGET) / HY_FAST_PCT
    min_decay = math.log(HY_TARGET) / HY_SLOW_PCT
    deltas = jnp.abs(jnp.linspace(min_decay, max_decay, D_MODEL, dtype=f32))
    h = h * jnp.exp(-t * jnp.tile(deltas, 2)[None, :])
    return h[:, :D_MODEL], h[:, D_MODEL:]


def bidir_long_conv(u, h_f, h_b, d_skip):
    L = u.shape[1]
    n = 2 * L
    h_full = jnp.concatenate([h_f, jnp.zeros((1, h_f.shape[1]), h_f.dtype), h_b[1:][::-1]], axis=0)
    uf = jnp.fft.rfft(u.astype(jnp.float32), n=n, axis=1)
    hf = jnp.fft.rfft(h_full, n=n, axis=0)
    y = jnp.fft.irfft(uf * hf[None], n=n, axis=1)[:, :L]
    return (y + u.astype(jnp.float32) * d_skip.astype(jnp.float32)).astype(u.dtype)


def hyena(h_lat, h_ctx, ctx_out, w_in, conv_w, conv_b, fw1, fb1, ff1, fw2, fb2, ff2, fw3, d_skip, w_o):
    def run(h):
        n = h.shape[1]
        h_f, h_b = hyena_filters(n, fw1, fb1, ff1, fw2, fb2, ff2, fw3)
        u = dwconv3(h @ w_in, conv_w) + conv_b
        x0, x1, v = jnp.split(u, 3, axis=-1)
        return (x0 * bidir_long_conv(x1 * v, h_f, h_b, d_skip)) @ w_o
    return run(h_lat), (run(h_ctx) if ctx_out else None)


def retention_chunkwise(q, k, v, gamma, state):
    f32 = jnp.float32
    b, n, h, _ = q.shape
    dv = v.shape[-1]
    nc = n // RT_CHUNK
    log_g = jnp.log(gamma)
    pos = jnp.arange(RT_CHUNK, dtype=f32)
    lag = pos[:, None] - pos[None, :]
    d_intra = jnp.where(lag >= 0, jnp.exp(jnp.maximum(lag, 0.0)[None] * log_g[:, None, None]), 0.0)
    d_read = jnp.exp((pos[:, None] + 1.0) * log_g[None, :])[None, :, :, None]
    d_write = jnp.exp((RT_CHUNK - 1.0 - pos[:, None]) * log_g[None, :])[None, :, :, None]
    d_chunk = jnp.exp(RT_CHUNK * log_g)[None, :, None, None]

    def chunks(a):
        return jnp.moveaxis(a.astype(f32).reshape(b, nc, RT_CHUNK, h, a.shape[-1]), 1, 0)

    def step(s, qkv):
        qc, kc, vc = qkv
        scores = jnp.einsum('bihd,bjhd->bhij', qc, kc) * d_intra
        out = jnp.einsum('bhij,bjhe->bihe', scores, vc) + jnp.einsum('bihd,bhde->bihe', qc, s) * d_read
        s = d_chunk * s + jnp.einsum('bjhd,bjhe->bhde', kc * d_write, vc)
        return s, out

    state, out = lax.scan(step, state, (chunks(q), chunks(k), chunks(v)))
    return jnp.moveaxis(out, 0, 1).reshape(b, n, h, dv), state


def retention(h_lat, h_ctx, ctx_out, w_in, decay_logit, gn_g, w_o):
    qk_w = RT_HEADS * RT_QK_DIM
    v_w = RT_HEADS * RT_V_DIM

    def project(h):
        b, n, _ = h.shape
        q, k, v, g = jnp.split(h @ w_in, [qk_w, 2 * qk_w, 2 * qk_w + v_w], axis=-1)
        q = q.reshape(b, n, RT_HEADS, RT_QK_DIM)
        k = k.reshape(b, n, RT_HEADS, RT_QK_DIM) * RT_QK_DIM ** -0.5
        v = v.reshape(b, n, RT_HEADS, RT_V_DIM)
        return q, k, v, g

    def finish(o, g):
        b, n = o.shape[:2]
        mu = jnp.mean(o, -1, keepdims=True)
        var = jnp.mean(jnp.square(o - mu), -1, keepdims=True)
        o = ((o - mu) * lax.rsqrt(var + LN_EPS)).reshape(b, n, v_w) * gn_g.astype(jnp.float32)
        return (jax.nn.silu(g) * o.astype(g.dtype)) @ w_o

    B, L = h_lat.shape[:2]
    gam = jax.nn.sigmoid(decay_logit.astype(jnp.float32))
    qc, kc, vc, gc = project(h_ctx)
    ql, kl, vl, gl = project(h_lat)
    cos, sin = axial_rope(L, RT_QK_DIM)
    ql = apply_rope(ql, cos, sin)
    kl = apply_rope(kl, cos, sin)
    zero = jnp.zeros((B, RT_HEADS, RT_QK_DIM, RT_V_DIM), jnp.float32)
    flip = lambda a: jnp.flip(a, axis=1)
    oc_f, s_f = retention_chunkwise(qc, kc, vc, gam[0], zero)
    oc_b, s_b = retention_chunkwise(flip(qc), flip(kc), flip(vc), gam[1], zero)
    ol_f, _ = retention_chunkwise(ql, kl, vl, gam[0], s_f)
    ol_b, _ = retention_chunkwise(flip(ql), flip(kl), flip(vl), gam[1], s_b)
    y_lat = finish(ol_f + flip(ol_b), gl)
    y_ctx = finish(oc_f + flip(oc_b), gc) if ctx_out else None
    return y_lat, y_ctx


def short_conv_mixer(h_lat, h_ctx, ctx_out, w_in, conv_w, w_o):
    def run(h):
        b_gate, c_gate, u = jnp.split(h @ w_in, 3, axis=-1)
        return (b_gate * dwconv3(c_gate * u, conv_w)) @ w_o
    return run(h_lat), (run(h_ctx) if ctx_out else None)


def setup_inputs(seed: int = 0) -> dict:
    key = jax.random.key(seed)
    keys = iter(jax.random.split(key, 48))

    def nrm(shape, scale):
        return scale * jax.random.normal(next(keys), shape, jnp.float32)

    d = D_MODEL
    beta = (8.0 * DEPTH) ** -0.25
    n_da = len(range(MIX_DIFF_ATTN, DEPTH, N_MIXERS))
    n_hy = len(range(MIX_HYENA, DEPTH, N_MIXERS))
    n_rt = len(range(MIX_RETENTION, DEPTH, N_MIXERS))
    n_sc = len(range(MIX_SHORTCONV, DEPTH, N_MIXERS))
    qk_w = RT_HEADS * RT_QK_DIM
    v_w = RT_HEADS * RT_V_DIM
    gamma0 = 1.0 - 2.0 ** (-5.0 - jnp.arange(RT_HEADS, dtype=jnp.float32))
    logit0 = jnp.log(gamma0) - jnp.log1p(-gamma0)
    return {
        'x': nrm((BATCH, SEQ, d), 1.0),
        'c': nrm((BATCH, d), 1.0),
        'ctx': nrm((BATCH, CTX_LEN, d), 1.0),
        'c_ctx': nrm((d,), 1.0),
        'ada_w': nrm((DEPTH, d, N_MOD * d), 0.5 * d ** -0.5),
        'ada_b': nrm((DEPTH, N_MOD * d), 0.02),
        'ln_g': 1.0 + nrm((DEPTH, 3, d), 0.02),
        'ln_b': nrm((DEPTH, 3, d), 0.02),
        'ffa_wi': nrm((DEPTH, d, 2 * D_FF), d ** -0.5),
        'ffa_wo': nrm((DEPTH, D_FF, d), beta * D_FF ** -0.5),
        'ffb_wi': nrm((DEPTH, d, 2 * D_FF), d ** -0.5),
        'ffb_wo': nrm((DEPTH, D_FF, d), beta * D_FF ** -0.5),
        'da_w_qkv': nrm((n_da, d, 3 * d), d ** -0.5),
        'da_w_o': nrm((n_da, d, d), beta * d ** -0.5),
        'da_lambda': nrm((n_da, 4, DA_HEAD_DIM), 0.1),
        'da_subln_g': 1.0 + nrm((n_da, DA_V_DIM), 0.02),
        'hy_w_in': nrm((n_hy, d, 3 * d), d ** -0.5),
        'hy_conv_w': nrm((n_hy, 3, 3 * d), 3 ** -0.5),
        'hy_conv_b': nrm((n_hy, 3 * d), 0.02),
        'hy_fw1': nrm((n_hy, HY_EMB, HY_FH), HY_EMB ** -0.5),
        'hy_fb1': nrm((n_hy, HY_FH), 0.02),
        'hy_ff1': 1.0 + nrm((n_hy, HY_FH), 0.02),
        'hy_fw2': nrm((n_hy, HY_FH, HY_FH), HY_FH ** -0.5),
        'hy_fb2': nrm((n_hy, HY_FH), 0.02),
        'hy_ff2': 1.0 + nrm((n_hy, HY_FH), 0.02),
        'hy_fw3': nrm((n_hy, HY_FH, 2 * d), HY_FH ** -0.5),
        'hy_d_skip': nrm((n_hy, d), 1.0),
        'hy_w_o': nrm((n_hy, d, d), beta * d ** -0.5),
        'rt_w_in': nrm((n_rt, d, 2 * qk_w + 2 * v_w), d ** -0.5),
        'rt_decay_logit': logit0 + nrm((n_rt, 2, RT_HEADS), 0.1),
        'rt_gn_g': 1.0 + nrm((n_rt, v_w), 0.02),
        'rt_w_o': nrm((n_rt, v_w, d), beta * v_w ** -0.5),
        'sc_w_in': nrm((n_sc, d, 3 * d), d ** -0.5),
        'sc_conv_w': nrm((n_sc, 3, d), 3 ** -0.5),
        'sc_w_o': nrm((n_sc, d, d), beta * d ** -0.5),
    }


def reference(x, c, ctx, c_ctx, ada_w, ada_b, ln_g, ln_b, ffa_wi, ffa_wo, ffb_wi, ffb_wo,
              da_w_qkv, da_w_o, da_lambda, da_subln_g,
              hy_w_in, hy_conv_w, hy_conv_b, hy_fw1, hy_fb1, hy_ff1, hy_fw2, hy_fb2, hy_ff2, hy_fw3,
              hy_d_skip, hy_w_o,
              rt_w_in, rt_decay_logit, rt_gn_g, rt_w_o,
              sc_w_in, sc_conv_w, sc_w_o):
    alpha = (2.0 * DEPTH) ** 0.25
    ctx_last = max(i for i in range(DEPTH) if i % N_MIXERS in CTX_READING_MIXERS)
    xl, xc = x, ctx
    for i in range(DEPTH):
        kind, j = i % N_MIXERS, i // N_MIXERS
        use_ctx, ctx_next = i <= ctx_last, i < ctx_last
        ml = modulation(c, ada_w[i], ada_b[i])
        mc = modulation(c_ctx, ada_w[i], ada_b[i]) if use_ctx else None

        xl = half_ffn(xl, ml[0], ml[1], ml[2], ffa_wi[i], ffa_wo[i], ln_g[i, 0], ln_b[i, 0], alpha)
        if use_ctx:
            xc = half_ffn(xc, mc[0], mc[1], mc[2], ffa_wi[i], ffa_wo[i], ln_g[i, 0], ln_b[i, 0], alpha)

        hl = modulate(xl, ml[3], ml[4])
        hc = modulate(xc, mc[3], mc[4]) if use_ctx else None
        if kind == MIX_DIFF_ATTN:
            yl, yc = diff_attention(hl, hc, ctx_next, da_w_qkv[j], da_w_o[j], da_lambda[j], da_subln_g[j],
                                    0.8 - 0.6 * math.exp(-0.3 * i))
        elif kind == MIX_HYENA:
            yl, yc = hyena(hl, hc, ctx_next, hy_w_in[j], hy_conv_w[j], hy_conv_b[j], hy_fw1[j], hy_fb1[j],
                           hy_ff1[j], hy_fw2[j], hy_fb2[j], hy_ff2[j], hy_fw3[j], hy_d_skip[j], hy_w_o[j])
        elif kind == MIX_RETENTION:
            yl, yc = retention(hl, hc, ctx_next, rt_w_in[j], rt_decay_logit[j], rt_gn_g[j], rt_w_o[j])
        else:
            yl, yc = short_conv_mixer(hl, hc, ctx_next, sc_w_in[j], sc_conv_w[j], sc_w_o[j])
        xl = layer_norm(alpha * xl + ml[5] * yl, ln_g[i, 1], ln_b[i, 1])
        if ctx_next:
            xc = layer_norm(alpha * xc + mc[5] * yc, ln_g[i, 1], ln_b[i, 1])

        xl = half_ffn(xl, ml[6], ml[7], ml[8], ffb_wi[i], ffb_wo[i], ln_g[i, 2], ln_b[i, 2], alpha)
        if ctx_next:
            xc = half_ffn(xc, mc[6], mc[7], mc[8], ffb_wi[i], ffb_wo[i], ln_g[i, 2], ln_b[i, 2], alpha)
    return xl
```

```python
import functools
import math

import jax
import jax.numpy as jnp
import numpy as np
from jax import lax
from jax.experimental import pallas as pl
from jax.experimental.pallas import tpu as pltpu

D = 1024
N_MOD = 9
D_FF = 2816
LN_EPS = 1e-5
ROPE_BASE = 10000.0
GRID_W = 64
DEPTH = 4
ALPHA = (2.0 * DEPTH) ** 0.25
DA_HEADS = 8
DA_HEAD_DIM = 64
RT_HEADS = 4
RT_QK = 256
RT_V = 512
HY_EMB = 33
HY_FH = 64
LANES = 128
HALO = 8
FF_CHUNK = 256
RT_CHUNK = 256
VMEM_LIMIT = 56 * 1024 * 1024

F32 = jnp.float32
BF = jnp.bfloat16
NT = (((1,), (1,)), ((), ()))
TN = (((0,), (0,)), ((), ()))


def _dot(a, b):
    return jnp.dot(a, b, preferred_element_type=F32)


def _resident(shape):
    nd = len(shape)
    return pl.BlockSpec(shape, lambda *_: (0,) * nd, pipeline_mode=pl.Buffered(1))


def _params(n_axes):
    return pltpu.CompilerParams(dimension_semantics=("parallel",) * n_axes,
                                vmem_limit_bytes=VMEM_LIMIT)


def _layer_norm(r, g, b):
    mu = jnp.mean(r, -1, keepdims=True)
    d = r - mu
    var = jnp.mean(d * d, -1, keepdims=True)
    return d * lax.rsqrt(var + LN_EPS) * g + b


def _silu(a):
    return a * jax.nn.sigmoid(a)


def _modulated(x, mod_ref, j):
    return (x * (1.0 + mod_ref[j + 1:j + 2, :]) + mod_ref[j:j + 1, :]).astype(BF)


def _mod_kernel(c_ref, w_ref, b_ref, o_ref):
    s = _silu(c_ref[...]).astype(BF)
    o_ref[...] = _dot(s, w_ref[...].astype(BF)) + b_ref[...]


def _modulation_all(cond, ada_w, ada_b):
    r = cond.shape[0]
    tn = 2304
    n = N_MOD * D
    out = pl.pallas_call(
        _mod_kernel,
        grid=(DEPTH, n // tn),
        in_specs=[pl.BlockSpec((r, D), lambda i, j: (0, 0)),
                  pl.BlockSpec((None, D, tn), lambda i, j: (i, 0, j)),
                  pl.BlockSpec((None, 1, tn), lambda i, j: (i, 0, j))],
        out_specs=pl.BlockSpec((None, r, tn), lambda i, j: (i, 0, j)),
        out_shape=jax.ShapeDtypeStruct((DEPTH, r, n), F32),
        compiler_params=_params(2),
        name="modulation",
    )(cond, ada_w, ada_b.reshape(DEPTH, 1, n))
    return out.reshape(DEPTH, r, N_MOD, D)


def _ffn_kernel(x_ref, mod_ref, wi_ref, wo_ref, g_ref, b_ref, o_ref, act_ref, *, j0):
    x = x_ref[...]
    h = _modulated(x, mod_ref, j0)
    for c in range(D_FF // FF_CHUNK):
        lo = c * FF_CHUNK
        a = _dot(h, wi_ref[:, lo:lo + FF_CHUNK])
        u = _dot(h, wi_ref[:, D_FF + lo:D_FF + lo + FF_CHUNK])
        act_ref[:, lo:lo + FF_CHUNK] = (_silu(a) * u).astype(BF)
    y = _dot(act_ref[...], wo_ref[...])
    r = ALPHA * x + (0.5 * mod_ref[j0 + 2:j0 + 3, :]) * y
    o_ref[...] = _layer_norm(r, g_ref[...], b_ref[...])


def _half_ffn(x2, mod, bsel, wi, wo, g, b, j0, tm):
    t = x2.shape[0]
    return pl.pallas_call(
        functools.partial(_ffn_kernel, j0=j0),
        grid=(t // tm,),
        in_specs=[pl.BlockSpec((tm, D), lambda i: (i, 0)),
                  pl.BlockSpec((None, N_MOD, D), lambda i: (bsel(i), 0, 0)),
                  _resident((D, 2 * D_FF)), _resident((D_FF, D)),
                  _resident((1, D)), _resident((1, D))],
        out_specs=pl.BlockSpec((tm, D), lambda i: (i, 0)),
        out_shape=jax.ShapeDtypeStruct((t, D), F32),
        scratch_shapes=[pltpu.VMEM((tm, D_FF), BF)],
        compiler_params=_params(1),
        name="half_ffn",
    )(x2, mod, wi, wo, g, b)


def _outproj_kernel(a_ref, w_ref, x_ref, mod_ref, g_ref, b_ref, o_ref):
    y = _dot(a_ref[...], w_ref[...])
    r = ALPHA * x_ref[...] + mod_ref[5:6, :] * y
    o_ref[...] = _layer_norm(r, g_ref[...], b_ref[...])


def _outproj(a2, w, x2, mod, bsel, g, b, tm):
    t, k = a2.shape
    return pl.pallas_call(
        _outproj_kernel,
        grid=(t // tm,),
        in_specs=[pl.BlockSpec((tm, k), lambda i: (i, 0)),
                  _resident((k, D)),
                  pl.BlockSpec((tm, D), lambda i: (i, 0)),
                  pl.BlockSpec((None, N_MOD, D), lambda i: (bsel(i), 0, 0)),
                  _resident((1, D)), _resident((1, D))],
        out_specs=pl.BlockSpec((tm, D), lambda i: (i, 0)),
        out_shape=jax.ShapeDtypeStruct((t, D), F32),
        compiler_params=_params(1),
        name="outproj",
    )(a2, w, x2, mod, g, b)


def _rope_angles(n_tokens, dim):
    rows = n_tokens // GRID_W
    row = np.repeat(np.arange(rows), GRID_W).astype(np.float32)
    col = np.tile(np.arange(GRID_W), rows).astype(np.float32)
    n_freq = dim // 4
    inv = (ROPE_BASE ** (-np.arange(n_freq, dtype=np.float32) / n_freq)).astype(np.float32)
    return np.concatenate([row[:, None] * inv, col[:, None] * inv], axis=-1)


def _da_rope_tables(n_tokens):
    ang = _rope_angles(n_tokens, DA_HEAD_DIM)
    cos, sin = np.cos(ang), np.sin(ang)
    cos128 = np.tile(cos, (1, 4))
    sin128 = np.tile(np.concatenate([-sin, sin], axis=-1), (1, 2))
    return jnp.asarray(cos128, F32), jnp.asarray(sin128, F32)


def _rt_rope_tables(n_tokens):
    ang = _rope_angles(n_tokens, RT_QK)
    return jnp.asarray(np.cos(ang), F32), jnp.asarray(np.sin(ang), F32)


def _proj_da_kernel(*refs, rope):
    if rope:
        x_ref, mod_ref, w_ref, cos_ref, sin_ref, q_ref, k_ref, v_ref = refs
        cos, sin = cos_ref[...], sin_ref[...]
        lane = lax.broadcasted_iota(jnp.int32, (1, LANES), 1)
        first_half = (lane % DA_HEAD_DIM) < (DA_HEAD_DIM // 2)
    else:
        x_ref, mod_ref, w_ref, q_ref, k_ref, v_ref = refs
    h = _modulated(x_ref[...], mod_ref, 3)

    def rot(y):
        if not rope:
            return y
        partner = jnp.where(first_half, pltpu.roll(y, LANES - 32, 1), pltpu.roll(y, 32, 1))
        return y * cos + partner * sin

    q = _dot(h, w_ref[:, 0:D]) * (DA_HEAD_DIM ** -0.5)
    for j in range(D // LANES):
        q_ref[:, j * LANES:(j + 1) * LANES] = rot(q[:, j * LANES:(j + 1) * LANES]).astype(BF)
    k = _dot(h, w_ref[:, D:2 * D])
    for j in range(D // LANES):
        k_ref[:, j * LANES:(j + 1) * LANES] = rot(k[:, j * LANES:(j + 1) * LANES]).astype(BF)
    v_ref[...] = _dot(h, w_ref[:, 2 * D:3 * D]).astype(BF)


def _proj_da(x2, mod, bsel, w, rope_tabs, seq, tm):
    t = x2.shape[0]
    rpb = seq // tm
    rope = rope_tabs is not None
    in_specs = [pl.BlockSpec((tm, D), lambda i: (i, 0)),
                pl.BlockSpec((None, N_MOD, D), lambda i: (bsel(i), 0, 0)),
                _resident((D, 3 * D))]
    args = [x2, mod, w]
    if rope:
        in_specs += [pl.BlockSpec((tm, LANES), lambda i: (i % rpb, 0))] * 2
        args += list(rope_tabs)
    return pl.pallas_call(
        functools.partial(_proj_da_kernel, rope=rope),
        grid=(t // tm,),
        in_specs=in_specs,
        out_specs=[pl.BlockSpec((tm, D), lambda i: (i, 0))] * 3,
        out_shape=[jax.ShapeDtypeStruct((t, D), BF)] * 3,
        compiler_params=_params(1),
        name="proj_da",
    )(*args)


def _attn_kernel(lam_ref, g_ref, q_ref, k_ref, v_ref, o_ref, *, lam_init):
    q, k, v = q_ref[...], k_ref[...], v_ref[...]
    lane = lax.broadcasted_iota(jnp.int32, q.shape, 1)
    zero = jnp.zeros_like(q)
    q0 = jnp.where(lane < DA_HEAD_DIM, q, zero)
    q1 = jnp.where(lane >= DA_HEAD_DIM, q, zero)

    def softmax(s):
        e = jnp.exp(s - jnp.max(s, -1, keepdims=True))
        return e / jnp.sum(e, -1, keepdims=True)

    p0 = softmax(lax.dot_general(q0, k, NT, preferred_element_type=F32))
    p1 = softmax(lax.dot_general(q1, k, NT, preferred_element_type=F32))
    lam = lam_ref[...]
    lam_full = (jnp.exp(jnp.sum(lam[0:1] * lam[1:2], -1, keepdims=True))
                - jnp.exp(jnp.sum(lam[2:3] * lam[3:4], -1, keepdims=True)) + lam_init)
    a = (p0 - lam_full * p1).astype(BF)
    o = _dot(a, v)
    o = o * lax.rsqrt(jnp.mean(o * o, -1, keepdims=True) + LN_EPS)
    o_ref[...] = (o * g_ref[...] * (1.0 - lam_init)).astype(BF)


def _diff_attn(q, k_all, v_all, lam, subln_g, lam_init, tq):
    b, n, _ = q.shape
    nk = k_all.shape[1]
    return pl.pallas_call(
        functools.partial(_attn_kernel, lam_init=lam_init),
        grid=(b, DA_HEADS, n // tq),
        in_specs=[_resident((4, DA_HEAD_DIM)), _resident((1, LANES)),
                  pl.BlockSpec((None, tq, LANES), lambda bi, h, i: (bi, i, h)),
                  pl.BlockSpec((None, nk, LANES), lambda bi, h, i: (bi, 0, h)),
                  pl.BlockSpec((None, nk, LANES), lambda bi, h, i: (bi, 0, h))],
        out_specs=pl.BlockSpec((None, tq, LANES), lambda bi, h, i: (bi, i, h)),
        out_shape=jax.ShapeDtypeStruct((b, n, D), BF),
        compiler_params=_params(3),
        name="diff_attn",
    )(lam, subln_g, q, k_all, v_all)


def _halo_specs(tm, t):
    blocks_per_tile = tm // HALO
    last = t // HALO - 1
    return [pl.BlockSpec((HALO, D), lambda i: (jnp.maximum(i * blocks_per_tile - 1, 0), 0)),
            pl.BlockSpec((tm, D), lambda i: (i, 0)),
            pl.BlockSpec((HALO, D), lambda i: (jnp.minimum((i + 1) * blocks_per_tile, last), 0))]


def _halo_valid(tm, rpb):
    pos = pl.program_id(0) % rpb
    row = lax.broadcasted_iota(jnp.int32, (tm + 2 * HALO, 1), 0)
    outside = ((pos == 0) & (row < HALO)) | ((pos == rpb - 1) & (row >= tm + HALO))
    return jnp.logical_not(outside)


def _dwconv3_rows(u, w):
    n = u.shape[0]
    y = pltpu.roll(u, 1, 0) * w[0:1, :] + u * w[1:2, :] + pltpu.roll(u, n - 1, 0) * w[2:3, :]
    return y[HALO:n - HALO, :]


def _proj_hy_kernel(xp_ref, x_ref, xn_ref, mod_ref, w_ref, cw_ref, cb_ref, x0_ref, z_ref, *, tm, rpb):
    xe = jnp.concatenate([xp_ref[...], x_ref[...], xn_ref[...]], axis=0)
    h = _modulated(xe, mod_ref, 3)
    valid = _halo_valid(tm, rpb)

    def branch(j):
        u = jnp.where(valid, _dot(h, w_ref[:, j * D:(j + 1) * D]), 0.0)
        return _dwconv3_rows(u, cw_ref[:, j * D:(j + 1) * D]) + cb_ref[:, j * D:(j + 1) * D]

    x0_ref[...] = branch(0)
    z_ref[...] = branch(1) * branch(2)


def _proj_hy(x2, mod, bsel, w, conv_w, conv_b, seq, tm):
    t = x2.shape[0]
    return pl.pallas_call(
        functools.partial(_proj_hy_kernel, tm=tm, rpb=seq // tm),
        grid=(t // tm,),
        in_specs=_halo_specs(tm, t) + [
            pl.BlockSpec((None, N_MOD, D), lambda i: (bsel(i), 0, 0)),
            _resident((D, 3 * D)), _resident((3, 3 * D)), _resident((1, 3 * D))],
        out_specs=[pl.BlockSpec((tm, D), lambda i: (i, 0))] * 2,
        out_shape=[jax.ShapeDtypeStruct((t, D), F32)] * 2,
        compiler_params=_params(1),
        name="proj_hy",
    )(x2, x2, x2, mod, w, conv_w, conv_b)


def _dft_tables(n):
    k = np.arange(n, dtype=np.int64)
    ph = (k[:, None] * k[None, :]) % (2 * n)
    ang = ph.astype(np.float64) * (math.pi / n)
    cos = np.cos(ang).astype(np.float32)
    sin = np.sin(ang).astype(np.float32)
    tail = np.zeros((HALO, n), np.float32)
    tail[0] = 1.0 - 2.0 * (k % 2)
    return (jnp.asarray(np.concatenate([cos, tail], 0)).astype(BF), jnp.asarray(sin).astype(BF))


def _hy_features(n):
    t = np.linspace(0.0, 1.0, n, dtype=np.float32)[:, None]
    bands = (HY_EMB - 1) // 2
    fr = np.linspace(1e-4, bands - 1, bands, dtype=np.float32)[None, :]
    w = (2.0 * math.pi * np.arange(n, dtype=np.float32)[:, None] / n).astype(np.float32)
    z = np.concatenate([t, np.cos(fr * w), -np.sin(fr * w)], axis=-1).astype(np.float32)
    return jnp.asarray(np.pad(z, ((0, 0), (0, LANES - HY_EMB))), F32)


def _hy_deltas():
    max_decay = math.log(1e-2) / 0.3
    min_decay = math.log(1e-2) / 1.5
    return jnp.asarray(np.abs(np.linspace(min_decay, max_decay, D, dtype=np.float32))[None, :], F32)


def _hy_filter_kernel(z_ref, w1_ref, b1_ref, f1_ref, w2_ref, b2_ref, f2_ref, w3f_ref, w3b_ref,
                      dl_ref, cos_ref, sin_ref, p_ref, q_ref, hn_ref):
    n = z_ref.shape[0]
    z = z_ref[...]
    h = jnp.sin(f1_ref[...] * (_dot(z.astype(BF), w1_ref[...].astype(BF)) + b1_ref[...]))
    h = jnp.sin(f2_ref[...] * (_dot(h.astype(BF), w2_ref[...].astype(BF)) + b2_ref[...]))
    hb16 = h.astype(BF)
    decay = jnp.exp(-z[:, 0:1] * dl_ref[...])
    h_f = _dot(hb16, w3f_ref[...].astype(BF)) * decay
    h_b = _dot(hb16, w3b_ref[...].astype(BF)) * decay
    row = lax.broadcasted_iota(jnp.int32, h_f.shape, 0)
    h_b = jnp.where(row == 0, 0.0, h_b)
    a = _dot(cos_ref[...], (h_f + h_b).astype(BF))
    bm = _dot(sin_ref[...], (h_f - h_b).astype(BF))
    wgt = jnp.where(row == 0, 1.0, 2.0) * (0.5 / n)
    p_ref[...] = a[0:n, :] * wgt
    q_ref[...] = -bm * wgt
    hn_ref[...] = a[n:n + HALO, :] * (0.5 / n)


def _hy_filter(n, tabs, fw1, fb1, ff1, fw2, fb2, ff2, fw3, tc):
    cos_t, sin_t = tabs
    w1 = jnp.pad(fw1, ((0, LANES - HY_EMB), (0, 0)))
    nt = D // tc
    small = lambda s: pl.BlockSpec(s, lambda j: (0, 0))
    return pl.pallas_call(
        _hy_filter_kernel,
        grid=(nt,),
        in_specs=[small((n, LANES)), small((LANES, HY_FH)), small((1, HY_FH)), small((1, HY_FH)),
                  small((HY_FH, HY_FH)), small((1, HY_FH)), small((1, HY_FH)),
                  pl.BlockSpec((HY_FH, tc), lambda j: (0, j)),
                  pl.BlockSpec((HY_FH, tc), lambda j: (0, j + nt)),
                  pl.BlockSpec((1, tc), lambda j: (0, j)),
                  _resident((n + HALO, n)), _resident((n, n))],
        out_specs=[pl.BlockSpec((n, tc), lambda j: (0, j)),
                   pl.BlockSpec((n, tc), lambda j: (0, j)),
                   pl.BlockSpec((HALO, tc), lambda j: (0, j))],
        out_shape=[jax.ShapeDtypeStruct((n, D), F32), jax.ShapeDtypeStruct((n, D), F32),
                   jax.ShapeDtypeStruct((HALO, D), F32)],
        compiler_params=_params(1),
        name="hy_filter",
    )(_hy_features(n), w1, fb1.reshape(1, -1), ff1.reshape(1, -1), fw2, fb2.reshape(1, -1),
      ff2.reshape(1, -1), fw3, fw3, _hy_deltas(), cos_t, sin_t)


def _hy_conv_kernel(z_ref, x0_ref, p_ref, q_ref, hn_ref, ds_ref, cos_ref, sin_ref, o_ref):
    n = z_ref.shape[0]
    z = z_ref[...]
    zb = z.astype(BF)
    a_ext = _dot(cos_ref[...], zb)
    bm = _dot(sin_ref[...], zb)
    a = a_ext[0:n, :]
    p, q = p_ref[...], q_ref[...]
    yr = (a * p + bm * q).astype(BF)
    yi = (a * q - bm * p).astype(BF)
    y = _dot(cos_ref[0:n, :], yr) - _dot(sin_ref[...], yi)
    row = lax.broadcasted_iota(jnp.int32, y.shape, 0)
    sign = (1 - 2 * (row % 2)).astype(F32)
    y = y + sign * (a_ext[n:n + 1, :] * hn_ref[0:1, :])
    o_ref[...] = (x0_ref[...] * (y + z * ds_ref[...])).astype(BF)


def _hy_conv(z, x0, p, q, hn, d_skip, tabs, tc):
    b, n, _ = z.shape
    cos_t, sin_t = tabs
    return pl.pallas_call(
        _hy_conv_kernel,
        grid=(D // tc, b),
        in_specs=[pl.BlockSpec((None, n, tc), lambda j, bi: (bi, 0, j)),
                  pl.BlockSpec((None, n, tc), lambda j, bi: (bi, 0, j)),
                  pl.BlockSpec((n, tc), lambda j, bi: (0, j)),
                  pl.BlockSpec((n, tc), lambda j, bi: (0, j)),
                  pl.BlockSpec((HALO, tc), lambda j, bi: (0, j)),
                  pl.BlockSpec((1, tc), lambda j, bi: (0, j)),
                  _resident((n + HALO, n)), _resident((n, n))],
        out_specs=pl.BlockSpec((None, n, tc), lambda j, bi: (bi, 0, j)),
        out_shape=jax.ShapeDtypeStruct((b, n, D), BF),
        compiler_params=_params(2),
        name="hy_conv",
    )(z, x0, p, q, hn, d_skip, cos_t, sin_t)


def _proj_rt_kernel(*refs, rope):
    if rope:
        x_ref, mod_ref, w_ref, cos_ref, sin_ref, q_ref, k_ref, v_ref, g_ref = refs
        cos, sin = cos_ref[...], sin_ref[...]
    else:
        x_ref, mod_ref, w_ref, k_ref, v_ref = refs
    h = _modulated(x_ref[...], mod_ref, 3)
    qk_w = RT_HEADS * RT_QK
    half = RT_QK // 2

    def store_rot(y, ref):
        for hd in range(RT_HEADS):
            lo = hd * RT_QK
            x1, x2 = y[:, lo:lo + half], y[:, lo + half:lo + RT_QK]
            if rope:
                x1, x2 = x1 * cos - x2 * sin, x1 * sin + x2 * cos
            ref[:, lo:lo + half] = x1.astype(BF)
            ref[:, lo + half:lo + RT_QK] = x2.astype(BF)

    if rope:
        store_rot(_dot(h, w_ref[:, 0:qk_w]), q_ref)
    store_rot(_dot(h, w_ref[:, qk_w:2 * qk_w]) * (RT_QK ** -0.5), k_ref)
    v_w = RT_HEADS * RT_V
    v_ref[...] = _dot(h, w_ref[:, 2 * qk_w:2 * qk_w + v_w]).astype(BF)
    if rope:
        g_ref[...] = _dot(h, w_ref[:, 2 * qk_w + v_w:2 * qk_w + 2 * v_w])


def _proj_rt(x2, mod, bsel, w, rope_tabs, seq, tm):
    t = x2.shape[0]
    rpb = seq // tm
    rope = rope_tabs is not None
    qk_w, v_w = RT_HEADS * RT_QK, RT_HEADS * RT_V
    in_specs = [pl.BlockSpec((tm, D), lambda i: (i, 0)),
                pl.BlockSpec((None, N_MOD, D), lambda i: (bsel(i), 0, 0)),
                _resident((D, 2 * qk_w + 2 * v_w))]
    args = [x2, mod, w]
    row = lambda width: pl.BlockSpec((tm, width), lambda i: (i, 0))
    if rope:
        in_specs += [pl.BlockSpec((tm, LANES), lambda i: (i % rpb, 0))] * 2
        args += list(rope_tabs)
        out_specs = [row(qk_w), row(qk_w), row(v_w), row(v_w)]
        out_shape = [jax.ShapeDtypeStruct((t, qk_w), BF), jax.ShapeDtypeStruct((t, qk_w), BF),
                     jax.ShapeDtypeStruct((t, v_w), BF), jax.ShapeDtypeStruct((t, v_w), F32)]
    else:
        out_specs = [row(qk_w), row(v_w)]
        out_shape = [jax.ShapeDtypeStruct((t, qk_w), BF), jax.ShapeDtypeStruct((t, v_w), BF)]
    return pl.pallas_call(
        functools.partial(_proj_rt_kernel, rope=rope),
        grid=(t // tm,),
        in_specs=in_specs,
        out_specs=out_specs,
        out_shape=out_shape,
        compiler_params=_params(1),
        name="proj_rt",
    )(*args)


def _ret_kernel(logit_ref, kc_ref, vc_ref, ql_ref, kl_ref, vl_ref, g_ref, gn_ref, o_ref,
                s_ref, acc_ref, *, chunk):
    hd = pl.program_id(1)
    n_ctx = kc_ref.shape[0] // chunk
    n_lat = ql_ref.shape[0] // chunk
    row = lax.broadcasted_iota(jnp.int32, (chunk, chunk), 0)
    col = lax.broadcasted_iota(jnp.int32, (chunk, chunk), 1)
    pos = lax.broadcasted_iota(jnp.int32, (chunk, 1), 0).astype(F32)

    for direction in (0, 1):
        log_g = jnp.log(jax.nn.sigmoid(jnp.full((chunk, 1), logit_ref[direction, hd], F32)))
        lag = (row - col) if direction == 0 else (col - row)
        d_intra = jnp.where(lag >= 0, jnp.exp(jnp.maximum(lag, 0).astype(F32) * log_g), 0.0)
        if direction == 0:
            d_read = jnp.exp((pos + 1.0) * log_g)
            d_write = jnp.exp((chunk - 1.0 - pos) * log_g)
        else:
            d_read = jnp.exp((chunk - pos) * log_g)
            d_write = jnp.exp(pos * log_g)
        d_chunk = jnp.exp(chunk * log_g[0:1, :])
        s_ref[...] = jnp.zeros_like(s_ref)

        def absorb(k_ref, v_ref, rows, d_write=d_write, d_chunk=d_chunk):
            kw = (k_ref[rows, :].astype(F32) * d_write).astype(BF)
            s_ref[...] = d_chunk * s_ref[...] + lax.dot_general(kw, v_ref[rows, :], TN,
                                                                preferred_element_type=F32)

        for c in (range(n_ctx) if direction == 0 else reversed(range(n_ctx))):
            absorb(kc_ref, vc_ref, pl.ds(c * chunk, chunk))

        def body(t, carry, direction=direction, d_intra=d_intra, d_read=d_read, absorb=absorb):
            c = t if direction == 0 else n_lat - 1 - t
            rows = pl.ds(pl.multiple_of(c * chunk, chunk), chunk)
            qc = ql_ref[rows, :]
            scores = lax.dot_general(qc, kl_ref[rows, :], NT, preferred_element_type=F32) * d_intra
            out = _dot(scores.astype(BF), vl_ref[rows, :]) + _dot(qc, s_ref[...].astype(BF)) * d_read
            if direction == 0:
                acc_ref[rows, :] = out
            else:
                acc_ref[rows, :] += out
            absorb(kl_ref, vl_ref, rows)
            return carry

        lax.fori_loop(0, n_lat, body, 0)

    o = acc_ref[...]
    mu = jnp.mean(o, -1, keepdims=True)
    dlt = o - mu
    var = jnp.mean(dlt * dlt, -1, keepdims=True)
    o_ref[...] = (_silu(g_ref[...]) * (dlt * lax.rsqrt(var + LN_EPS) * gn_ref[...])).astype(BF)


def _retention(kc, vc, ql, kl, vl, g, decay_logit, gn_g):
    b, n, _ = ql.shape
    nc = kc.shape[1]
    qk = lambda m: pl.BlockSpec((None, m, RT_QK), lambda bi, h: (bi, 0, h))
    vv = lambda m: pl.BlockSpec((None, m, RT_V), lambda bi, h: (bi, 0, h))
    return pl.pallas_call(
        functools.partial(_ret_kernel, chunk=RT_CHUNK),
        grid=(b, RT_HEADS),
        in_specs=[pl.BlockSpec(memory_space=pltpu.SMEM),
                  qk(nc), vv(nc), qk(n), qk(n), vv(n), vv(n),
                  pl.BlockSpec((1, RT_V), lambda bi, h: (0, h))],
        out_specs=vv(n),
        out_shape=jax.ShapeDtypeStruct((b, n, RT_HEADS * RT_V), BF),
        scratch_shapes=[pltpu.VMEM((RT_QK, RT_V), F32), pltpu.VMEM((n, RT_V), F32)],
        compiler_params=_params(2),
        name="retention",
    )(decay_logit, kc, vc, ql, kl, vl, g, gn_g)


def _sc_kernel(xp_ref, x_ref, xn_ref, mod_ref, wi_ref, cw_ref, wo_ref, g_ref, b_ref, o_ref, *, tm, rpb):
    x = x_ref[...]
    xe = jnp.concatenate([xp_ref[...], x, xn_ref[...]], axis=0)
    h = _modulated(xe, mod_ref, 3)
    valid = _halo_valid(tm, rpb)
    cu = _dot(h, wi_ref[:, D:2 * D]) * _dot(h, wi_ref[:, 2 * D:3 * D])
    conv = _dwconv3_rows(jnp.where(valid, cu, 0.0), cw_ref[...])
    b_gate = _dot(h[HALO:HALO + tm, :], wi_ref[:, 0:D])
    y = _dot((b_gate * conv).astype(BF), wo_ref[...])
    r = ALPHA * x + mod_ref[5:6, :] * y
    o_ref[...] = _layer_norm(r, g_ref[...], b_ref[...])


def _short_conv(x2, mod, bsel, wi, conv_w, wo, g, b, seq, tm):
    t = x2.shape[0]
    return pl.pallas_call(
        functools.partial(_sc_kernel, tm=tm, rpb=seq // tm),
        grid=(t // tm,),
        in_specs=_halo_specs(tm, t) + [
            pl.BlockSpec((None, N_MOD, D), lambda i: (bsel(i), 0, 0)),
            _resident((D, 3 * D)), _resident((3, D)), _resident((D, D)),
            _resident((1, D)), _resident((1, D))],
        out_specs=pl.BlockSpec((tm, D), lambda i: (i, 0)),
        out_shape=jax.ShapeDtypeStruct((t, D), F32),
        compiler_params=_params(1),
        name="short_conv",
    )(x2, x2, x2, mod, wi, conv_w, wo, g, b)


def kernel(x, c, ctx, c_ctx, ada_w, ada_b, ln_g, ln_b, ffa_wi, ffa_wo, ffb_wi, ffb_wo, da_w_qkv, da_w_o, da_lambda, da_subln_g, hy_w_in, hy_conv_w, hy_conv_b, hy_fw1, hy_fb1, hy_ff1, hy_fw2, hy_fb2, hy_ff2, hy_fw3, hy_d_skip, hy_w_o, rt_w_in, rt_decay_logit, rt_gn_g, rt_w_o, sc_w_in, sc_conv_w, sc_w_o):
    bsz, seq, _ = x.shape
    n_ctx = ctx.shape[1]
    assert x.shape[2] == D and ada_w.shape[0] == DEPTH and seq % GRID_W == 0
    tm = min(512, seq)
    tmc = min(512, n_ctx)
    assert seq % tm == 0 and n_ctx % tmc == 0 and seq % RT_CHUNK == 0 and n_ctx % RT_CHUNK == 0

    n_rows = -(-(bsz + 1) // HALO) * HALO
    cond = jnp.zeros((n_rows, D), F32).at[:bsz].set(c).at[bsz].set(c_ctx)
    mods = _modulation_all(cond, ada_w, ada_b)

    rpb = seq // tm
    lat_sel = lambda i: i // rpb
    ctx_sel = lambda i: bsz

    xl = x.reshape(bsz * seq, D)
    xc = ctx.reshape(bsz * n_ctx, D)
    lng = lambda i, j: ln_g[i, j].reshape(1, D)
    lnb = lambda i, j: ln_b[i, j].reshape(1, D)

    def ffn(xx, i, which, sel, tile):
        wi, wo = (ffa_wi, ffa_wo) if which == 0 else (ffb_wi, ffb_wo)
        return _half_ffn(xx, mods[i], sel, wi[i].astype(BF), wo[i].astype(BF),
                         lng(i, 2 * which), lnb(i, 2 * which), 6 * which, tile)

    i = 0
    xl = ffn(xl, i, 0, lat_sel, tm)
    xc = ffn(xc, i, 0, ctx_sel, tmc)
    w_qkv = da_w_qkv[0].astype(BF)
    ql, kl, vl = _proj_da(xl, mods[i], lat_sel, w_qkv, _da_rope_tables(seq), seq, tm)
    qc, kc, vc = _proj_da(xc, mods[i], ctx_sel, w_qkv, None, n_ctx, tmc)
    r3 = lambda a, n: a.reshape(bsz, n, D)
    k_all = jnp.concatenate([r3(kc, n_ctx), r3(kl, seq)], axis=1)
    v_all = jnp.concatenate([r3(vc, n_ctx), r3(vl, seq)], axis=1)
    lam_init = 0.8 - 0.6 * math.exp(-0.3 * i)
    subln = da_subln_g[0].reshape(1, LANES)
    ol = _diff_attn(r3(ql, seq), k_all, v_all, da_lambda[0], subln, lam_init, min(512, seq))
    oc = _diff_attn(r3(qc, n_ctx), r3(kc, n_ctx), r3(vc, n_ctx), da_lambda[0], subln, lam_init,
                    min(512, n_ctx))
    w_o = da_w_o[0].astype(BF)
    xl = _outproj(ol.reshape(bsz * seq, D), w_o, xl, mods[i], lat_sel, lng(i, 1), lnb(i, 1), tm)
    xc = _outproj(oc.reshape(bsz * n_ctx, D), w_o, xc, mods[i], ctx_sel, lng(i, 1), lnb(i, 1), tmc)
    xl = ffn(xl, i, 1, lat_sel, tm)
    xc = ffn(xc, i, 1, ctx_sel, tmc)

    i = 1
    xl = ffn(xl, i, 0, lat_sel, tm)
    xc = ffn(xc, i, 0, ctx_sel, tmc)
    w_in = hy_w_in[0].astype(BF)
    w_o = hy_w_o[0].astype(BF)
    conv_b = hy_conv_b[0].reshape(1, 3 * D)
    d_skip = hy_d_skip[0].reshape(1, D)
    tc = 256

    def hyena(xx, sel, n, tile):
        tabs = _dft_tables(n)
        p, q, hn = _hy_filter(n, tabs, hy_fw1[0], hy_fb1[0], hy_ff1[0], hy_fw2[0], hy_fb2[0],
                              hy_ff2[0], hy_fw3[0], tc)
        x0, z = _proj_hy(xx, mods[i], sel, w_in, hy_conv_w[0], conv_b, n, tile)
        y = _hy_conv(z.reshape(bsz, n, D), x0.reshape(bsz, n, D), p, q, hn, d_skip, tabs, tc)
        return _outproj(y.reshape(bsz * n, D), w_o, xx, mods[i], sel, lng(i, 1), lnb(i, 1), tile)

    xl = hyena(xl, lat_sel, seq, tm)
    xc = hyena(xc, ctx_sel, n_ctx, tmc)
    xl = ffn(xl, i, 1, lat_sel, tm)
    xc = ffn(xc, i, 1, ctx_sel, tmc)

    i = 2
    xl = ffn(xl, i, 0, lat_sel, tm)
    xc = ffn(xc, i, 0, ctx_sel, tmc)
    w_in = rt_w_in[0].astype(BF)
    ql, kl, vl, gl = _proj_rt(xl, mods[i], lat_sel, w_in, _rt_rope_tables(seq), seq, tm)
    kc, vc = _proj_rt(xc, mods[i], ctx_sel, w_in, None, n_ctx, tmc)
    qk_w, v_w = RT_HEADS * RT_QK, RT_HEADS * RT_V
    o = _retention(kc.reshape(bsz, n_ctx, qk_w), vc.reshape(bsz, n_ctx, v_w),
                   ql.reshape(bsz, seq, qk_w), kl.reshape(bsz, seq, qk_w),
                   vl.reshape(bsz, seq, v_w), gl.reshape(bsz, seq, v_w),
                   rt_decay_logit[0], rt_gn_g[0].reshape(1, v_w))
    xl = _outproj(o.reshape(bsz * seq, v_w), rt_w_o[0].astype(BF), xl, mods[i], lat_sel,
                  lng(i, 1), lnb(i, 1), tm)
    xl = ffn(xl, i, 1, lat_sel, tm)

    i = 3
    xl = ffn(xl, i, 0, lat_sel, tm)
    xl = _short_conv(xl, mods[i], lat_sel, sc_w_in[0].astype(BF), sc_conv_w[0],
                     sc_w_o[0].astype(BF), lng(i, 1), lnb(i, 1), seq, tm)
    xl = ffn(xl, i, 1, lat_sel, tm)
    return xl.reshape(bsz, seq, D)
```

```python
import functools
import math

import jax
import jax.numpy as jnp
import numpy as np
from jax import lax
from jax.experimental import pallas as pl
from jax.experimental.pallas import tpu as pltpu

D = 1024
N_MOD = 9
D_FF = 2816
LN_EPS = 1e-5
ROPE_BASE = 10000.0
GRID_W = 64
DEPTH = 4
ALPHA = (2.0 * DEPTH) ** 0.25
DA_HEADS = 8
DA_HEAD_DIM = 64
RT_HEADS = 4
RT_QK = 256
RT_V = 512
HY_EMB = 33
HY_FH = 64
LANES = 128
HALO = 8
FF_CHUNK = 256
RT_CHUNK = 256
ATTN_TILE = 1024
ATTN_PART = 512
VMEM_LIMIT = 56 * 1024 * 1024
LOG2E = math.log2(math.e)

F32 = jnp.float32
BF = jnp.bfloat16
NT = (((1,), (1,)), ((), ()))
TN = (((0,), (0,)), ((), ()))


def _dot(a, b):
    return jnp.dot(a, b, preferred_element_type=F32)


def _resident(shape):
    nd = len(shape)
    return pl.BlockSpec(shape, lambda *_: (0,) * nd, pipeline_mode=pl.Buffered(1))


def _layer_spec(shape, layer):
    nd = len(shape)
    return pl.BlockSpec((None,) + tuple(shape), lambda *_: (layer,) + (0,) * nd,
                        pipeline_mode=pl.Buffered(1))


def _mod_spec(layer, sel):
    return pl.BlockSpec((None, None, N_MOD, D), lambda i: (layer, sel(i), 0, 0))


def _rows(tm, width):
    return pl.BlockSpec((tm, width), lambda i: (i, 0))


def _params(n_axes):
    return pltpu.CompilerParams(dimension_semantics=("parallel",) * n_axes,
                                vmem_limit_bytes=VMEM_LIMIT)


def _layer_norm(r, g, b):
    mu = jnp.mean(r, -1, keepdims=True)
    d = r - mu
    var = jnp.mean(d * d, -1, keepdims=True)
    return d * lax.rsqrt(var + LN_EPS) * g + b


def _silu(a):
    return a * jax.nn.sigmoid(a)


def _modulated(x, mod_ref, j):
    return (x * (1.0 + mod_ref[j + 1:j + 2, :]) + mod_ref[j:j + 1, :]).astype(BF)


def _mod_kernel(c_ref, w_ref, b_ref, o_ref):
    s = _silu(c_ref[...]).astype(BF)
    o_ref[...] = _dot(s, w_ref[...].astype(BF)) + b_ref[...]


def _modulation_all(cond, ada_w, ada_b):
    r = cond.shape[0]
    tn = 2304
    n = N_MOD * D
    out = pl.pallas_call(
        _mod_kernel,
        grid=(DEPTH, n // tn),
        in_specs=[pl.BlockSpec((r, D), lambda i, j: (0, 0)),
                  pl.BlockSpec((None, D, tn), lambda i, j: (i, 0, j)),
                  pl.BlockSpec((None, 1, tn), lambda i, j: (i, 0, j))],
        out_specs=pl.BlockSpec((None, r, tn), lambda i, j: (i, 0, j)),
        out_shape=jax.ShapeDtypeStruct((DEPTH, r, n), F32),
        compiler_params=_params(2),
        name="modulation",
    )(cond, ada_w, ada_b.reshape(DEPTH, 1, n))
    return out.reshape(DEPTH, r, N_MOD, D)


def _ffn_body(x, mod_ref, j0, wi_ref, wo_ref, g_ref, b_ref, act_ref):
    h = _modulated(x, mod_ref, j0)
    for c in range(D_FF // FF_CHUNK):
        lo = c * FF_CHUNK
        a = _dot(h, wi_ref[:, lo:lo + FF_CHUNK])
        u = _dot(h, wi_ref[:, D_FF + lo:D_FF + lo + FF_CHUNK])
        act_ref[:, lo:lo + FF_CHUNK] = (_silu(a) * u).astype(BF)
    y = _dot(act_ref[...], wo_ref[...])
    r = ALPHA * x + (0.5 * mod_ref[j0 + 2:j0 + 3, :]) * y
    return _layer_norm(r, g_ref[...], b_ref[...])


def _ffn_kernel(x_ref, mod_ref, wi_ref, wo_ref, g_ref, b_ref, o_ref, act_ref, *, j0):
    o_ref[...] = _ffn_body(x_ref[...], mod_ref, j0, wi_ref, wo_ref, g_ref, b_ref, act_ref)


def _ffn_specs(layer, which):
    return [_layer_spec((D, 2 * D_FF), layer), _layer_spec((D_FF, D), layer),
            _layer_spec((1, D), 3 * layer + 2 * which), _layer_spec((1, D), 3 * layer + 2 * which)]


def _half_ffn(x2, mods, layer, which, sel, wi, wo, lng, lnb, tm):
    t = x2.shape[0]
    return pl.pallas_call(
        functools.partial(_ffn_kernel, j0=6 * which),
        grid=(t // tm,),
        in_specs=[_rows(tm, D), _mod_spec(layer, sel)] + _ffn_specs(layer, which),
        out_specs=_rows(tm, D),
        out_shape=jax.ShapeDtypeStruct((t, D), F32),
        scratch_shapes=[pltpu.VMEM((tm, D_FF), BF)],
        compiler_params=_params(1),
        name="half_ffn",
    )(x2, mods, wi, wo, lng, lnb)


def _outproj_ffn_kernel(a_ref, w_ref, x_ref, mod_ref, g1_ref, b1_ref, wi_ref, wo_ref, g2_ref, b2_ref,
                        o_ref, act_ref):
    y = _dot(a_ref[...], w_ref[...])
    x1 = _layer_norm(ALPHA * x_ref[...] + mod_ref[5:6, :] * y, g1_ref[...], b1_ref[...])
    o_ref[...] = _ffn_body(x1, mod_ref, 6, wi_ref, wo_ref, g2_ref, b2_ref, act_ref)


def _outproj_ffn(a2, w, x2, mods, layer, sel, wi, wo, lng, lnb, tm):
    t, k = a2.shape
    return pl.pallas_call(
        _outproj_ffn_kernel,
        grid=(t // tm,),
        in_specs=[_rows(tm, k), _layer_spec((k, D), 0), _rows(tm, D), _mod_spec(layer, sel),
                  _layer_spec((1, D), 3 * layer + 1), _layer_spec((1, D), 3 * layer + 1)]
                 + _ffn_specs(layer, 1),
        out_specs=_rows(tm, D),
        out_shape=jax.ShapeDtypeStruct((t, D), F32),
        scratch_shapes=[pltpu.VMEM((tm, D_FF), BF)],
        compiler_params=_params(1),
        name="outproj_ffn",
    )(a2, w, x2, mods, lng, lnb, wi, wo, lng, lnb)


def _rope_angles(n_tokens, dim):
    rows = n_tokens // GRID_W
    row = np.repeat(np.arange(rows), GRID_W).astype(np.float32)
    col = np.tile(np.arange(GRID_W), rows).astype(np.float32)
    n_freq = dim // 4
    inv = (ROPE_BASE ** (-np.arange(n_freq, dtype=np.float32) / n_freq)).astype(np.float32)
    return np.concatenate([row[:, None] * inv, col[:, None] * inv], axis=-1)


def _da_rope_tables(n_tokens):
    ang = _rope_angles(n_tokens, DA_HEAD_DIM)
    cos, sin = np.cos(ang), np.sin(ang)
    cos128 = np.tile(cos, (1, 4))
    sin128 = np.tile(np.concatenate([-sin, sin], axis=-1), (1, 2))
    return jnp.asarray(cos128, F32), jnp.asarray(sin128, F32)


def _rt_rope_tables(n_tokens):
    ang = _rope_angles(n_tokens, RT_QK)
    return jnp.asarray(np.cos(ang), F32), jnp.asarray(np.sin(ang), F32)


def _proj_da_kernel(*refs, rope):
    if rope:
        x_ref, mod_ref, w_ref, cos_ref, sin_ref, q_ref, k_ref, v_ref = refs
        cos, sin = cos_ref[...], sin_ref[...]
        lane = lax.broadcasted_iota(jnp.int32, (1, LANES), 1)
        first_half = (lane % DA_HEAD_DIM) < (DA_HEAD_DIM // 2)
    else:
        x_ref, mod_ref, w_ref, q_ref, k_ref, v_ref = refs
    h = _modulated(x_ref[...], mod_ref, 3)

    def rot(y):
        if not rope:
            return y
        partner = jnp.where(first_half, pltpu.roll(y, LANES - 32, 1), pltpu.roll(y, 32, 1))
        return y * cos + partner * sin

    q = _dot(h, w_ref[:, 0:D]) * (DA_HEAD_DIM ** -0.5 * LOG2E)
    for j in range(D // LANES):
        q_ref[:, j * LANES:(j + 1) * LANES] = rot(q[:, j * LANES:(j + 1) * LANES]).astype(BF)
    k = _dot(h, w_ref[:, D:2 * D])
    for j in range(D // LANES):
        k_ref[:, j * LANES:(j + 1) * LANES] = rot(k[:, j * LANES:(j + 1) * LANES]).astype(BF)
    v_ref[...] = _dot(h, w_ref[:, 2 * D:3 * D]).T.astype(BF)


def _proj_da(x2, mods, layer, sel, w, rope_tabs, seq, tm):
    t = x2.shape[0]
    rpb = seq // tm
    rope = rope_tabs is not None
    in_specs = [_rows(tm, D), _mod_spec(layer, sel), _layer_spec((D, 3 * D), 0)]
    args = [x2, mods, w]
    if rope:
        in_specs += [pl.BlockSpec((tm, LANES), lambda i: (i % rpb, 0))] * 2
        args += list(rope_tabs)
    return pl.pallas_call(
        functools.partial(_proj_da_kernel, rope=rope),
        grid=(t // tm,),
        in_specs=in_specs,
        out_specs=[_rows(tm, D), _rows(tm, D),
                   pl.BlockSpec((None, D, tm), lambda i: (i // rpb, 0, i % rpb))],
        out_shape=[jax.ShapeDtypeStruct((t, D), BF), jax.ShapeDtypeStruct((t, D), BF),
                   jax.ShapeDtypeStruct((t // seq, D, seq), BF)],
        compiler_params=_params(1),
        name="proj_da",
    )(*args)


def _attn_kernel(*refs, lam_init, n_kv, parts):
    lam_ref, g_ref, q_ref = refs[:3]
    kv_refs = [(refs[3 + 2 * j], refs[4 + 2 * j]) for j in range(n_kv)]
    o_ref = refs[3 + 2 * n_kv]
    width = q_ref.shape[0] // parts
    lane = lax.broadcasted_iota(jnp.int32, (width, LANES), 1)

    def scores(part, head_map):
        q = q_ref[part * width:(part + 1) * width, :]
        keep = (lane < DA_HEAD_DIM) if head_map == 0 else (lane >= DA_HEAD_DIM)
        qm = jnp.where(keep, q, jnp.zeros_like(q))
        return [lax.dot_general(k_ref[...], qm, NT, preferred_element_type=F32) for k_ref, _ in kv_refs]

    def attend(ss):
        m = functools.reduce(jnp.maximum, [jnp.max(s, 0, keepdims=True) for s in ss])
        es = [jnp.exp2(s - m) for s in ss]
        den = functools.reduce(jnp.add, [jnp.sum(e, 0, keepdims=True) for e in es])
        o = functools.reduce(jnp.add, [_dot(vt_ref[...], e.astype(BF))
                                       for e, (_, vt_ref) in zip(es, kv_refs)])
        return o / den

    chains = [(part, head_map) for part in range(parts) for head_map in range(2)]
    result, pending = {}, None
    for chain in chains:
        ss = scores(*chain)
        if pending is not None:
            result[pending[0]] = attend(pending[1])
        pending = (chain, ss)
    result[pending[0]] = attend(pending[1])

    lam = lam_ref[...]
    lam_full = (jnp.exp(jnp.sum(lam[0:1] * lam[1:2], -1, keepdims=True))
                - jnp.exp(jnp.sum(lam[2:3] * lam[3:4], -1, keepdims=True)) + lam_init)
    for part in range(parts):
        o = result[(part, 0)] - lam_full * result[(part, 1)]
        o = o * lax.rsqrt(jnp.mean(o * o, 0, keepdims=True) + LN_EPS)
        o_ref[part * width:(part + 1) * width, :] = (o.T * g_ref[...] * (1.0 - lam_init)).astype(BF)


def _diff_attn(q, kvs, lam, subln_g, lam_init, tq):
    b, n, _ = q.shape
    in_specs = [_layer_spec((4, DA_HEAD_DIM), 0), _layer_spec((1, LANES), 0),
                pl.BlockSpec((None, tq, LANES), lambda bi, h, i: (bi, i, h))]
    args = [lam, subln_g, q]
    for k, vt in kvs:
        n_k = k.shape[1]
        in_specs += [pl.BlockSpec((None, n_k, LANES), lambda bi, h, i: (bi, 0, h)),
                     pl.BlockSpec((None, LANES, n_k), lambda bi, h, i: (bi, h, 0))]
        args += [k, vt]
    return pl.pallas_call(
        functools.partial(_attn_kernel, lam_init=lam_init, n_kv=len(kvs),
                          parts=max(1, tq // ATTN_PART)),
        grid=(b, DA_HEADS, n // tq),
        in_specs=in_specs,
        out_specs=pl.BlockSpec((None, tq, LANES), lambda bi, h, i: (bi, i, h)),
        out_shape=jax.ShapeDtypeStruct((b, n, D), BF),
        compiler_params=_params(3),
        name="diff_attn",
    )(*args)


def _halo_specs(tm, t):
    blocks_per_tile = tm // HALO
    last = t // HALO - 1
    return [pl.BlockSpec((HALO, D), lambda i: (jnp.maximum(i * blocks_per_tile - 1, 0), 0)),
            _rows(tm, D),
            pl.BlockSpec((HALO, D), lambda i: (jnp.minimum((i + 1) * blocks_per_tile, last), 0))]


def _halo_valid(tm, rpb):
    pos = pl.program_id(0) % rpb
    row = lax.broadcasted_iota(jnp.int32, (tm + 2 * HALO, 1), 0)
    outside = ((pos == 0) & (row < HALO)) | ((pos == rpb - 1) & (row >= tm + HALO))
    return jnp.logical_not(outside)


def _dwconv3_rows(u, w):
    n = u.shape[0]
    y = pltpu.roll(u, 1, 0) * w[0:1, :] + u * w[1:2, :] + pltpu.roll(u, n - 1, 0) * w[2:3, :]
    return y[HALO:n - HALO, :]


def _proj_hy_kernel(xp_ref, x_ref, xn_ref, mod_ref, w_ref, cw_ref, cb_ref, x0_ref, z_ref, *, tm, rpb):
    xe = jnp.concatenate([xp_ref[...], x_ref[...], xn_ref[...]], axis=0)
    h = _modulated(xe, mod_ref, 3)
    valid = _halo_valid(tm, rpb)

    def branch(j):
        u = jnp.where(valid, _dot(h, w_ref[:, j * D:(j + 1) * D]), 0.0)
        return _dwconv3_rows(u, cw_ref[:, j * D:(j + 1) * D]) + cb_ref[:, j * D:(j + 1) * D]

    x0_ref[...] = branch(0)
    z_ref[...] = branch(1) * branch(2)


def _proj_hy(x2, mods, layer, sel, w, conv_w, conv_b, seq, tm):
    t = x2.shape[0]
    return pl.pallas_call(
        functools.partial(_proj_hy_kernel, tm=tm, rpb=seq // tm),
        grid=(t // tm,),
        in_specs=_halo_specs(tm, t) + [
            _mod_spec(layer, sel), _layer_spec((D, 3 * D), 0), _layer_spec((3, 3 * D), 0),
            _layer_spec((1, 3 * D), 0)],
        out_specs=[_rows(tm, D)] * 2,
        out_shape=[jax.ShapeDtypeStruct((t, D), F32)] * 2,
        compiler_params=_params(1),
        name="proj_hy",
    )(x2, x2, x2, mods, w, conv_w, conv_b)


def _dft_tables(n):
    k = np.arange(n, dtype=np.int64)
    ph = (k[:, None] * k[None, :]) % (2 * n)
    ang = ph.astype(np.float64) * (math.pi / n)
    cos = np.cos(ang).astype(np.float32)
    sin = np.sin(ang).astype(np.float32)
    tail = np.zeros((HALO, n), np.float32)
    tail[0] = 1.0 - 2.0 * (k % 2)
    return (jnp.asarray(np.concatenate([cos, tail], 0)).astype(BF), jnp.asarray(sin).astype(BF))


def _hy_features(n):
    t = np.linspace(0.0, 1.0, n, dtype=np.float32)[:, None]
    bands = (HY_EMB - 1) // 2
    fr = np.linspace(1e-4, bands - 1, bands, dtype=np.float32)[None, :]
    w = (2.0 * math.pi * np.arange(n, dtype=np.float32)[:, None] / n).astype(np.float32)
    z = np.concatenate([t, np.cos(fr * w), -np.sin(fr * w)], axis=-1).astype(np.float32)
    return jnp.asarray(np.pad(z, ((0, 0), (0, LANES - HY_EMB))), F32)


def _hy_deltas():
    max_decay = math.log(1e-2) / 0.3
    min_decay = math.log(1e-2) / 1.5
    return jnp.asarray(np.abs(np.linspace(min_decay, max_decay, D, dtype=np.float32))[None, :], F32)


def _hy_filter_kernel(z_ref, w1_ref, b1_ref, f1_ref, w2_ref, b2_ref, f2_ref, w3f_ref, w3b_ref,
                      dl_ref, cos_ref, sin_ref, p_ref, q_ref, hn_ref):
    n = z_ref.shape[0]
    z = z_ref[...]
    h = jnp.sin(f1_ref[...] * (_dot(z.astype(BF), w1_ref[...].astype(BF)) + b1_ref[...]))
    h = jnp.sin(f2_ref[...] * (_dot(h.astype(BF), w2_ref[...].astype(BF)) + b2_ref[...]))
    hb16 = h.astype(BF)
    decay = jnp.exp(-z[:, 0:1] * dl_ref[...])
    h_f = _dot(hb16, w3f_ref[...].astype(BF)) * decay
    h_b = _dot(hb16, w3b_ref[...].astype(BF)) * decay
    row = lax.broadcasted_iota(jnp.int32, h_f.shape, 0)
    h_b = jnp.where(row == 0, 0.0, h_b)
    a = _dot(cos_ref[...], (h_f + h_b).astype(BF))
    bm = _dot(sin_ref[...], (h_f - h_b).astype(BF))
    wgt = jnp.where(row == 0, 1.0, 2.0) * (0.5 / n)
    p_ref[...] = a[0:n, :] * wgt
    q_ref[...] = -bm * wgt
    hn_ref[...] = a[n:n + HALO, :] * (0.5 / n)


def _hy_filter(n, tabs, fw1, fb1, ff1, fw2, fb2, ff2, fw3, tc):
    cos_t, sin_t = tabs
    w1 = jnp.pad(fw1, ((0, LANES - HY_EMB), (0, 0)))
    nt = D // tc
    small = lambda s: pl.BlockSpec(s, lambda j: (0, 0))
    return pl.pallas_call(
        _hy_filter_kernel,
        grid=(nt,),
        in_specs=[small((n, LANES)), small((LANES, HY_FH)), small((1, HY_FH)), small((1, HY_FH)),
                  small((HY_FH, HY_FH)), small((1, HY_FH)), small((1, HY_FH)),
                  pl.BlockSpec((HY_FH, tc), lambda j: (0, j)),
                  pl.BlockSpec((HY_FH, tc), lambda j: (0, j + nt)),
                  pl.BlockSpec((1, tc), lambda j: (0, j)),
                  _resident((n + HALO, n)), _resident((n, n))],
        out_specs=[pl.BlockSpec((n, tc), lambda j: (0, j)),
                   pl.BlockSpec((n, tc), lambda j: (0, j)),
                   pl.BlockSpec((HALO, tc), lambda j: (0, j))],
        out_shape=[jax.ShapeDtypeStruct((n, D), F32), jax.ShapeDtypeStruct((n, D), F32),
                   jax.ShapeDtypeStruct((HALO, D), F32)],
        compiler_params=_params(1),
        name="hy_filter",
    )(_hy_features(n), w1, fb1.reshape(1, -1), ff1.reshape(1, -1), fw2, fb2.reshape(1, -1),
      ff2.reshape(1, -1), fw3, fw3, _hy_deltas(), cos_t, sin_t)


def _hy_conv_kernel(z_ref, x0_ref, p_ref, q_ref, hn_ref, ds_ref, cos_ref, sin_ref, o_ref):
    n = z_ref.shape[0]
    z = z_ref[...]
    zb = z.astype(BF)
    a_ext = _dot(cos_ref[...], zb)
    bm = _dot(sin_ref[...], zb)
    a = a_ext[0:n, :]
    p, q = p_ref[...], q_ref[...]
    yr = (a * p + bm * q).astype(BF)
    yi = (a * q - bm * p).astype(BF)
    y = _dot(cos_ref[0:n, :], yr) - _dot(sin_ref[...], yi)
    row = lax.broadcasted_iota(jnp.int32, y.shape, 0)
    sign = (1 - 2 * (row % 2)).astype(F32)
    y = y + sign * (a_ext[n:n + 1, :] * hn_ref[0:1, :])
    o_ref[...] = (x0_ref[...] * (y + z * ds_ref[...])).astype(BF)


def _hy_conv(z, x0, p, q, hn, d_skip, tabs, tc):
    b, n, _ = z.shape
    cos_t, sin_t = tabs
    return pl.pallas_call(
        _hy_conv_kernel,
        grid=(D // tc, b),
        in_specs=[pl.BlockSpec((None, n, tc), lambda j, bi: (bi, 0, j)),
                  pl.BlockSpec((None, n, tc), lambda j, bi: (bi, 0, j)),
                  pl.BlockSpec((n, tc), lambda j, bi: (0, j)),
                  pl.BlockSpec((n, tc), lambda j, bi: (0, j)),
                  pl.BlockSpec((HALO, tc), lambda j, bi: (0, j)),
                  pl.BlockSpec((None, 1, tc), lambda j, bi: (0, 0, j)),
                  _resident((n + HALO, n)), _resident((n, n))],
        out_specs=pl.BlockSpec((None, n, tc), lambda j, bi: (bi, 0, j)),
        out_shape=jax.ShapeDtypeStruct((b, n, D), BF),
        compiler_params=_params(2),
        name="hy_conv",
    )(z, x0, p, q, hn, d_skip, cos_t, sin_t)


def _proj_rt_kernel(*refs, rope):
    if rope:
        x_ref, mod_ref, w_ref, cos_ref, sin_ref, q_ref, k_ref, v_ref, g_ref = refs
        cos, sin = cos_ref[...], sin_ref[...]
    else:
        x_ref, mod_ref, w_ref, k_ref, v_ref = refs
    h = _modulated(x_ref[...], mod_ref, 3)
    qk_w = RT_HEADS * RT_QK
    half = RT_QK // 2

    def store_rot(y, ref):
        for hd in range(RT_HEADS):
            lo = hd * RT_QK
            x1, x2 = y[:, lo:lo + half], y[:, lo + half:lo + RT_QK]
            if rope:
                x1, x2 = x1 * cos - x2 * sin, x1 * sin + x2 * cos
            ref[:, lo:lo + half] = x1.astype(BF)
            ref[:, lo + half:lo + RT_QK] = x2.astype(BF)

    if rope:
        store_rot(_dot(h, w_ref[:, 0:qk_w]), q_ref)
    store_rot(_dot(h, w_ref[:, qk_w:2 * qk_w]) * (RT_QK ** -0.5), k_ref)
    v_w = RT_HEADS * RT_V
    v_ref[...] = _dot(h, w_ref[:, 2 * qk_w:2 * qk_w + v_w]).astype(BF)
    if rope:
        g_ref[...] = _dot(h, w_ref[:, 2 * qk_w + v_w:2 * qk_w + 2 * v_w])


def _proj_rt(x2, mods, layer, sel, w, rope_tabs, seq, tm):
    t = x2.shape[0]
    rpb = seq // tm
    rope = rope_tabs is not None
    qk_w, v_w = RT_HEADS * RT_QK, RT_HEADS * RT_V
    in_specs = [_rows(tm, D), _mod_spec(layer, sel), _layer_spec((D, 2 * qk_w + 2 * v_w), 0)]
    args = [x2, mods, w]
    if rope:
        in_specs += [pl.BlockSpec((tm, LANES), lambda i: (i % rpb, 0))] * 2
        args += list(rope_tabs)
        out_specs = [_rows(tm, qk_w), _rows(tm, qk_w), _rows(tm, v_w), _rows(tm, v_w)]
        out_shape = [jax.ShapeDtypeStruct((t, qk_w), BF), jax.ShapeDtypeStruct((t, qk_w), BF),
                     jax.ShapeDtypeStruct((t, v_w), BF), jax.ShapeDtypeStruct((t, v_w), F32)]
    else:
        out_specs = [_rows(tm, qk_w), _rows(tm, v_w)]
        out_shape = [jax.ShapeDtypeStruct((t, qk_w), BF), jax.ShapeDtypeStruct((t, v_w), BF)]
    return pl.pallas_call(
        functools.partial(_proj_rt_kernel, rope=rope),
        grid=(t // tm,),
        in_specs=in_specs,
        out_specs=out_specs,
        out_shape=out_shape,
        compiler_params=_params(1),
        name="proj_rt",
    )(*args)


def _ret_kernel(logit_ref, kc_ref, vc_ref, ql_ref, kl_ref, vl_ref, g_ref, gn_ref, o_ref,
                s_ref, acc_ref, *, chunk):
    hd = pl.program_id(1)
    n_ctx = kc_ref.shape[0] // chunk
    n_lat = ql_ref.shape[0] // chunk
    row = lax.broadcasted_iota(jnp.int32, (chunk, chunk), 0)
    col = lax.broadcasted_iota(jnp.int32, (chunk, chunk), 1)
    pos = lax.broadcasted_iota(jnp.int32, (chunk, 1), 0).astype(F32)

    decays = []
    for direction in (0, 1):
        log_g = jnp.log(jax.nn.sigmoid(jnp.full((chunk, 1), logit_ref[direction, hd], F32)))
        lag = (row - col) if direction == 0 else (col - row)
        d_intra = jnp.where(lag >= 0, jnp.exp(jnp.maximum(lag, 0).astype(F32) * log_g), 0.0)
        if direction == 0:
            d_read = jnp.exp((pos + 1.0) * log_g)
            d_write = jnp.exp((chunk - 1.0 - pos) * log_g)
        else:
            d_read = jnp.exp((chunk - pos) * log_g)
            d_write = jnp.exp(pos * log_g)
        decays.append((d_intra, d_read, d_write, jnp.exp(chunk * log_g[0:1, :])))

    s_ref[...] = jnp.zeros_like(s_ref)

    def absorb(direction, k_ref, v_ref, rows):
        _, _, d_write, d_chunk = decays[direction]
        kw = (k_ref[rows, :].astype(F32) * d_write).astype(BF)
        s_ref[direction] = d_chunk * s_ref[direction] + lax.dot_general(
            kw, v_ref[rows, :], TN, preferred_element_type=F32)

    for direction in (0, 1):
        for c in (range(n_ctx) if direction == 0 else reversed(range(n_ctx))):
            absorb(direction, kc_ref, vc_ref, pl.ds(c * chunk, chunk))

    def body(t, carry):
        for direction in (0, 1):
            d_intra, d_read, _, _ = decays[direction]
            c = t if direction == 0 else n_lat - 1 - t
            rows = pl.ds(pl.multiple_of(c * chunk, chunk), chunk)
            qc = ql_ref[rows, :]
            scores = lax.dot_general(qc, kl_ref[rows, :], NT, preferred_element_type=F32) * d_intra
            acc_ref[direction, rows, :] = (_dot(scores.astype(BF), vl_ref[rows, :])
                                           + _dot(qc, s_ref[direction].astype(BF)) * d_read)
            absorb(direction, kl_ref, vl_ref, rows)
        return carry

    lax.fori_loop(0, n_lat, body, 0)

    o = acc_ref[0] + acc_ref[1]
    mu = jnp.mean(o, -1, keepdims=True)
    dlt = o - mu
    var = jnp.mean(dlt * dlt, -1, keepdims=True)
    o_ref[...] = (_silu(g_ref[...]) * (dlt * lax.rsqrt(var + LN_EPS) * gn_ref[...])).astype(BF)


def _retention(kc, vc, ql, kl, vl, g, decay_logit, gn_g):
    b, n, _ = ql.shape
    nc = kc.shape[1]
    qk = lambda m: pl.BlockSpec((None, m, RT_QK), lambda bi, h: (bi, 0, h))
    vv = lambda m: pl.BlockSpec((None, m, RT_V), lambda bi, h: (bi, 0, h))
    return pl.pallas_call(
        functools.partial(_ret_kernel, chunk=RT_CHUNK),
        grid=(b, RT_HEADS),
        in_specs=[pl.BlockSpec(memory_space=pltpu.SMEM),
                  qk(nc), vv(nc), qk(n), qk(n), vv(n), vv(n),
                  pl.BlockSpec((None, 1, RT_V), lambda bi, h: (0, 0, h))],
        out_specs=vv(n),
        out_shape=jax.ShapeDtypeStruct((b, n, RT_HEADS * RT_V), BF),
        scratch_shapes=[pltpu.VMEM((2, RT_QK, RT_V), F32), pltpu.VMEM((2, n, RT_V), F32)],
        compiler_params=_params(2),
        name="retention",
    )(decay_logit, kc, vc, ql, kl, vl, g, gn_g)


def _sc_ffn_kernel(xp_ref, x_ref, xn_ref, mod_ref, wi_ref, cw_ref, wo_ref, g1_ref, b1_ref,
                   fwi_ref, fwo_ref, g2_ref, b2_ref, o_ref, act_ref, *, tm, rpb):
    x = x_ref[...]
    xe = jnp.concatenate([xp_ref[...], x, xn_ref[...]], axis=0)
    h = _modulated(xe, mod_ref, 3)
    valid = _halo_valid(tm, rpb)
    cu = _dot(h, wi_ref[:, D:2 * D]) * _dot(h, wi_ref[:, 2 * D:3 * D])
    conv = _dwconv3_rows(jnp.where(valid, cu, 0.0), cw_ref[...])
    b_gate = _dot(h[HALO:HALO + tm, :], wi_ref[:, 0:D])
    y = _dot((b_gate * conv).astype(BF), wo_ref[...])
    x1 = _layer_norm(ALPHA * x + mod_ref[5:6, :] * y, g1_ref[...], b1_ref[...])
    o_ref[...] = _ffn_body(x1, mod_ref, 6, fwi_ref, fwo_ref, g2_ref, b2_ref, act_ref)


def _short_conv_ffn(x2, mods, layer, sel, wi, conv_w, wo, fwi, fwo, lng, lnb, seq, tm):
    t = x2.shape[0]
    return pl.pallas_call(
        functools.partial(_sc_ffn_kernel, tm=tm, rpb=seq // tm),
        grid=(t // tm,),
        in_specs=_halo_specs(tm, t) + [
            _mod_spec(layer, sel), _layer_spec((D, 3 * D), 0), _layer_spec((3, D), 0),
            _layer_spec((D, D), 0), _layer_spec((1, D), 3 * layer + 1),
            _layer_spec((1, D), 3 * layer + 1)] + _ffn_specs(layer, 1),
        out_specs=_rows(tm, D),
        out_shape=jax.ShapeDtypeStruct((t, D), F32),
        scratch_shapes=[pltpu.VMEM((tm, D_FF), BF)],
        compiler_params=_params(1),
        name="short_conv_ffn",
    )(x2, x2, x2, mods, wi, conv_w, wo, lng, lnb, fwi, fwo, lng, lnb)


def kernel(x, c, ctx, c_ctx, ada_w, ada_b, ln_g, ln_b, ffa_wi, ffa_wo, ffb_wi, ffb_wo, da_w_qkv, da_w_o, da_lambda, da_subln_g, hy_w_in, hy_conv_w, hy_conv_b, hy_fw1, hy_fb1, hy_ff1, hy_fw2, hy_fb2, hy_ff2, hy_fw3, hy_d_skip, hy_w_o, rt_w_in, rt_decay_logit, rt_gn_g, rt_w_o, sc_w_in, sc_conv_w, sc_w_o):
    bsz, seq, _ = x.shape
    n_ctx = ctx.shape[1]
    assert x.shape[2] == D and ada_w.shape[0] == DEPTH and seq % GRID_W == 0
    tm = min(512, seq)
    tmc = min(512, n_ctx)
    assert seq % tm == 0 and n_ctx % tmc == 0 and seq % RT_CHUNK == 0 and n_ctx % RT_CHUNK == 0

    n_rows = -(-(bsz + 1) // HALO) * HALO
    cond = jnp.zeros((n_rows, D), F32).at[:bsz].set(c).at[bsz].set(c_ctx)
    mods = _modulation_all(cond, ada_w, ada_b)

    rpb = seq // tm
    lat = (lambda i: i // rpb, seq, tm)
    cx = (lambda i: bsz, n_ctx, tmc)

    xl = x.reshape(bsz * seq, D)
    xc = ctx.reshape(bsz * n_ctx, D)
    lng = ln_g.reshape(DEPTH * 3, 1, D)
    lnb = ln_b.reshape(DEPTH * 3, 1, D)
    ffa = (ffa_wi.astype(BF), ffa_wo.astype(BF))
    ffb = (ffb_wi.astype(BF), ffb_wo.astype(BF))

    def ffn_a(xx, i, stream):
        return _half_ffn(xx, mods, i, 0, stream[0], ffa[0], ffa[1], lng, lnb, stream[2])

    def mix_out(a, w_o, xx, i, stream):
        return _outproj_ffn(a, w_o, xx, mods, i, stream[0], ffb[0], ffb[1], lng, lnb, stream[2])

    def by_sample(a, n):
        return a.reshape(bsz, n, a.shape[-1])

    i = 0
    xl, xc = ffn_a(xl, i, lat), ffn_a(xc, i, cx)
    w_qkv = da_w_qkv.astype(BF)
    ql, kl, vtl = _proj_da(xl, mods, i, lat[0], w_qkv, _da_rope_tables(seq), seq, tm)
    qc, kc, vtc = _proj_da(xc, mods, i, cx[0], w_qkv, None, n_ctx, tmc)
    ql, kl, qc, kc = by_sample(ql, seq), by_sample(kl, seq), by_sample(qc, n_ctx), by_sample(kc, n_ctx)
    lam_init = 0.8 - 0.6 * math.exp(-0.3 * i)
    subln = da_subln_g.reshape(-1, 1, LANES)
    ol = _diff_attn(ql, [(kc, vtc), (kl, vtl)], da_lambda, subln, lam_init, min(ATTN_TILE, seq))
    oc = _diff_attn(qc, [(kc, vtc)], da_lambda, subln, lam_init, min(ATTN_TILE, n_ctx))
    w_o = da_w_o.astype(BF)
    xl = mix_out(ol.reshape(bsz * seq, D), w_o, xl, i, lat)
    xc = mix_out(oc.reshape(bsz * n_ctx, D), w_o, xc, i, cx)

    i = 1
    xl, xc = ffn_a(xl, i, lat), ffn_a(xc, i, cx)
    w_in = hy_w_in.astype(BF)
    w_o = hy_w_o.astype(BF)
    conv_b = hy_conv_b.reshape(-1, 1, 3 * D)
    d_skip = hy_d_skip.reshape(-1, 1, D)
    tc = 256

    def hyena(xx, stream):
        sel, n, tile = stream
        tabs = _dft_tables(n)
        p, q, hn = _hy_filter(n, tabs, hy_fw1[0], hy_fb1[0], hy_ff1[0], hy_fw2[0], hy_fb2[0],
                              hy_ff2[0], hy_fw3[0], tc)
        x0, z = _proj_hy(xx, mods, i, sel, w_in, hy_conv_w, conv_b, n, tile)
        y = _hy_conv(by_sample(z, n), by_sample(x0, n), p, q, hn, d_skip, tabs, tc)
        return mix_out(y.reshape(bsz * n, D), w_o, xx, i, stream)

    xl, xc = hyena(xl, lat), hyena(xc, cx)

    i = 2
    xl, xc = ffn_a(xl, i, lat), ffn_a(xc, i, cx)
    w_in = rt_w_in.astype(BF)
    ql, kl, vl, gl = [by_sample(a, seq) for a in
                      _proj_rt(xl, mods, i, lat[0], w_in, _rt_rope_tables(seq), seq, tm)]
    kc, vc = [by_sample(a, n_ctx) for a in _proj_rt(xc, mods, i, cx[0], w_in, None, n_ctx, tmc)]
    o = _retention(kc, vc, ql, kl, vl, gl, rt_decay_logit[0], rt_gn_g.reshape(-1, 1, RT_HEADS * RT_V))
    xl = mix_out(o.reshape(bsz * seq, RT_HEADS * RT_V), rt_w_o.astype(BF), xl, i, lat)

    i = 3
    xl = ffn_a(xl, i, lat)
    xl = _short_conv_ffn(xl, mods, i, lat[0], sc_w_in.astype(BF), sc_conv_w, sc_w_o.astype(BF),
                         ffb[0], ffb[1], lng, lnb, seq, tm)
    return xl.reshape(bsz, seq, D)
```

```python
import functools
import math

import jax
import jax.numpy as jnp
import numpy as np
from jax import lax
from jax.experimental import pallas as pl
from jax.experimental.pallas import tpu as pltpu

D = 1024
N_MOD = 9
D_FF = 2816
LN_EPS = 1e-5
ROPE_BASE = 10000.0
GRID_W = 64
DEPTH = 4
ALPHA = (2.0 * DEPTH) ** 0.25
DA_HEADS = 8
DA_HEAD_DIM = 64
RT_HEADS = 4
RT_QK = 256
RT_V = 512
HY_EMB = 33
HY_FH = 64
LANES = 128
HALO = 8
FF_CHUNK = 256
RT_CHUNK = 256
FFN_TILE = 1024
FFN_SLAB = 256
ATTN_TILE = 2048
ATTN_PART = 512
VMEM_LIMIT = 56 * 1024 * 1024
LOG2E = math.log2(math.e)

F32 = jnp.float32
BF = jnp.bfloat16
NT = (((1,), (1,)), ((), ()))
TN = (((0,), (0,)), ((), ()))


def _dot(a, b):
    return jnp.dot(a, b, preferred_element_type=F32)


def _resident(shape):
    nd = len(shape)
    return pl.BlockSpec(shape, lambda *_: (0,) * nd, pipeline_mode=pl.Buffered(1))


def _layer_spec(shape, layer):
    nd = len(shape)
    return pl.BlockSpec((None,) + tuple(shape), lambda *_: (layer,) + (0,) * nd,
                        pipeline_mode=pl.Buffered(1))


def _mod_spec(layer, sel):
    return pl.BlockSpec((None, None, N_MOD, D), lambda i: (layer, sel(i), 0, 0))


def _rows(tm, width):
    return pl.BlockSpec((tm, width), lambda i: (i, 0))


def _params(n_axes):
    return pltpu.CompilerParams(dimension_semantics=("parallel",) * n_axes,
                                vmem_limit_bytes=VMEM_LIMIT)


def _layer_norm(r, g, b):
    mu = jnp.mean(r, -1, keepdims=True)
    d = r - mu
    var = jnp.mean(d * d, -1, keepdims=True)
    return d * lax.rsqrt(var + LN_EPS) * g + b


def _silu(a):
    return a * jax.nn.sigmoid(a)


def _modulated(x, mod_ref, j):
    return (x * (1.0 + mod_ref[j + 1:j + 2, :]) + mod_ref[j:j + 1, :]).astype(BF)


def _mod_kernel(c_ref, w_ref, b_ref, o_ref):
    s = _silu(c_ref[...]).astype(BF)
    o_ref[...] = _dot(s, w_ref[...].astype(BF)) + b_ref[...]


def _modulation_all(cond, ada_w, ada_b):
    r = cond.shape[0]
    tn = 2304
    n = N_MOD * D
    out = pl.pallas_call(
        _mod_kernel,
        grid=(DEPTH, n // tn),
        in_specs=[pl.BlockSpec((r, D), lambda i, j: (0, 0)),
                  pl.BlockSpec((None, D, tn), lambda i, j: (i, 0, j)),
                  pl.BlockSpec((None, 1, tn), lambda i, j: (i, 0, j))],
        out_specs=pl.BlockSpec((None, r, tn), lambda i, j: (i, 0, j)),
        out_shape=jax.ShapeDtypeStruct((DEPTH, r, n), F32),
        compiler_params=_params(2),
        name="modulation",
    )(cond, ada_w, ada_b.reshape(DEPTH, 1, n))
    return out.reshape(DEPTH, r, N_MOD, D)


def _ffn_slabs(x_of, n_slabs, mod_ref, j0, wi_ref, wo_ref, g_ref, b_ref, o_ref, act_ref):
    rows = o_ref.shape[0] // n_slabs
    xs = {}

    def up(p):
        xs[p] = x_of(p)
        h = _modulated(xs[p], mod_ref, j0)
        for c in range(D_FF // FF_CHUNK):
            lo = c * FF_CHUNK
            a = _dot(h, wi_ref[:, lo:lo + FF_CHUNK])
            u = _dot(h, wi_ref[:, D_FF + lo:D_FF + lo + FF_CHUNK])
            act_ref[p * rows:(p + 1) * rows, lo:lo + FF_CHUNK] = (_silu(a) * u).astype(BF)

    def down(p):
        y = _dot(act_ref[p * rows:(p + 1) * rows, :], wo_ref[...])
        r = ALPHA * xs.pop(p) + (0.5 * mod_ref[j0 + 2:j0 + 3, :]) * y
        o_ref[p * rows:(p + 1) * rows, :] = _layer_norm(r, g_ref[...], b_ref[...])

    for p in range(n_slabs):
        up(p)
        if p > 0:
            down(p - 1)
    down(n_slabs - 1)


def _ffn_kernel(x_ref, mod_ref, wi_ref, wo_ref, g_ref, b_ref, o_ref, act_ref, *, j0, n_slabs):
    rows = x_ref.shape[0] // n_slabs
    _ffn_slabs(lambda p: x_ref[p * rows:(p + 1) * rows, :], n_slabs, mod_ref, j0, wi_ref, wo_ref,
               g_ref, b_ref, o_ref, act_ref)


def _ffn_specs(layer, which):
    return [_layer_spec((D, 2 * D_FF), layer), _layer_spec((D_FF, D), layer),
            _layer_spec((1, D), 3 * layer + 2 * which), _layer_spec((1, D), 3 * layer + 2 * which)]


def _half_ffn(x2, mods, layer, which, sel, wi, wo, lng, lnb, tm):
    t = x2.shape[0]
    return pl.pallas_call(
        functools.partial(_ffn_kernel, j0=6 * which, n_slabs=tm // FFN_SLAB),
        grid=(t // tm,),
        in_specs=[_rows(tm, D), _mod_spec(layer, sel)] + _ffn_specs(layer, which),
        out_specs=_rows(tm, D),
        out_shape=jax.ShapeDtypeStruct((t, D), F32),
        scratch_shapes=[pltpu.VMEM((tm, D_FF), BF)],
        compiler_params=_params(1),
        name="half_ffn",
    )(x2, mods, wi, wo, lng, lnb)


def _outproj_ffn_kernel(a_ref, w_ref, x_ref, mod_ref, g1_ref, b1_ref, wi_ref, wo_ref, g2_ref, b2_ref,
                        o_ref, act_ref):
    n_slabs = x_ref.shape[0] // FFN_SLAB

    def mixed(p):
        rows = slice(p * FFN_SLAB, (p + 1) * FFN_SLAB)
        y = _dot(a_ref[rows, :], w_ref[...])
        return _layer_norm(ALPHA * x_ref[rows, :] + mod_ref[5:6, :] * y, g1_ref[...], b1_ref[...])

    _ffn_slabs(mixed, n_slabs, mod_ref, 6, wi_ref, wo_ref, g2_ref, b2_ref, o_ref, act_ref)


def _outproj_ffn(a2, w, x2, mods, layer, sel, wi, wo, lng, lnb, tm):
    t, k = a2.shape
    return pl.pallas_call(
        _outproj_ffn_kernel,
        grid=(t // tm,),
        in_specs=[_rows(tm, k), _layer_spec((k, D), 0), _rows(tm, D), _mod_spec(layer, sel),
                  _layer_spec((1, D), 3 * layer + 1), _layer_spec((1, D), 3 * layer + 1)]
                 + _ffn_specs(layer, 1),
        out_specs=_rows(tm, D),
        out_shape=jax.ShapeDtypeStruct((t, D), F32),
        scratch_shapes=[pltpu.VMEM((tm, D_FF), BF)],
        compiler_params=_params(1),
        name="outproj_ffn",
    )(a2, w, x2, mods, lng, lnb, wi, wo, lng, lnb)


def _rope_angles(n_tokens, dim):
    rows = n_tokens // GRID_W
    row = np.repeat(np.arange(rows), GRID_W).astype(np.float32)
    col = np.tile(np.arange(GRID_W), rows).astype(np.float32)
    n_freq = dim // 4
    inv = (ROPE_BASE ** (-np.arange(n_freq, dtype=np.float32) / n_freq)).astype(np.float32)
    return np.concatenate([row[:, None] * inv, col[:, None] * inv], axis=-1)


def _da_rope_tables(n_tokens):
    ang = _rope_angles(n_tokens, DA_HEAD_DIM)
    cos, sin = np.cos(ang), np.sin(ang)
    cos128 = np.tile(cos, (1, 4))
    sin128 = np.tile(np.concatenate([-sin, sin], axis=-1), (1, 2))
    return jnp.asarray(cos128, F32), jnp.asarray(sin128, F32)


def _rt_rope_tables(n_tokens):
    ang = _rope_angles(n_tokens, RT_QK)
    return jnp.asarray(np.cos(ang), F32), jnp.asarray(np.sin(ang), F32)


def _proj_da_kernel(*refs, rope):
    if rope:
        x_ref, mod_ref, w_ref, cos_ref, sin_ref, q_ref, k_ref, v_ref = refs
        cos, sin = cos_ref[...], sin_ref[...]
        lane = lax.broadcasted_iota(jnp.int32, (1, LANES), 1)
        first_half = (lane % DA_HEAD_DIM) < (DA_HEAD_DIM // 2)
    else:
        x_ref, mod_ref, w_ref, q_ref, k_ref, v_ref = refs
    h = _modulated(x_ref[...], mod_ref, 3)

    def rot(y):
        if not rope:
            return y
        partner = jnp.where(first_half, pltpu.roll(y, LANES - 32, 1), pltpu.roll(y, 32, 1))
        return y * cos + partner * sin

    q = _dot(h, w_ref[:, 0:D]) * (DA_HEAD_DIM ** -0.5 * LOG2E)
    for j in range(D // LANES):
        q_ref[:, j * LANES:(j + 1) * LANES] = rot(q[:, j * LANES:(j + 1) * LANES]).astype(BF)
    k = _dot(h, w_ref[:, D:2 * D])
    for j in range(D // LANES):
        k_ref[:, j * LANES:(j + 1) * LANES] = rot(k[:, j * LANES:(j + 1) * LANES]).astype(BF)
    v_ref[...] = _dot(h, w_ref[:, 2 * D:3 * D]).T.astype(BF)


def _proj_da(x2, mods, layer, sel, w, rope_tabs, seq, tm):
    t = x2.shape[0]
    rpb = seq // tm
    rope = rope_tabs is not None
    in_specs = [_rows(tm, D), _mod_spec(layer, sel), _layer_spec((D, 3 * D), 0)]
    args = [x2, mods, w]
    if rope:
        in_specs += [pl.BlockSpec((tm, LANES), lambda i: (i % rpb, 0))] * 2
        args += list(rope_tabs)
    return pl.pallas_call(
        functools.partial(_proj_da_kernel, rope=rope),
        grid=(t // tm,),
        in_specs=in_specs,
        out_specs=[_rows(tm, D), _rows(tm, D),
                   pl.BlockSpec((None, D, tm), lambda i: (i // rpb, 0, i % rpb))],
        out_shape=[jax.ShapeDtypeStruct((t, D), BF), jax.ShapeDtypeStruct((t, D), BF),
                   jax.ShapeDtypeStruct((t // seq, D, seq), BF)],
        compiler_params=_params(1),
        name="proj_da",
    )(*args)


def _attn_kernel(*refs, lam_init, n_kv, parts):
    lam_ref, g_ref, q_ref = refs[:3]
    kv_refs = [(refs[3 + 2 * j], refs[4 + 2 * j]) for j in range(n_kv)]
    o_ref = refs[3 + 2 * n_kv]
    width = q_ref.shape[0] // parts
    lane = lax.broadcasted_iota(jnp.int32, (width, LANES), 1)

    def scores(part, head_map):
        q = q_ref[part * width:(part + 1) * width, :]
        keep = (lane < DA_HEAD_DIM) if head_map == 0 else (lane >= DA_HEAD_DIM)
        qm = jnp.where(keep, q, jnp.zeros_like(q))
        return [lax.dot_general(k_ref[...], qm, NT, preferred_element_type=F32) for k_ref, _ in kv_refs]

    def attend(ss):
        m = functools.reduce(jnp.maximum, [jnp.max(s, 0, keepdims=True) for s in ss])
        es = [jnp.exp2(s - m) for s in ss]
        den = functools.reduce(jnp.add, [jnp.sum(e, 0, keepdims=True) for e in es])
        o = functools.reduce(jnp.add, [_dot(vt_ref[...], e.astype(BF))
                                       for e, (_, vt_ref) in zip(es, kv_refs)])
        return o / den

    chains = [(part, head_map) for part in range(parts) for head_map in range(2)]
    result, pending = {}, None
    for chain in chains:
        ss = scores(*chain)
        if pending is not None:
            result[pending[0]] = attend(pending[1])
        pending = (chain, ss)
    result[pending[0]] = attend(pending[1])

    lam = lam_ref[...]
    lam_full = (jnp.exp(jnp.sum(lam[0:1] * lam[1:2], -1, keepdims=True))
                - jnp.exp(jnp.sum(lam[2:3] * lam[3:4], -1, keepdims=True)) + lam_init)
    for part in range(parts):
        o = result[(part, 0)] - lam_full * result[(part, 1)]
        o = o * lax.rsqrt(jnp.mean(o * o, 0, keepdims=True) + LN_EPS)
        o_ref[part * width:(part + 1) * width, :] = (o.T * g_ref[...] * (1.0 - lam_init)).astype(BF)


def _diff_attn(q, kvs, lam, subln_g, lam_init, tq):
    b, n, _ = q.shape
    in_specs = [_layer_spec((4, DA_HEAD_DIM), 0), _layer_spec((1, LANES), 0),
                pl.BlockSpec((None, tq, LANES), lambda bi, h, i: (bi, i, h))]
    args = [lam, subln_g, q]
    for k, vt in kvs:
        n_k = k.shape[1]
        in_specs += [pl.BlockSpec((None, n_k, LANES), lambda bi, h, i: (bi, 0, h)),
                     pl.BlockSpec((None, LANES, n_k), lambda bi, h, i: (bi, h, 0))]
        args += [k, vt]
    return pl.pallas_call(
        functools.partial(_attn_kernel, lam_init=lam_init, n_kv=len(kvs),
                          parts=max(1, tq // ATTN_PART)),
        grid=(b, DA_HEADS, n // tq),
        in_specs=in_specs,
        out_specs=pl.BlockSpec((None, tq, LANES), lambda bi, h, i: (bi, i, h)),
        out_shape=jax.ShapeDtypeStruct((b, n, D), BF),
        compiler_params=_params(3),
        name="diff_attn",
    )(*args)


def _halo_specs(tm, t):
    blocks_per_tile = tm // HALO
    last = t // HALO - 1
    return [pl.BlockSpec((HALO, D), lambda i: (jnp.maximum(i * blocks_per_tile - 1, 0), 0)),
            _rows(tm, D),
            pl.BlockSpec((HALO, D), lambda i: (jnp.minimum((i + 1) * blocks_per_tile, last), 0))]


def _zero_outside(u, tm, rpb):
    pos = pl.program_id(0) % rpb
    keep_prev = jnp.where(pos == 0, 0.0, 1.0)
    keep_next = jnp.where(pos == rpb - 1, 0.0, 1.0)
    return jnp.concatenate([u[:HALO] * keep_prev, u[HALO:HALO + tm], u[HALO + tm:] * keep_next], axis=0)


def _dwconv3_rows(u, w):
    n = u.shape[0]
    y = pltpu.roll(u, 1, 0) * w[0:1, :] + u * w[1:2, :] + pltpu.roll(u, n - 1, 0) * w[2:3, :]
    return y[HALO:n - HALO, :]


def _proj_hy_kernel(xp_ref, x_ref, xn_ref, mod_ref, w_ref, cw_ref, cb_ref, x0_ref, z_ref, *, tm, rpb):
    xe = jnp.concatenate([xp_ref[...], x_ref[...], xn_ref[...]], axis=0)
    h = _modulated(xe, mod_ref, 3)

    def project(j):
        return _dot(h, w_ref[:, j * D:(j + 1) * D])

    def conv(u, j):
        u = _zero_outside(u, tm, rpb)
        return _dwconv3_rows(u, cw_ref[:, j * D:(j + 1) * D]) + cb_ref[:, j * D:(j + 1) * D]

    u0 = project(0)
    u1 = project(1)
    x0_ref[...] = conv(u0, 0)
    u2 = project(2)
    x1 = conv(u1, 1)
    z_ref[...] = x1 * conv(u2, 2)


def _proj_hy(x2, mods, layer, sel, w, conv_w, conv_b, seq, tm):
    t = x2.shape[0]
    return pl.pallas_call(
        functools.partial(_proj_hy_kernel, tm=tm, rpb=seq // tm),
        grid=(t // tm,),
        in_specs=_halo_specs(tm, t) + [
            _mod_spec(layer, sel), _layer_spec((D, 3 * D), 0), _layer_spec((3, 3 * D), 0),
            _layer_spec((1, 3 * D), 0)],
        out_specs=[_rows(tm, D)] * 2,
        out_shape=[jax.ShapeDtypeStruct((t, D), F32)] * 2,
        compiler_params=_params(1),
        name="proj_hy",
    )(x2, x2, x2, mods, w, conv_w, conv_b)


def _dft_tables(n):
    k = np.arange(n, dtype=np.int64)
    ph = (k[:, None] * k[None, :]) % (2 * n)
    ang = ph.astype(np.float64) * (math.pi / n)
    cos = np.cos(ang).astype(np.float32)
    sin = np.sin(ang).astype(np.float32)
    tail = np.zeros((HALO, n), np.float32)
    tail[0] = 1.0 - 2.0 * (k % 2)
    return (jnp.asarray(np.concatenate([cos, tail], 0)).astype(BF), jnp.asarray(sin).astype(BF))


def _hy_features(n):
    t = np.linspace(0.0, 1.0, n, dtype=np.float32)[:, None]
    bands = (HY_EMB - 1) // 2
    fr = np.linspace(1e-4, bands - 1, bands, dtype=np.float32)[None, :]
    w = (2.0 * math.pi * np.arange(n, dtype=np.float32)[:, None] / n).astype(np.float32)
    z = np.concatenate([t, np.cos(fr * w), -np.sin(fr * w)], axis=-1).astype(np.float32)
    return jnp.asarray(np.pad(z, ((0, 0), (0, LANES - HY_EMB))), F32)


def _hy_deltas():
    max_decay = math.log(1e-2) / 0.3
    min_decay = math.log(1e-2) / 1.5
    return jnp.asarray(np.abs(np.linspace(min_decay, max_decay, D, dtype=np.float32))[None, :], F32)


def _hy_filter_kernel(z_ref, w1_ref, b1_ref, f1_ref, w2_ref, b2_ref, f2_ref, w3f_ref, w3b_ref,
                      dl_ref, cos_ref, sin_ref, p_ref, q_ref, hn_ref):
    n = z_ref.shape[0]
    z = z_ref[...]
    h = jnp.sin(f1_ref[...] * (_dot(z.astype(BF), w1_ref[...].astype(BF)) + b1_ref[...]))
    h = jnp.sin(f2_ref[...] * (_dot(h.astype(BF), w2_ref[...].astype(BF)) + b2_ref[...]))
    hb16 = h.astype(BF)
    decay = jnp.exp(-z[:, 0:1] * dl_ref[...])
    h_f = _dot(hb16, w3f_ref[...].astype(BF)) * decay
    h_b = _dot(hb16, w3b_ref[...].astype(BF)) * decay
    row = lax.broadcasted_iota(jnp.int32, h_f.shape, 0)
    h_b = jnp.where(row == 0, 0.0, h_b)
    a = _dot(cos_ref[...], (h_f + h_b).astype(BF))
    bm = _dot(sin_ref[...], (h_f - h_b).astype(BF))
    wgt = jnp.where(row == 0, 1.0, 2.0) * (0.5 / n)
    p_ref[...] = a[0:n, :] * wgt
    q_ref[...] = -bm * wgt
    hn_ref[...] = a[n:n + HALO, :] * (0.5 / n)


def _hy_filter(n, tabs, fw1, fb1, ff1, fw2, fb2, ff2, fw3, tc):
    cos_t, sin_t = tabs
    w1 = jnp.pad(fw1, ((0, LANES - HY_EMB), (0, 0)))
    nt = D // tc
    small = lambda s: pl.BlockSpec(s, lambda j: (0, 0))
    return pl.pallas_call(
        _hy_filter_kernel,
        grid=(nt,),
        in_specs=[small((n, LANES)), small((LANES, HY_FH)), small((1, HY_FH)), small((1, HY_FH)),
                  small((HY_FH, HY_FH)), small((1, HY_FH)), small((1, HY_FH)),
                  pl.BlockSpec((HY_FH, tc), lambda j: (0, j)),
                  pl.BlockSpec((HY_FH, tc), lambda j: (0, j + nt)),
                  pl.BlockSpec((1, tc), lambda j: (0, j)),
                  _resident((n + HALO, n)), _resident((n, n))],
        out_specs=[pl.BlockSpec((n, tc), lambda j: (0, j)),
                   pl.BlockSpec((n, tc), lambda j: (0, j)),
                   pl.BlockSpec((HALO, tc), lambda j: (0, j))],
        out_shape=[jax.ShapeDtypeStruct((n, D), F32), jax.ShapeDtypeStruct((n, D), F32),
                   jax.ShapeDtypeStruct((HALO, D), F32)],
        compiler_params=_params(1),
        name="hy_filter",
    )(_hy_features(n), w1, fb1.reshape(1, -1), ff1.reshape(1, -1), fw2, fb2.reshape(1, -1),
      ff2.reshape(1, -1), fw3, fw3, _hy_deltas(), cos_t, sin_t)


def _hy_conv_kernel(z_ref, x0_ref, p_ref, q_ref, hn_ref, ds_ref, cos_ref, sin_ref, o_ref):
    n = z_ref.shape[0]
    z = z_ref[...]
    zb = z.astype(BF)
    a_ext = _dot(cos_ref[...], zb)
    bm = _dot(sin_ref[...], zb)
    a = a_ext[0:n, :]
    p, q = p_ref[...], q_ref[...]
    yr = (a * p + bm * q).astype(BF)
    yi = (a * q - bm * p).astype(BF)
    y = _dot(cos_ref[0:n, :], yr) - _dot(sin_ref[...], yi)
    row = lax.broadcasted_iota(jnp.int32, y.shape, 0)
    sign = (1 - 2 * (row % 2)).astype(F32)
    y = y + sign * (a_ext[n:n + 1, :] * hn_ref[0:1, :])
    o_ref[...] = (x0_ref[...] * (y + z * ds_ref[...])).astype(BF)


def _hy_conv(z, x0, p, q, hn, d_skip, tabs, tc):
    b, n, _ = z.shape
    cos_t, sin_t = tabs
    return pl.pallas_call(
        _hy_conv_kernel,
        grid=(D // tc, b),
        in_specs=[pl.BlockSpec((None, n, tc), lambda j, bi: (bi, 0, j)),
                  pl.BlockSpec((None, n, tc), lambda j, bi: (bi, 0, j)),
                  pl.BlockSpec((n, tc), lambda j, bi: (0, j)),
                  pl.BlockSpec((n, tc), lambda j, bi: (0, j)),
                  pl.BlockSpec((HALO, tc), lambda j, bi: (0, j)),
                  pl.BlockSpec((None, 1, tc), lambda j, bi: (0, 0, j)),
                  _resident((n + HALO, n)), _resident((n, n))],
        out_specs=pl.BlockSpec((None, n, tc), lambda j, bi: (bi, 0, j)),
        out_shape=jax.ShapeDtypeStruct((b, n, D), BF),
        compiler_params=_params(2),
        name="hy_conv",
    )(z, x0, p, q, hn, d_skip, cos_t, sin_t)


def _proj_rt_kernel(*refs, rope):
    if rope:
        x_ref, mod_ref, w_ref, cos_ref, sin_ref, q_ref, k_ref, v_ref, g_ref = refs
        cos, sin = cos_ref[...], sin_ref[...]
    else:
        x_ref, mod_ref, w_ref, k_ref, v_ref = refs
    h = _modulated(x_ref[...], mod_ref, 3)
    qk_w = RT_HEADS * RT_QK
    half = RT_QK // 2

    def store_rot(y, ref):
        for hd in range(RT_HEADS):
            lo = hd * RT_QK
            x1, x2 = y[:, lo:lo + half], y[:, lo + half:lo + RT_QK]
            if rope:
                x1, x2 = x1 * cos - x2 * sin, x1 * sin + x2 * cos
            ref[:, lo:lo + half] = x1.astype(BF)
            ref[:, lo + half:lo + RT_QK] = x2.astype(BF)

    if rope:
        store_rot(_dot(h, w_ref[:, 0:qk_w]), q_ref)
    store_rot(_dot(h, w_ref[:, qk_w:2 * qk_w]) * (RT_QK ** -0.5), k_ref)
    v_w = RT_HEADS * RT_V
    v_ref[...] = _dot(h, w_ref[:, 2 * qk_w:2 * qk_w + v_w]).astype(BF)
    if rope:
        g_ref[...] = _silu(_dot(h, w_ref[:, 2 * qk_w + v_w:2 * qk_w + 2 * v_w])).astype(BF)


def _proj_rt(x2, mods, layer, sel, w, rope_tabs, seq, tm):
    t = x2.shape[0]
    rpb = seq // tm
    rope = rope_tabs is not None
    qk_w, v_w = RT_HEADS * RT_QK, RT_HEADS * RT_V
    in_specs = [_rows(tm, D), _mod_spec(layer, sel), _layer_spec((D, 2 * qk_w + 2 * v_w), 0)]
    args = [x2, mods, w]
    if rope:
        in_specs += [pl.BlockSpec((tm, LANES), lambda i: (i % rpb, 0))] * 2
        args += list(rope_tabs)
        out_specs = [_rows(tm, qk_w), _rows(tm, qk_w), _rows(tm, v_w), _rows(tm, v_w)]
        out_shape = [jax.ShapeDtypeStruct((t, qk_w), BF), jax.ShapeDtypeStruct((t, qk_w), BF),
                     jax.ShapeDtypeStruct((t, v_w), BF), jax.ShapeDtypeStruct((t, v_w), BF)]
    else:
        out_specs = [_rows(tm, qk_w), _rows(tm, v_w)]
        out_shape = [jax.ShapeDtypeStruct((t, qk_w), BF), jax.ShapeDtypeStruct((t, v_w), BF)]
    return pl.pallas_call(
        functools.partial(_proj_rt_kernel, rope=rope),
        grid=(t // tm,),
        in_specs=in_specs,
        out_specs=out_specs,
        out_shape=out_shape,
        compiler_params=_params(1),
        name="proj_rt",
    )(*args)


def _ret_kernel(logit_ref, kc_ref, vc_ref, ql_ref, kl_ref, vl_ref, g_ref, gn_ref, o_ref,
                s_ref, acc_ref, *, chunk):
    hd = pl.program_id(1)
    n_ctx = kc_ref.shape[0] // chunk
    n_lat = ql_ref.shape[0] // chunk
    row = lax.broadcasted_iota(jnp.int32, (chunk, chunk), 0)
    col = lax.broadcasted_iota(jnp.int32, (chunk, chunk), 1)
    pos = lax.broadcasted_iota(jnp.int32, (chunk, 1), 0).astype(F32)

    decays = []
    for direction in (0, 1):
        log_g = jnp.log(jax.nn.sigmoid(jnp.full((chunk, 1), logit_ref[direction, hd], F32)))
        lag = (row - col) if direction == 0 else (col - row)
        d_intra = jnp.where(lag >= 0, jnp.exp(jnp.maximum(lag, 0).astype(F32) * log_g), 0.0)
        if direction == 0:
            d_read = jnp.exp((pos + 1.0) * log_g)
            d_write = jnp.exp((chunk - 1.0 - pos) * log_g)
        else:
            d_read = jnp.exp((chunk - pos) * log_g)
            d_write = jnp.exp(pos * log_g)
        decays.append((d_intra, d_read, d_write, jnp.exp(chunk * log_g[0:1, :])))

    s_ref[...] = jnp.zeros_like(s_ref)

    def absorb(direction, k_ref, v_ref, rows):
        _, _, d_write, d_chunk = decays[direction]
        kw = (k_ref[rows, :].astype(F32) * d_write).astype(BF)
        s_ref[direction] = d_chunk * s_ref[direction] + lax.dot_general(
            kw, v_ref[rows, :], TN, preferred_element_type=F32)

    for direction in (0, 1):
        for c in (range(n_ctx) if direction == 0 else reversed(range(n_ctx))):
            absorb(direction, kc_ref, vc_ref, pl.ds(c * chunk, chunk))

    def body(t, carry):
        for direction in (0, 1):
            d_intra, d_read, _, _ = decays[direction]
            c = t if direction == 0 else n_lat - 1 - t
            rows = pl.ds(pl.multiple_of(c * chunk, chunk), chunk)
            qc = ql_ref[rows, :]
            scores = lax.dot_general(qc, kl_ref[rows, :], NT, preferred_element_type=F32) * d_intra
            acc_ref[direction, rows, :] = (_dot(scores.astype(BF), vl_ref[rows, :])
                                           + _dot(qc, s_ref[direction].astype(BF)) * d_read)
            absorb(direction, kl_ref, vl_ref, rows)
        return carry

    lax.fori_loop(0, n_lat, body, 0)

    o = acc_ref[0] + acc_ref[1]
    mu = jnp.mean(o, -1, keepdims=True)
    dlt = o - mu
    var = jnp.mean(dlt * dlt, -1, keepdims=True)
    o_ref[...] = (g_ref[...].astype(F32) * (dlt * lax.rsqrt(var + LN_EPS) * gn_ref[...])).astype(BF)


def _retention(kc, vc, ql, kl, vl, g, decay_logit, gn_g):
    b, n, _ = ql.shape
    nc = kc.shape[1]
    qk = lambda m: pl.BlockSpec((None, m, RT_QK), lambda bi, h: (bi, 0, h))
    vv = lambda m: pl.BlockSpec((None, m, RT_V), lambda bi, h: (bi, 0, h))
    return pl.pallas_call(
        functools.partial(_ret_kernel, chunk=RT_CHUNK),
        grid=(b, RT_HEADS),
        in_specs=[pl.BlockSpec(memory_space=pltpu.SMEM),
                  qk(nc), vv(nc), qk(n), qk(n), vv(n), vv(n),
                  pl.BlockSpec((None, 1, RT_V), lambda bi, h: (0, 0, h))],
        out_specs=vv(n),
        out_shape=jax.ShapeDtypeStruct((b, n, RT_HEADS * RT_V), BF),
        scratch_shapes=[pltpu.VMEM((2, RT_QK, RT_V), F32), pltpu.VMEM((2, n, RT_V), F32)],
        compiler_params=_params(2),
        name="retention",
    )(decay_logit, kc, vc, ql, kl, vl, g, gn_g)


def _sc_ffn_kernel(xp_ref, x_ref, xn_ref, mod_ref, wi_ref, cw_ref, wo_ref, g1_ref, b1_ref,
                   fwi_ref, fwo_ref, g2_ref, b2_ref, o_ref, act_ref, *, tm, rpb):
    x = x_ref[...]
    xe = jnp.concatenate([xp_ref[...], x, xn_ref[...]], axis=0)
    h = _modulated(xe, mod_ref, 3)
    cu = _dot(h, wi_ref[:, D:2 * D]) * _dot(h, wi_ref[:, 2 * D:3 * D])
    b_gate = _dot(h[HALO:HALO + tm, :], wi_ref[:, 0:D])
    conv = _dwconv3_rows(_zero_outside(cu, tm, rpb), cw_ref[...])
    y = _dot((b_gate * conv).astype(BF), wo_ref[...])
    x1 = _layer_norm(ALPHA * x + mod_ref[5:6, :] * y, g1_ref[...], b1_ref[...])
    _ffn_slabs(lambda p: x1[p * FFN_SLAB:(p + 1) * FFN_SLAB, :], tm // FFN_SLAB, mod_ref, 6,
               fwi_ref, fwo_ref, g2_ref, b2_ref, o_ref, act_ref)


def _short_conv_ffn(x2, mods, layer, sel, wi, conv_w, wo, fwi, fwo, lng, lnb, seq, tm):
    t = x2.shape[0]
    return pl.pallas_call(
        functools.partial(_sc_ffn_kernel, tm=tm, rpb=seq // tm),
        grid=(t // tm,),
        in_specs=_halo_specs(tm, t) + [
            _mod_spec(layer, sel), _layer_spec((D, 3 * D), 0), _layer_spec((3, D), 0),
            _layer_spec((D, D), 0), _layer_spec((1, D), 3 * layer + 1),
            _layer_spec((1, D), 3 * layer + 1)] + _ffn_specs(layer, 1),
        out_specs=_rows(tm, D),
        out_shape=jax.ShapeDtypeStruct((t, D), F32),
        scratch_shapes=[pltpu.VMEM((tm, D_FF), BF)],
        compiler_params=_params(1),
        name="short_conv_ffn",
    )(x2, x2, x2, mods, wi, conv_w, wo, lng, lnb, fwi, fwo, lng, lnb)


def kernel(x, c, ctx, c_ctx, ada_w, ada_b, ln_g, ln_b, ffa_wi, ffa_wo, ffb_wi, ffb_wo, da_w_qkv, da_w_o, da_lambda, da_subln_g, hy_w_in, hy_conv_w, hy_conv_b, hy_fw1, hy_fb1, hy_ff1, hy_fw2, hy_fb2, hy_ff2, hy_fw3, hy_d_skip, hy_w_o, rt_w_in, rt_decay_logit, rt_gn_g, rt_w_o, sc_w_in, sc_conv_w, sc_w_o):
    bsz, seq, _ = x.shape
    n_ctx = ctx.shape[1]
    assert x.shape[2] == D and ada_w.shape[0] == DEPTH and seq % GRID_W == 0
    tm = min(512, seq)
    tmc = min(512, n_ctx)
    assert seq % tm == 0 and n_ctx % tmc == 0 and seq % RT_CHUNK == 0 and n_ctx % RT_CHUNK == 0

    n_rows = -(-(bsz + 1) // HALO) * HALO
    cond = jnp.zeros((n_rows, D), F32).at[:bsz].set(c).at[bsz].set(c_ctx)
    mods = _modulation_all(cond, ada_w, ada_b)

    rpb = seq // tm
    lat = (lambda i: i // rpb, seq, tm)
    cx = (lambda i: bsz, n_ctx, tmc)

    xl = x.reshape(bsz * seq, D)
    xc = ctx.reshape(bsz * n_ctx, D)
    lng = ln_g.reshape(DEPTH * 3, 1, D)
    lnb = ln_b.reshape(DEPTH * 3, 1, D)
    ffa = (ffa_wi.astype(BF), ffa_wo.astype(BF))
    ffb = (ffb_wi.astype(BF), ffb_wo.astype(BF))

    def row_tile(stream, want):
        rows = seq if stream is lat else bsz * n_ctx
        tile = min(want, rows)
        assert rows % tile == 0
        sel = (lambda j: j // (seq // tile)) if stream is lat else stream[0]
        return tile, sel

    def ffn_a(xx, i, stream):
        tile, sel = row_tile(stream, FFN_TILE)
        return _half_ffn(xx, mods, i, 0, sel, ffa[0], ffa[1], lng, lnb, tile)

    def mix_out(a, w_o, xx, i, stream):
        tile, sel = row_tile(stream, tm)
        return _outproj_ffn(a, w_o, xx, mods, i, sel, ffb[0], ffb[1], lng, lnb, tile)

    def by_sample(a, n):
        return a.reshape(bsz, n, a.shape[-1])

    i = 0
    xl, xc = ffn_a(xl, i, lat), ffn_a(xc, i, cx)
    w_qkv = da_w_qkv.astype(BF)
    ql, kl, vtl = _proj_da(xl, mods, i, lat[0], w_qkv, _da_rope_tables(seq), seq, tm)
    qc, kc, vtc = _proj_da(xc, mods, i, cx[0], w_qkv, None, n_ctx, tmc)
    ql, kl, qc, kc = by_sample(ql, seq), by_sample(kl, seq), by_sample(qc, n_ctx), by_sample(kc, n_ctx)
    lam_init = 0.8 - 0.6 * math.exp(-0.3 * i)
    subln = da_subln_g.reshape(-1, 1, LANES)
    ol = _diff_attn(ql, [(kc, vtc), (kl, vtl)], da_lambda, subln, lam_init, min(ATTN_TILE, seq))
    oc = _diff_attn(qc, [(kc, vtc)], da_lambda, subln, lam_init, min(ATTN_TILE, n_ctx))
    w_o = da_w_o.astype(BF)
    xl = mix_out(ol.reshape(bsz * seq, D), w_o, xl, i, lat)
    xc = mix_out(oc.reshape(bsz * n_ctx, D), w_o, xc, i, cx)

    i = 1
    xl, xc = ffn_a(xl, i, lat), ffn_a(xc, i, cx)
    w_in = hy_w_in.astype(BF)
    w_o = hy_w_o.astype(BF)
    conv_b = hy_conv_b.reshape(-1, 1, 3 * D)
    d_skip = hy_d_skip.reshape(-1, 1, D)
    tc = 256

    def hyena(xx, stream):
        sel, n, tile = stream
        tabs = _dft_tables(n)
        p, q, hn = _hy_filter(n, tabs, hy_fw1[0], hy_fb1[0], hy_ff1[0], hy_fw2[0], hy_fb2[0],
                              hy_ff2[0], hy_fw3[0], tc)
        x0, z = _proj_hy(xx, mods, i, sel, w_in, hy_conv_w, conv_b, n, tile)
        y = _hy_conv(by_sample(z, n), by_sample(x0, n), p, q, hn, d_skip, tabs, tc)
        return mix_out(y.reshape(bsz * n, D), w_o, xx, i, stream)

    xl, xc = hyena(xl, lat), hyena(xc, cx)

    i = 2
    xl, xc = ffn_a(xl, i, lat), ffn_a(xc, i, cx)
    w_in = rt_w_in.astype(BF)
    ql, kl, vl, gl = [by_sample(a, seq) for a in
                      _proj_rt(xl, mods, i, lat[0], w_in, _rt_rope_tables(seq), seq, tm)]
    kc, vc = [by_sample(a, n_ctx) for a in _proj_rt(xc, mods, i, cx[0], w_in, None, n_ctx, tmc)]
    o = _retention(kc, vc, ql, kl, vl, gl, rt_decay_logit[0], rt_gn_g.reshape(-1, 1, RT_HEADS * RT_V))
    xl = mix_out(o.reshape(bsz * seq, RT_HEADS * RT_V), rt_w_o.astype(BF), xl, i, lat)

    i = 3
    xl = ffn_a(xl, i, lat)
    xl = _short_conv_ffn(xl, mods, i, lat[0], sc_w_in.astype(BF), sc_conv_w, sc_w_o.astype(BF),
                         ffb[0], ffb[1], lng, lnb, seq, tm)
    return xl.reshape(bsz, seq, D)
```

```python
import functools
import math

import jax
import jax.numpy as jnp
import numpy as np
from jax import lax
from jax.experimental import pallas as pl
from jax.experimental.pallas import tpu as pltpu

D = 1024
N_MOD = 9
D_FF = 2816
LN_EPS = 1e-5
ROPE_BASE = 10000.0
GRID_W = 64
DEPTH = 4
ALPHA = (2.0 * DEPTH) ** 0.25
DA_HEADS = 8
DA_HEAD_DIM = 64
RT_HEADS = 4
RT_QK = 256
RT_V = 512
HY_EMB = 33
HY_FH = 64
LANES = 128
HALO = 8
FF_CHUNK = 256
RT_CHUNK = 256
FFN_TILE = 1024
FFN_SLAB = 256
ATTN_TILE = 2048
ATTN_PART = 512
VMEM_LIMIT = 56 * 1024 * 1024
LOG2E = math.log2(math.e)

F32 = jnp.float32
BF = jnp.bfloat16
NT = (((1,), (1,)), ((), ()))
TN = (((0,), (0,)), ((), ()))


def _dot(a, b):
    return jnp.dot(a, b, preferred_element_type=F32)


def _resident(shape):
    nd = len(shape)
    return pl.BlockSpec(shape, lambda *_: (0,) * nd, pipeline_mode=pl.Buffered(1))


def _layer_spec(shape, layer):
    nd = len(shape)
    return pl.BlockSpec((None,) + tuple(shape), lambda *_: (layer,) + (0,) * nd,
                        pipeline_mode=pl.Buffered(1))


def _mod_spec(layer, sel):
    return pl.BlockSpec((None, None, N_MOD, D), lambda i: (layer, sel(i), 0, 0))


def _rows(tm, width):
    return pl.BlockSpec((tm, width), lambda i: (i, 0))


def _params(n_axes):
    return pltpu.CompilerParams(dimension_semantics=("parallel",) * n_axes,
                                vmem_limit_bytes=VMEM_LIMIT)


def _layer_norm(r, g, b):
    mu = jnp.mean(r, -1, keepdims=True)
    d = r - mu
    var = jnp.mean(d * d, -1, keepdims=True)
    return d * lax.rsqrt(var + LN_EPS) * g + b


def _silu(a):
    return a * jax.nn.sigmoid(a)


def _modulated(x, mod_ref, j):
    return (x * (1.0 + mod_ref[j + 1:j + 2, :]) + mod_ref[j:j + 1, :]).astype(BF)


def _mod_kernel(c_ref, w_ref, b_ref, o_ref):
    s = _silu(c_ref[...]).astype(BF)
    o_ref[...] = _dot(s, w_ref[...].astype(BF)) + b_ref[...]


def _modulation_all(cond, ada_w, ada_b):
    r = cond.shape[0]
    tn = 2304
    n = N_MOD * D
    out = pl.pallas_call(
        _mod_kernel,
        grid=(DEPTH, n // tn),
        in_specs=[pl.BlockSpec((r, D), lambda i, j: (0, 0)),
                  pl.BlockSpec((None, D, tn), lambda i, j: (i, 0, j)),
                  pl.BlockSpec((None, 1, tn), lambda i, j: (i, 0, j))],
        out_specs=pl.BlockSpec((None, r, tn), lambda i, j: (i, 0, j)),
        out_shape=jax.ShapeDtypeStruct((DEPTH, r, n), F32),
        compiler_params=_params(2),
        name="modulation",
    )(cond, ada_w, ada_b.reshape(DEPTH, 1, n))
    return out.reshape(DEPTH, r, N_MOD, D)


def _ffn_slabs(x_of, n_slabs, mod_ref, j0, wi_ref, wo_ref, g_ref, b_ref, o_ref, act_ref):
    rows = o_ref.shape[0] // n_slabs
    xs = {}

    def up(p):
        xs[p] = x_of(p)
        h = _modulated(xs[p], mod_ref, j0)
        for c in range(D_FF // FF_CHUNK):
            lo = c * FF_CHUNK
            a = _dot(h, wi_ref[:, lo:lo + FF_CHUNK])
            u = _dot(h, wi_ref[:, D_FF + lo:D_FF + lo + FF_CHUNK])
            act_ref[p * rows:(p + 1) * rows, lo:lo + FF_CHUNK] = (_silu(a) * u).astype(BF)

    def down(p):
        y = _dot(act_ref[p * rows:(p + 1) * rows, :], wo_ref[...])
        r = ALPHA * xs.pop(p) + (0.5 * mod_ref[j0 + 2:j0 + 3, :]) * y
        o_ref[p * rows:(p + 1) * rows, :] = _layer_norm(r, g_ref[...], b_ref[...])

    for p in range(n_slabs):
        up(p)
        if p > 0:
            down(p - 1)
    down(n_slabs - 1)


def _ffn_specs(layer, which):
    return [_resident((D, 2 * D_FF)), _resident((D_FF, D)),
            _layer_spec((1, D), 3 * layer + 2 * which), _layer_spec((1, D), 3 * layer + 2 * which)]


def _rider_specs(next_layer, steps):
    def spec(rows):
        return pl.BlockSpec((None, rows // steps, 2 * D_FF), lambda i: (next_layer, i, 0))
    out = lambda rows: pl.BlockSpec((rows // steps, 2 * D_FF), lambda i: (i, 0))
    in_specs = [spec(D), spec(D // 2)]
    out_specs = [out(D), out(D // 2)]
    out_shape = [jax.ShapeDtypeStruct((D, 2 * D_FF), BF), jax.ShapeDtypeStruct((D // 2, 2 * D_FF), BF)]
    return in_specs, out_specs, out_shape


def _run_rider(rider_refs):
    for src_ref, dst_ref in rider_refs:
        dst_ref[...] = src_ref[...].astype(BF)


def _split_rider(rest, has_rider):
    if has_rider:
        src_wi, src_wo, o_ref, dst_wi, dst_wo, act_ref = rest
        return o_ref, act_ref, [(src_wi, dst_wi), (src_wo, dst_wo)]
    o_ref, act_ref = rest
    return o_ref, act_ref, []


def _ffn_kernel(x_ref, mod_ref, wi_ref, wo_ref, g_ref, b_ref, *rest, j0, n_slabs, has_rider):
    o_ref, act_ref, rider = _split_rider(rest, has_rider)
    _run_rider(rider)
    rows = x_ref.shape[0] // n_slabs
    _ffn_slabs(lambda p: x_ref[p * rows:(p + 1) * rows, :], n_slabs, mod_ref, j0, wi_ref, wo_ref,
               g_ref, b_ref, o_ref, act_ref)


def _call_with_rider(kernel_fn, name, t, tm, in_specs, args, rider):
    out_specs, out_shape = [_rows(tm, D)], [jax.ShapeDtypeStruct((t, D), F32)]
    if rider is not None:
        r_in, r_out, r_shape = _rider_specs(rider[0], t // tm)
        in_specs, args = in_specs + r_in, args + [rider[1], rider[2]]
        out_specs, out_shape = out_specs + r_out, out_shape + r_shape
    outs = pl.pallas_call(
        functools.partial(kernel_fn, has_rider=rider is not None),
        grid=(t // tm,),
        in_specs=in_specs,
        out_specs=out_specs,
        out_shape=out_shape,
        scratch_shapes=[pltpu.VMEM((tm, D_FF), BF)],
        compiler_params=_params(1),
        name=name,
    )(*args)
    if rider is None:
        return outs[0], None
    return outs[0], (outs[1], outs[2].reshape(D_FF, D))


def _half_ffn(x2, mods, layer, which, sel, weights, lng, lnb, tm, rider):
    return _call_with_rider(
        functools.partial(_ffn_kernel, j0=6 * which, n_slabs=tm // FFN_SLAB), "half_ffn", x2.shape[0], tm,
        [_rows(tm, D), _mod_spec(layer, sel)] + _ffn_specs(layer, which),
        [x2, mods, weights[0], weights[1], lng, lnb], rider)


def _outproj_ffn_kernel(a_ref, w_ref, x_ref, mod_ref, g1_ref, b1_ref, wi_ref, wo_ref, g2_ref, b2_ref,
                        *rest, has_rider):
    o_ref, act_ref, rider = _split_rider(rest, has_rider)
    _run_rider(rider)
    n_slabs = x_ref.shape[0] // FFN_SLAB

    def mixed(p):
        rows = slice(p * FFN_SLAB, (p + 1) * FFN_SLAB)
        y = _dot(a_ref[rows, :], w_ref[...])
        return _layer_norm(ALPHA * x_ref[rows, :] + mod_ref[5:6, :] * y, g1_ref[...], b1_ref[...])

    _ffn_slabs(mixed, n_slabs, mod_ref, 6, wi_ref, wo_ref, g2_ref, b2_ref, o_ref, act_ref)


def _outproj_ffn(a2, w, x2, mods, layer, sel, weights, lng, lnb, tm, rider):
    t, k = a2.shape
    return _call_with_rider(
        _outproj_ffn_kernel, "outproj_ffn", t, tm,
        [_rows(tm, k), _layer_spec((k, D), 0), _rows(tm, D), _mod_spec(layer, sel),
         _layer_spec((1, D), 3 * layer + 1), _layer_spec((1, D), 3 * layer + 1)] + _ffn_specs(layer, 1),
        [a2, w, x2, mods, lng, lnb, weights[0], weights[1], lng, lnb], rider)


def _rope_angles(n_tokens, dim):
    rows = n_tokens // GRID_W
    row = np.repeat(np.arange(rows), GRID_W).astype(np.float32)
    col = np.tile(np.arange(GRID_W), rows).astype(np.float32)
    n_freq = dim // 4
    inv = (ROPE_BASE ** (-np.arange(n_freq, dtype=np.float32) / n_freq)).astype(np.float32)
    return np.concatenate([row[:, None] * inv, col[:, None] * inv], axis=-1)


def _da_rope_tables(n_tokens):
    ang = _rope_angles(n_tokens, DA_HEAD_DIM)
    cos, sin = np.cos(ang), np.sin(ang)
    cos128 = np.tile(cos, (1, 4))
    sin128 = np.tile(np.concatenate([-sin, sin], axis=-1), (1, 2))
    return jnp.asarray(cos128, F32), jnp.asarray(sin128, F32)


def _rt_rope_tables(n_tokens):
    ang = _rope_angles(n_tokens, RT_QK)
    return jnp.asarray(np.cos(ang), F32), jnp.asarray(np.sin(ang), F32)


def _proj_da_kernel(*refs, rope):
    if rope:
        x_ref, mod_ref, w_ref, cos_ref, sin_ref, q_ref, k_ref, v_ref = refs
        cos, sin = cos_ref[...], sin_ref[...]
        lane = lax.broadcasted_iota(jnp.int32, (1, LANES), 1)
        first_half = (lane % DA_HEAD_DIM) < (DA_HEAD_DIM // 2)
    else:
        x_ref, mod_ref, w_ref, q_ref, k_ref, v_ref = refs
    h = _modulated(x_ref[...], mod_ref, 3)

    def rot(y):
        if not rope:
            return y
        partner = jnp.where(first_half, pltpu.roll(y, LANES - 32, 1), pltpu.roll(y, 32, 1))
        return y * cos + partner * sin

    q = _dot(h, w_ref[:, 0:D]) * (DA_HEAD_DIM ** -0.5 * LOG2E)
    for j in range(D // LANES):
        q_ref[:, j * LANES:(j + 1) * LANES] = rot(q[:, j * LANES:(j + 1) * LANES]).astype(BF)
    k = _dot(h, w_ref[:, D:2 * D])
    for j in range(D // LANES):
        k_ref[:, j * LANES:(j + 1) * LANES] = rot(k[:, j * LANES:(j + 1) * LANES]).astype(BF)
    v_ref[...] = _dot(h, w_ref[:, 2 * D:3 * D]).T.astype(BF)


def _proj_da(x2, mods, layer, sel, w, rope_tabs, seq, tm):
    t = x2.shape[0]
    rpb = seq // tm
    rope = rope_tabs is not None
    in_specs = [_rows(tm, D), _mod_spec(layer, sel), _layer_spec((D, 3 * D), 0)]
    args = [x2, mods, w]
    if rope:
        in_specs += [pl.BlockSpec((tm, LANES), lambda i: (i % rpb, 0))] * 2
        args += list(rope_tabs)
    return pl.pallas_call(
        functools.partial(_proj_da_kernel, rope=rope),
        grid=(t // tm,),
        in_specs=in_specs,
        out_specs=[_rows(tm, D), _rows(tm, D),
                   pl.BlockSpec((None, D, tm), lambda i: (i // rpb, 0, i % rpb))],
        out_shape=[jax.ShapeDtypeStruct((t, D), BF), jax.ShapeDtypeStruct((t, D), BF),
                   jax.ShapeDtypeStruct((t // seq, D, seq), BF)],
        compiler_params=_params(1),
        name="proj_da",
    )(*args)


def _attn_kernel(*refs, lam_init, n_kv, parts):
    lam_ref, g_ref, q_ref = refs[:3]
    kv_refs = [(refs[3 + 2 * j], refs[4 + 2 * j]) for j in range(n_kv)]
    o_ref = refs[3 + 2 * n_kv]
    width = q_ref.shape[0] // parts
    lane = lax.broadcasted_iota(jnp.int32, (width, LANES), 1)

    def scores(part, head_map):
        q = q_ref[part * width:(part + 1) * width, :]
        keep = (lane < DA_HEAD_DIM) if head_map == 0 else (lane >= DA_HEAD_DIM)
        qm = jnp.where(keep, q, jnp.zeros_like(q))
        return [lax.dot_general(k_ref[...], qm, NT, preferred_element_type=F32) for k_ref, _ in kv_refs]

    def attend(ss):
        m = functools.reduce(jnp.maximum, [jnp.max(s, 0, keepdims=True) for s in ss])
        es = [jnp.exp2(s - m) for s in ss]
        den = functools.reduce(jnp.add, [jnp.sum(e, 0, keepdims=True) for e in es])
        o = functools.reduce(jnp.add, [_dot(vt_ref[...], e.astype(BF))
                                       for e, (_, vt_ref) in zip(es, kv_refs)])
        return o / den

    chains = [(part, head_map) for part in range(parts) for head_map in range(2)]
    result, pending = {}, None
    for chain in chains:
        ss = scores(*chain)
        if pending is not None:
            result[pending[0]] = attend(pending[1])
        pending = (chain, ss)
    result[pending[0]] = attend(pending[1])

    lam = lam_ref[...]
    lam_full = (jnp.exp(jnp.sum(lam[0:1] * lam[1:2], -1, keepdims=True))
                - jnp.exp(jnp.sum(lam[2:3] * lam[3:4], -1, keepdims=True)) + lam_init)
    for part in range(parts):
        o = result[(part, 0)] - lam_full * result[(part, 1)]
        o = o * lax.rsqrt(jnp.mean(o * o, 0, keepdims=True) + LN_EPS)
        o_ref[part * width:(part + 1) * width, :] = (o.T * g_ref[...] * (1.0 - lam_init)).astype(BF)


def _diff_attn(q, kvs, lam, subln_g, lam_init, tq):
    b, n, _ = q.shape
    in_specs = [_layer_spec((4, DA_HEAD_DIM), 0), _layer_spec((1, LANES), 0),
                pl.BlockSpec((None, tq, LANES), lambda bi, h, i: (bi, i, h))]
    args = [lam, subln_g, q]
    for k, vt in kvs:
        n_k = k.shape[1]
        in_specs += [pl.BlockSpec((None, n_k, LANES), lambda bi, h, i: (bi, 0, h)),
                     pl.BlockSpec((None, LANES, n_k), lambda bi, h, i: (bi, h, 0))]
        args += [k, vt]
    return pl.pallas_call(
        functools.partial(_attn_kernel, lam_init=lam_init, n_kv=len(kvs),
                          parts=max(1, tq // ATTN_PART)),
        grid=(b, DA_HEADS, n // tq),
        in_specs=in_specs,
        out_specs=pl.BlockSpec((None, tq, LANES), lambda bi, h, i: (bi, i, h)),
        out_shape=jax.ShapeDtypeStruct((b, n, D), BF),
        compiler_params=_params(3),
        name="diff_attn",
    )(*args)


def _halo_specs(tm, t):
    blocks_per_tile = tm // HALO
    last = t // HALO - 1
    return [pl.BlockSpec((HALO, D), lambda i: (jnp.maximum(i * blocks_per_tile - 1, 0), 0)),
            _rows(tm, D),
            pl.BlockSpec((HALO, D), lambda i: (jnp.minimum((i + 1) * blocks_per_tile, last), 0))]


def _zero_outside(u, tm, rpb):
    pos = pl.program_id(0) % rpb
    keep_prev = jnp.where(pos == 0, 0.0, 1.0)
    keep_next = jnp.where(pos == rpb - 1, 0.0, 1.0)
    return jnp.concatenate([u[:HALO] * keep_prev, u[HALO:HALO + tm], u[HALO + tm:] * keep_next], axis=0)


def _dwconv3_rows(u, w):
    n = u.shape[0]
    y = pltpu.roll(u, 1, 0) * w[0:1, :] + u * w[1:2, :] + pltpu.roll(u, n - 1, 0) * w[2:3, :]
    return y[HALO:n - HALO, :]


def _proj_hy_kernel(xp_ref, x_ref, xn_ref, mod_ref, w_ref, cw_ref, cb_ref, x0_ref, z_ref, *, tm, rpb):
    xe = jnp.concatenate([xp_ref[...], x_ref[...], xn_ref[...]], axis=0)
    h = _modulated(xe, mod_ref, 3)

    def project(j):
        return _dot(h, w_ref[:, j * D:(j + 1) * D])

    def conv(u, j):
        u = _zero_outside(u, tm, rpb)
        return _dwconv3_rows(u, cw_ref[:, j * D:(j + 1) * D]) + cb_ref[:, j * D:(j + 1) * D]

    u0 = project(0)
    u1 = project(1)
    x0_ref[...] = conv(u0, 0)
    u2 = project(2)
    x1 = conv(u1, 1)
    z_ref[...] = x1 * conv(u2, 2)


def _proj_hy(x2, mods, layer, sel, w, conv_w, conv_b, seq, tm):
    t = x2.shape[0]
    return pl.pallas_call(
        functools.partial(_proj_hy_kernel, tm=tm, rpb=seq // tm),
        grid=(t // tm,),
        in_specs=_halo_specs(tm, t) + [
            _mod_spec(layer, sel), _layer_spec((D, 3 * D), 0), _layer_spec((3, 3 * D), 0),
            _layer_spec((1, 3 * D), 0)],
        out_specs=[_rows(tm, D)] * 2,
        out_shape=[jax.ShapeDtypeStruct((t, D), F32)] * 2,
        compiler_params=_params(1),
        name="proj_hy",
    )(x2, x2, x2, mods, w, conv_w, conv_b)


def _dft_tables(n):
    m = n // 2
    k = np.arange(m + HALO, dtype=np.int64)[:, None]
    valid = k <= m
    s_even = 2 * np.arange(m, dtype=np.int64)[None, :]

    def table(fn, offset):
        phase = (k * (s_even + offset)) % (2 * n)
        return np.where(valid, fn(phase.astype(np.float64) * (math.pi / n)), 0.0).astype(np.float32)

    ce, se, co, so = table(np.cos, 0), table(np.sin, 0), table(np.cos, 1), table(np.sin, 1)
    as_bf = lambda a: jnp.asarray(np.ascontiguousarray(a)).astype(BF)
    fwd = (as_bf(np.concatenate([ce, se], 0)), as_bf(np.concatenate([co, so], 0)))
    inv = tuple(as_bf(t[:m].T) for t in (ce, se, co, so))
    return fwd, inv


def _parity_split(v, split_ref):
    n, width = v.shape
    slabs = width // LANES
    for c in range(slabs):
        split_ref[c] = v[:, c * LANES:(c + 1) * LANES]
    take = lambda start: jnp.concatenate(
        [split_ref[c, pl.ds(start, n // 2, stride=2), :] for c in range(slabs)], axis=1)
    return take(0), take(1)


def _parity_merge(even, odd, split_ref):
    m, width = even.shape
    slabs = width // LANES
    for c in range(slabs):
        split_ref[c, pl.ds(0, m, stride=2), :] = even[:, c * LANES:(c + 1) * LANES]
        split_ref[c, pl.ds(1, m, stride=2), :] = odd[:, c * LANES:(c + 1) * LANES]
    return jnp.concatenate([split_ref[c] for c in range(slabs)], axis=1)


def _hy_features(n):
    t = np.linspace(0.0, 1.0, n, dtype=np.float32)[:, None]
    bands = (HY_EMB - 1) // 2
    fr = np.linspace(1e-4, bands - 1, bands, dtype=np.float32)[None, :]
    w = (2.0 * math.pi * np.arange(n, dtype=np.float32)[:, None] / n).astype(np.float32)
    z = np.concatenate([t, np.cos(fr * w), -np.sin(fr * w)], axis=-1).astype(np.float32)
    return jnp.asarray(np.pad(z, ((0, 0), (0, LANES - HY_EMB))), F32)


def _hy_deltas():
    max_decay = math.log(1e-2) / 0.3
    min_decay = math.log(1e-2) / 1.5
    return jnp.asarray(np.abs(np.linspace(min_decay, max_decay, D, dtype=np.float32))[None, :], F32)


def _hy_filter_kernel(z_ref, w1_ref, b1_ref, f1_ref, w2_ref, b2_ref, f2_ref, w3f_ref, w3b_ref,
                      dl_ref, fe_ref, fo_ref, plo_ref, qlo_ref, phi_ref, qhi_ref, split_ref, hid_ref):
    n = z_ref.shape[0]
    mp = n // 2 + HALO
    z = z_ref[...]

    @pl.when(pl.program_id(0) == 0)
    def _():
        h = jnp.sin(f1_ref[...] * (_dot(z.astype(BF), w1_ref[...].astype(BF)) + b1_ref[...]))
        h = jnp.sin(f2_ref[...] * (_dot(h.astype(BF), w2_ref[...].astype(BF)) + b2_ref[...]))
        hid_ref[...] = h.astype(BF)

    hb16 = hid_ref[...]
    decay = jnp.exp(-z[:, 0:1] * dl_ref[...])
    h_f = _dot(hb16, w3f_ref[...].astype(BF)) * decay
    h_b = _dot(hb16, w3b_ref[...].astype(BF)) * decay
    row = lax.broadcasted_iota(jnp.int32, h_f.shape, 0)
    h_b = jnp.where(row == 0, 0.0, h_b)
    sm_even, sm_odd = _parity_split(h_f + h_b, split_ref)
    a_e = _dot(fe_ref[0:mp, :], sm_even.astype(BF))
    a_o = _dot(fo_ref[0:mp, :], sm_odd.astype(BF))
    df_even, df_odd = _parity_split(h_f - h_b, split_ref)
    b_e = _dot(fe_ref[mp:2 * mp, :], df_even.astype(BF))
    b_o = _dot(fo_ref[mp:2 * mp, :], df_odd.astype(BF))
    frow = lax.broadcasted_iota(jnp.int32, a_e.shape, 0)
    w_lo = jnp.where(frow == 0, 1.0, 2.0) * (0.5 / n)
    w_hi = jnp.where(frow == 0, 1.0, jnp.where(frow < n // 2, 2.0, 0.0)) * (0.5 / n)
    plo_ref[...] = (a_e + a_o) * w_lo
    qlo_ref[...] = -(b_e + b_o) * w_lo
    phi_ref[...] = (a_e - a_o) * w_hi
    qhi_ref[...] = (b_e - b_o) * w_hi


def _hy_filter(n, tabs, fw1, fb1, ff1, fw2, fb2, ff2, fw3, tc):
    (fwd_even, fwd_odd), _ = tabs
    w1 = jnp.pad(fw1, ((0, LANES - HY_EMB), (0, 0)))
    nt = D // tc
    m, mp = n // 2, n // 2 + HALO
    small = lambda s: pl.BlockSpec(s, lambda j: (0, 0))
    return pl.pallas_call(
        _hy_filter_kernel,
        grid=(nt,),
        in_specs=[small((n, LANES)), small((LANES, HY_FH)), small((1, HY_FH)), small((1, HY_FH)),
                  small((HY_FH, HY_FH)), small((1, HY_FH)), small((1, HY_FH)),
                  pl.BlockSpec((HY_FH, tc), lambda j: (0, j)),
                  pl.BlockSpec((HY_FH, tc), lambda j: (0, j + nt)),
                  pl.BlockSpec((1, tc), lambda j: (0, j)),
                  _resident((2 * mp, m)), _resident((2 * mp, m))],
        out_specs=[pl.BlockSpec((mp, tc), lambda j: (0, j))] * 4,
        out_shape=[jax.ShapeDtypeStruct((mp, D), F32)] * 4,
        scratch_shapes=[pltpu.VMEM((tc // LANES, n, LANES), F32), pltpu.VMEM((n, HY_FH), BF)],
        compiler_params=pltpu.CompilerParams(dimension_semantics=("arbitrary",),
                                             vmem_limit_bytes=VMEM_LIMIT),
        name="hy_filter",
    )(_hy_features(n), w1, fb1.reshape(1, -1), ff1.reshape(1, -1), fw2, fb2.reshape(1, -1),
      ff2.reshape(1, -1), fw3, fw3, _hy_deltas(), fwd_even, fwd_odd)


def _hy_conv_kernel(z_ref, x0_ref, plo_ref, qlo_ref, phi_ref, qhi_ref, ds_ref, fe_ref, fo_ref,
                    cet_ref, set_ref, cot_ref, sot_ref, o_ref, split_ref):
    n = z_ref.shape[0]
    m, mp = n // 2, n // 2 + HALO
    z = z_ref[...]
    z_even, z_odd = _parity_split(z, split_ref)
    r_e = _dot(fe_ref[...], z_even.astype(BF))
    r_o = _dot(fo_ref[...], z_odd.astype(BF))
    a_e, b_e, a_o, b_o = r_e[0:mp], r_e[mp:2 * mp], r_o[0:mp], r_o[mp:2 * mp]
    a_lo, b_lo, a_hi, b_hi = a_e + a_o, b_e + b_o, a_e - a_o, b_o - b_e
    p, q = plo_ref[...], qlo_ref[...]
    yr_lo, yi_lo = a_lo * p + b_lo * q, a_lo * q - b_lo * p
    p, q = phi_ref[...], qhi_ref[...]
    yr_hi, yi_hi = a_hi * p + b_hi * q, a_hi * q - b_hi * p
    y_even = (_dot(cet_ref[...], (yr_lo + yr_hi)[0:m].astype(BF))
              - _dot(set_ref[...], (yi_lo - yi_hi)[0:m].astype(BF)))
    y_odd = (_dot(cot_ref[...], (yr_lo - yr_hi)[0:m].astype(BF))
             - _dot(sot_ref[...], (yi_lo + yi_hi)[0:m].astype(BF)))
    half = lax.broadcasted_iota(jnp.int32, y_even.shape, 0)
    sign = (1 - 2 * (half % 2)).astype(F32)
    y_even = y_even + sign * yr_lo[m:m + 1, :]
    y_odd = y_odd - sign * yi_lo[m:m + 1, :]
    y = _parity_merge(y_even, y_odd, split_ref)
    o_ref[...] = (x0_ref[...] * (y + z * ds_ref[...])).astype(BF)


def _hy_conv(z, x0, spectra, d_skip, tabs, tc):
    b, n, _ = z.shape
    (fwd_even, fwd_odd), inv = tabs
    m, mp = n // 2, n // 2 + HALO
    sample = pl.BlockSpec((None, n, tc), lambda j, bi: (bi, 0, j))
    return pl.pallas_call(
        _hy_conv_kernel,
        grid=(D // tc, b),
        in_specs=[sample, sample] + [pl.BlockSpec((mp, tc), lambda j, bi: (0, j))] * 4
                 + [pl.BlockSpec((None, 1, tc), lambda j, bi: (0, 0, j)),
                    _resident((2 * mp, m)), _resident((2 * mp, m))] + [_resident((m, m))] * 4,
        out_specs=sample,
        out_shape=jax.ShapeDtypeStruct((b, n, D), BF),
        scratch_shapes=[pltpu.VMEM((tc // LANES, n, LANES), F32)],
        compiler_params=_params(2),
        name="hy_conv",
    )(z, x0, *spectra, d_skip, fwd_even, fwd_odd, *inv)


def _proj_rt_kernel(*refs, rope):
    if rope:
        x_ref, mod_ref, w_ref, cos_ref, sin_ref, q_ref, k_ref, v_ref, g_ref = refs
        cos, sin = cos_ref[...], sin_ref[...]
    else:
        x_ref, mod_ref, w_ref, k_ref, v_ref = refs
    h = _modulated(x_ref[...], mod_ref, 3)
    qk_w = RT_HEADS * RT_QK
    half = RT_QK // 2

    def store_rot(y, ref):
        for hd in range(RT_HEADS):
            lo = hd * RT_QK
            x1, x2 = y[:, lo:lo + half], y[:, lo + half:lo + RT_QK]
            if rope:
                x1, x2 = x1 * cos - x2 * sin, x1 * sin + x2 * cos
            ref[:, lo:lo + half] = x1.astype(BF)
            ref[:, lo + half:lo + RT_QK] = x2.astype(BF)

    if rope:
        store_rot(_dot(h, w_ref[:, 0:qk_w]), q_ref)
    store_rot(_dot(h, w_ref[:, qk_w:2 * qk_w]) * (RT_QK ** -0.5), k_ref)
    v_w = RT_HEADS * RT_V
    v_ref[...] = _dot(h, w_ref[:, 2 * qk_w:2 * qk_w + v_w]).astype(BF)
    if rope:
        g_ref[...] = _silu(_dot(h, w_ref[:, 2 * qk_w + v_w:2 * qk_w + 2 * v_w])).astype(BF)


def _proj_rt(x2, mods, layer, sel, w, rope_tabs, seq, tm):
    t = x2.shape[0]
    rpb = seq // tm
    rope = rope_tabs is not None
    qk_w, v_w = RT_HEADS * RT_QK, RT_HEADS * RT_V
    in_specs = [_rows(tm, D), _mod_spec(layer, sel), _layer_spec((D, 2 * qk_w + 2 * v_w), 0)]
    args = [x2, mods, w]
    if rope:
        in_specs += [pl.BlockSpec((tm, LANES), lambda i: (i % rpb, 0))] * 2
        args += list(rope_tabs)
        out_specs = [_rows(tm, qk_w), _rows(tm, qk_w), _rows(tm, v_w), _rows(tm, v_w)]
        out_shape = [jax.ShapeDtypeStruct((t, qk_w), BF), jax.ShapeDtypeStruct((t, qk_w), BF),
                     jax.ShapeDtypeStruct((t, v_w), BF), jax.ShapeDtypeStruct((t, v_w), BF)]
    else:
        out_specs = [_rows(tm, qk_w), _rows(tm, v_w)]
        out_shape = [jax.ShapeDtypeStruct((t, qk_w), BF), jax.ShapeDtypeStruct((t, v_w), BF)]
    return pl.pallas_call(
        functools.partial(_proj_rt_kernel, rope=rope),
        grid=(t // tm,),
        in_specs=in_specs,
        out_specs=out_specs,
        out_shape=out_shape,
        compiler_params=_params(1),
        name="proj_rt",
    )(*args)


def _ret_kernel(logit_ref, kc_ref, vc_ref, ql_ref, kl_ref, vl_ref, g_ref, gn_ref, o_ref,
                s_ref, acc_ref, *, chunk):
    hd = pl.program_id(1)
    ctx_chunk = min(chunk, kc_ref.shape[0])
    n_ctx = kc_ref.shape[0] // ctx_chunk
    n_lat = ql_ref.shape[0] // chunk
    row = lax.broadcasted_iota(jnp.int32, (chunk, chunk), 0)
    col = lax.broadcasted_iota(jnp.int32, (chunk, chunk), 1)

    def log_gamma(direction, shape):
        return jnp.log(jax.nn.sigmoid(jnp.full(shape, logit_ref[direction, hd], F32)))

    def decay(direction, size, power):
        pos = lax.broadcasted_iota(jnp.int32, (size, 1), 0).astype(F32)
        return jnp.exp(power(pos) * log_gamma(direction, (size, 1)))

    def write_decays(direction, size):
        d_write = decay(direction, size, (lambda p: size - 1.0 - p) if direction == 0 else (lambda p: p))
        return d_write, jnp.exp(size * log_gamma(direction, (1, 1)))

    intra, read = [], []
    for direction in (0, 1):
        lag = (row - col) if direction == 0 else (col - row)
        intra.append(jnp.where(lag >= 0, jnp.exp(jnp.maximum(lag, 0).astype(F32)
                                                 * log_gamma(direction, (chunk, 1))), 0.0))
        read.append(decay(direction, chunk, (lambda p: p + 1.0) if direction == 0 else (lambda p: chunk - p)))
    lat_write = [write_decays(direction, chunk) for direction in (0, 1)]
    ctx_write = [write_decays(direction, ctx_chunk) for direction in (0, 1)]

    s_ref[...] = jnp.zeros_like(s_ref)

    def absorb(direction, k_ref, v_ref, rows, decays):
        d_write, d_block = decays[direction]
        kw = (k_ref[rows, :].astype(F32) * d_write).astype(BF)
        s_ref[direction] = d_block * s_ref[direction] + lax.dot_general(
            kw, v_ref[rows, :], TN, preferred_element_type=F32)

    for direction in (0, 1):
        for c in (range(n_ctx) if direction == 0 else reversed(range(n_ctx))):
            absorb(direction, kc_ref, vc_ref, pl.ds(c * ctx_chunk, ctx_chunk), ctx_write)

    def body(t, carry):
        for direction in (0, 1):
            c = t if direction == 0 else n_lat - 1 - t
            rows = pl.ds(pl.multiple_of(c * chunk, chunk), chunk)
            qc = ql_ref[rows, :]
            scores = (lax.dot_general(qc, kl_ref[rows, :], NT, preferred_element_type=F32)
                      * intra[direction])
            acc_ref[direction, rows, :] = (_dot(scores.astype(BF), vl_ref[rows, :])
                                           + _dot(qc, s_ref[direction].astype(BF)) * read[direction])
            absorb(direction, kl_ref, vl_ref, rows, lat_write)
        return carry

    lax.fori_loop(0, n_lat, body, 0)

    o = acc_ref[0] + acc_ref[1]
    mu = jnp.mean(o, -1, keepdims=True)
    dlt = o - mu
    var = jnp.mean(dlt * dlt, -1, keepdims=True)
    o_ref[...] = (g_ref[...].astype(F32) * (dlt * lax.rsqrt(var + LN_EPS) * gn_ref[...])).astype(BF)


def _retention(kc, vc, ql, kl, vl, g, decay_logit, gn_g):
    b, n, _ = ql.shape
    nc = kc.shape[1]
    qk = lambda m: pl.BlockSpec((None, m, RT_QK), lambda bi, h: (bi, 0, h))
    vv = lambda m: pl.BlockSpec((None, m, RT_V), lambda bi, h: (bi, 0, h))
    return pl.pallas_call(
        functools.partial(_ret_kernel, chunk=RT_CHUNK),
        grid=(b, RT_HEADS),
        in_specs=[pl.BlockSpec(memory_space=pltpu.SMEM),
                  qk(nc), vv(nc), qk(n), qk(n), vv(n), vv(n),
                  pl.BlockSpec((None, 1, RT_V), lambda bi, h: (0, 0, h))],
        out_specs=vv(n),
        out_shape=jax.ShapeDtypeStruct((b, n, RT_HEADS * RT_V), BF),
        scratch_shapes=[pltpu.VMEM((2, RT_QK, RT_V), F32), pltpu.VMEM((2, n, RT_V), F32)],
        compiler_params=_params(2),
        name="retention",
    )(decay_logit, kc, vc, ql, kl, vl, g, gn_g)


def _sc_ffn_kernel(xp_ref, x_ref, xn_ref, mod_ref, wi_ref, cw_ref, wo_ref, g1_ref, b1_ref,
                   fwi_ref, fwo_ref, g2_ref, b2_ref, o_ref, act_ref, *, tm, rpb):
    x = x_ref[...]
    xe = jnp.concatenate([xp_ref[...], x, xn_ref[...]], axis=0)
    h = _modulated(xe, mod_ref, 3)
    cu = _dot(h, wi_ref[:, D:2 * D]) * _dot(h, wi_ref[:, 2 * D:3 * D])
    b_gate = _dot(h[HALO:HALO + tm, :], wi_ref[:, 0:D])
    conv = _dwconv3_rows(_zero_outside(cu, tm, rpb), cw_ref[...])
    y = _dot((b_gate * conv).astype(BF), wo_ref[...])
    x1 = _layer_norm(ALPHA * x + mod_ref[5:6, :] * y, g1_ref[...], b1_ref[...])
    _ffn_slabs(lambda p: x1[p * FFN_SLAB:(p + 1) * FFN_SLAB, :], tm // FFN_SLAB, mod_ref, 6,
               fwi_ref, fwo_ref, g2_ref, b2_ref, o_ref, act_ref)


def _short_conv_ffn(x2, mods, layer, sel, wi, conv_w, wo, fwi, fwo, lng, lnb, seq, tm):
    t = x2.shape[0]
    return pl.pallas_call(
        functools.partial(_sc_ffn_kernel, tm=tm, rpb=seq // tm),
        grid=(t // tm,),
        in_specs=_halo_specs(tm, t) + [
            _mod_spec(layer, sel), _layer_spec((D, 3 * D), 0), _layer_spec((3, D), 0),
            _layer_spec((D, D), 0), _layer_spec((1, D), 3 * layer + 1),
            _layer_spec((1, D), 3 * layer + 1)] + _ffn_specs(layer, 1),
        out_specs=_rows(tm, D),
        out_shape=jax.ShapeDtypeStruct((t, D), F32),
        scratch_shapes=[pltpu.VMEM((tm, D_FF), BF)],
        compiler_params=_params(1),
        name="short_conv_ffn",
    )(x2, x2, x2, mods, wi, conv_w, wo, lng, lnb, fwi, fwo, lng, lnb)


def kernel(x, c, ctx, c_ctx, ada_w, ada_b, ln_g, ln_b, ffa_wi, ffa_wo, ffb_wi, ffb_wo, da_w_qkv, da_w_o, da_lambda, da_subln_g, hy_w_in, hy_conv_w, hy_conv_b, hy_fw1, hy_fb1, hy_ff1, hy_fw2, hy_fb2, hy_ff2, hy_fw3, hy_d_skip, hy_w_o, rt_w_in, rt_decay_logit, rt_gn_g, rt_w_o, sc_w_in, sc_conv_w, sc_w_o):
    bsz, seq, _ = x.shape
    n_ctx = ctx.shape[1]
    assert x.shape[2] == D and ada_w.shape[0] == DEPTH and seq % GRID_W == 0
    tm = min(512, seq)
    tmc = min(512, n_ctx)
    assert seq % tm == 0 and n_ctx % tmc == 0 and seq % RT_CHUNK == 0 and n_ctx % min(RT_CHUNK, n_ctx) == 0

    n_rows = -(-(bsz + 1) // HALO) * HALO
    cond = jnp.zeros((n_rows, D), F32).at[:bsz].set(c).at[bsz].set(c_ctx)
    mods = _modulation_all(cond, ada_w, ada_b)

    rpb = seq // tm
    lat = (lambda i: i // rpb, seq, tm)
    cx = (lambda i: bsz, n_ctx, tmc)

    xl = x.reshape(bsz * seq, D)
    xc = ctx.reshape(bsz * n_ctx, D)
    lng = ln_g.reshape(DEPTH * 3, 1, D)
    lnb = ln_b.reshape(DEPTH * 3, 1, D)
    ffw = {(0, 0): (ffa_wi[0].astype(BF), ffa_wo[0].astype(BF))}
    f32_stacks = {0: (ffa_wi, ffa_wo.reshape(DEPTH, D // 2, 2 * D_FF)),
                  1: (ffb_wi, ffb_wo.reshape(DEPTH, D // 2, 2 * D_FF))}

    def rider_for(i, which, stream):
        nxt = (i, 1) if which == 0 else (i + 1, 0)
        if stream is not lat or nxt[0] >= DEPTH:
            return None, nxt
        return (nxt[0],) + f32_stacks[nxt[1]], nxt

    def row_tile(stream, want):
        rows = seq if stream is lat else bsz * n_ctx
        tile = min(want, rows)
        assert rows % tile == 0
        sel = (lambda j: j // (seq // tile)) if stream is lat else stream[0]
        return tile, sel

    def ffn_a(xx, i, stream):
        tile, sel = row_tile(stream, FFN_TILE)
        rider, nxt = rider_for(i, 0, stream)
        out, cast = _half_ffn(xx, mods, i, 0, sel, ffw[(i, 0)], lng, lnb, tile, rider)
        if cast is not None:
            ffw[nxt] = cast
        return out

    def mix_out(a, w_o, xx, i, stream):
        tile, sel = row_tile(stream, tm)
        rider, nxt = rider_for(i, 1, stream)
        out, cast = _outproj_ffn(a, w_o, xx, mods, i, sel, ffw[(i, 1)], lng, lnb, tile, rider)
        if cast is not None:
            ffw[nxt] = cast
        return out

    def by_sample(a, n):
        return a.reshape(bsz, n, a.shape[-1])

    i = 0
    xl, xc = ffn_a(xl, i, lat), ffn_a(xc, i, cx)
    w_qkv = da_w_qkv.astype(BF)
    ql, kl, vtl = _proj_da(xl, mods, i, lat[0], w_qkv, _da_rope_tables(seq), seq, tm)
    qc, kc, vtc = _proj_da(xc, mods, i, cx[0], w_qkv, None, n_ctx, tmc)
    ql, kl, qc, kc = by_sample(ql, seq), by_sample(kl, seq), by_sample(qc, n_ctx), by_sample(kc, n_ctx)
    lam_init = 0.8 - 0.6 * math.exp(-0.3 * i)
    subln = da_subln_g.reshape(-1, 1, LANES)
    ol = _diff_attn(ql, [(kc, vtc), (kl, vtl)], da_lambda, subln, lam_init, min(ATTN_TILE, seq))
    oc = _diff_attn(qc, [(kc, vtc)], da_lambda, subln, lam_init, min(ATTN_TILE, n_ctx))
    w_o = da_w_o.astype(BF)
    xl = mix_out(ol.reshape(bsz * seq, D), w_o, xl, i, lat)
    xc = mix_out(oc.reshape(bsz * n_ctx, D), w_o, xc, i, cx)

    i = 1
    xl, xc = ffn_a(xl, i, lat), ffn_a(xc, i, cx)
    w_in = hy_w_in.astype(BF)
    w_o = hy_w_o.astype(BF)
    conv_b = hy_conv_b.reshape(-1, 1, 3 * D)
    d_skip = hy_d_skip.reshape(-1, 1, D)
    tc = 256

    def hyena(xx, stream):
        sel, n, tile = stream
        tabs = _dft_tables(n)
        spectra = _hy_filter(n, tabs, hy_fw1[0], hy_fb1[0], hy_ff1[0], hy_fw2[0], hy_fb2[0],
                             hy_ff2[0], hy_fw3[0], tc)
        x0, z = _proj_hy(xx, mods, i, sel, w_in, hy_conv_w, conv_b, n, tile)
        y = _hy_conv(by_sample(z, n), by_sample(x0, n), spectra, d_skip, tabs, tc)
        return mix_out(y.reshape(bsz * n, D), w_o, xx, i, stream)

    xl, xc = hyena(xl, lat), hyena(xc, cx)

    i = 2
    xl, xc = ffn_a(xl, i, lat), ffn_a(xc, i, cx)
    w_in = rt_w_in.astype(BF)
    ql, kl, vl, gl = [by_sample(a, seq) for a in
                      _proj_rt(xl, mods, i, lat[0], w_in, _rt_rope_tables(seq), seq, tm)]
    kc, vc = [by_sample(a, n_ctx) for a in _proj_rt(xc, mods, i, cx[0], w_in, None, n_ctx, tmc)]
    o = _retention(kc, vc, ql, kl, vl, gl, rt_decay_logit[0], rt_gn_g.reshape(-1, 1, RT_HEADS * RT_V))
    xl = mix_out(o.reshape(bsz * seq, RT_HEADS * RT_V), rt_w_o.astype(BF), xl, i, lat)

    i = 3
    xl = ffn_a(xl, i, lat)
    xl = _short_conv_ffn(xl, mods, i, lat[0], sc_w_in.astype(BF), sc_conv_w, sc_w_o.astype(BF),
                         ffw[(i, 1)][0], ffw[(i, 1)][1], lng, lnb, seq, tm)
    return xl.reshape(bsz, seq, D)
```

```python
import functools
import math

import jax
import jax.numpy as jnp
import numpy as np
from jax import lax
from jax.experimental import pallas as pl
from jax.experimental.pallas import tpu as pltpu

D = 1024
N_MOD = 9
D_FF = 2816
LN_EPS = 1e-5
ROPE_BASE = 10000.0
GRID_W = 64
DEPTH = 4
ALPHA = (2.0 * DEPTH) ** 0.25
DA_HEADS = 8
DA_HEAD_DIM = 64
RT_HEADS = 4
RT_QK = 256
RT_V = 512
HY_EMB = 33
HY_FH = 64
LANES = 128
HALO = 8
FF_CHUNK = 256
RT_CHUNK = 256
FFN_TILE = 1024
FFN_SLAB = 256
RIDER_BLOCKS = 16
ATTN_TILE = 2048
ATTN_PART = 512
VMEM_LIMIT = 56 * 1024 * 1024
LOG2E = math.log2(math.e)

F32 = jnp.float32
BF = jnp.bfloat16
NT = (((1,), (1,)), ((), ()))
TN = (((0,), (0,)), ((), ()))


def _dot(a, b):
    return jnp.dot(a, b, preferred_element_type=F32)


def _resident(shape):
    nd = len(shape)
    return pl.BlockSpec(shape, lambda *_: (0,) * nd, pipeline_mode=pl.Buffered(1))


def _layer_spec(shape, layer):
    nd = len(shape)
    return pl.BlockSpec((None,) + tuple(shape), lambda *_: (layer,) + (0,) * nd,
                        pipeline_mode=pl.Buffered(1))


def _mod_spec(layer, sel):
    return pl.BlockSpec((None, None, N_MOD, D), lambda i: (layer, sel(i), 0, 0))


def _rows(tm, width):
    return pl.BlockSpec((tm, width), lambda i: (i, 0))


def _params(n_axes):
    return pltpu.CompilerParams(dimension_semantics=("parallel",) * n_axes,
                                vmem_limit_bytes=VMEM_LIMIT)


def _layer_norm(r, g, b):
    mu = jnp.mean(r, -1, keepdims=True)
    d = r - mu
    var = jnp.mean(d * d, -1, keepdims=True)
    return d * lax.rsqrt(var + LN_EPS) * g + b


def _silu(a):
    return a * jax.nn.sigmoid(a)


def _modulated(x, mod_ref, j):
    return (x * (1.0 + mod_ref[j + 1:j + 2, :]) + mod_ref[j:j + 1, :]).astype(BF)


def _mod_kernel(c_ref, w_ref, b_ref, o_ref):
    s = _silu(c_ref[...]).astype(BF)
    o_ref[...] = _dot(s, w_ref[...].astype(BF)) + b_ref[...]


def _modulation_all(cond, ada_w, ada_b):
    r = cond.shape[0]
    tn = 2304
    n = N_MOD * D
    out = pl.pallas_call(
        _mod_kernel,
        grid=(DEPTH, n // tn),
        in_specs=[pl.BlockSpec((r, D), lambda i, j: (0, 0)),
                  pl.BlockSpec((None, D, tn), lambda i, j: (i, 0, j)),
                  pl.BlockSpec((None, 1, tn), lambda i, j: (i, 0, j))],
        out_specs=pl.BlockSpec((None, r, tn), lambda i, j: (i, 0, j)),
        out_shape=jax.ShapeDtypeStruct((DEPTH, r, n), F32),
        compiler_params=_params(2),
        name="modulation",
    )(cond, ada_w, ada_b.reshape(DEPTH, 1, n))
    return out.reshape(DEPTH, r, N_MOD, D)


def _ffn_slabs(x_of, n_slabs, mod_ref, j0, wi_ref, wo_ref, g_ref, b_ref, o_ref, act_ref):
    rows = o_ref.shape[0] // n_slabs
    xs = {}

    def up(p):
        xs[p] = x_of(p)
        h = _modulated(xs[p], mod_ref, j0)
        for c in range(D_FF // FF_CHUNK):
            lo = c * FF_CHUNK
            a = _dot(h, wi_ref[:, lo:lo + FF_CHUNK])
            u = _dot(h, wi_ref[:, D_FF + lo:D_FF + lo + FF_CHUNK])
            act_ref[p * rows:(p + 1) * rows, lo:lo + FF_CHUNK] = (_silu(a) * u).astype(BF)

    def down(p):
        y = _dot(act_ref[p * rows:(p + 1) * rows, :], wo_ref[...])
        r = ALPHA * xs.pop(p) + (0.5 * mod_ref[j0 + 2:j0 + 3, :]) * y
        o_ref[p * rows:(p + 1) * rows, :] = _layer_norm(r, g_ref[...], b_ref[...])

    for p in range(n_slabs):
        up(p)
        if p > 0:
            down(p - 1)
    down(n_slabs - 1)


def _ffn_specs(layer, which):
    return [_resident((D, 2 * D_FF)), _resident((D_FF, D)),
            _layer_spec((1, D), 3 * layer + 2 * which), _layer_spec((1, D), 3 * layer + 2 * which)]


def _rider_specs(next_layer, steps):
    blocks = min(RIDER_BLOCKS, steps)
    reps = steps // blocks
    assert steps % blocks == 0
    shapes = [(D, 2 * D_FF), (D_FF, D)]
    in_specs = [pl.BlockSpec((None, r // blocks, c), lambda i: (next_layer, i // reps, 0)) for r, c in shapes]
    out_specs = [pl.BlockSpec((r // blocks, c), lambda i: (i // reps, 0)) for r, c in shapes]
    out_shape = [jax.ShapeDtypeStruct(s, BF) for s in shapes]
    return in_specs, out_specs, out_shape


def _run_rider(rider_refs):
    for src_ref, dst_ref in rider_refs:
        dst_ref[...] = src_ref[...].astype(BF)


def _split_rider(rest, has_rider):
    if has_rider:
        src_wi, src_wo, o_ref, dst_wi, dst_wo, act_ref = rest
        return o_ref, act_ref, [(src_wi, dst_wi), (src_wo, dst_wo)]
    o_ref, act_ref = rest
    return o_ref, act_ref, []


def _ffn_kernel(x_ref, mod_ref, wi_ref, wo_ref, g_ref, b_ref, *rest, j0, n_slabs, has_rider):
    o_ref, act_ref, rider = _split_rider(rest, has_rider)
    _run_rider(rider)
    rows = x_ref.shape[0] // n_slabs
    _ffn_slabs(lambda p: x_ref[p * rows:(p + 1) * rows, :], n_slabs, mod_ref, j0, wi_ref, wo_ref,
               g_ref, b_ref, o_ref, act_ref)


def _call_with_rider(kernel_fn, name, t, tm, in_specs, args, rider):
    out_specs, out_shape = [_rows(tm, D)], [jax.ShapeDtypeStruct((t, D), F32)]
    if rider is not None:
        r_in, r_out, r_shape = _rider_specs(rider[0], t // tm)
        in_specs, args = in_specs + r_in, args + [rider[1], rider[2]]
        out_specs, out_shape = out_specs + r_out, out_shape + r_shape
    outs = pl.pallas_call(
        functools.partial(kernel_fn, has_rider=rider is not None),
        grid=(t // tm,),
        in_specs=in_specs,
        out_specs=out_specs,
        out_shape=out_shape,
        scratch_shapes=[pltpu.VMEM((tm, D_FF), BF)],
        compiler_params=_params(1),
        name=name,
    )(*args)
    if rider is None:
        return outs[0], None
    return outs[0], (outs[1], outs[2])


def _half_ffn(x2, mods, layer, which, sel, weights, lng, lnb, tm, rider):
    return _call_with_rider(
        functools.partial(_ffn_kernel, j0=6 * which, n_slabs=tm // FFN_SLAB), "half_ffn", x2.shape[0], tm,
        [_rows(tm, D), _mod_spec(layer, sel)] + _ffn_specs(layer, which),
        [x2, mods, weights[0], weights[1], lng, lnb], rider)


def _outproj_ffn_kernel(a_ref, w_ref, x_ref, mod_ref, g1_ref, b1_ref, wi_ref, wo_ref, g2_ref, b2_ref,
                        *rest, has_rider):
    o_ref, act_ref, rider = _split_rider(rest, has_rider)
    _run_rider(rider)
    n_slabs = x_ref.shape[0] // FFN_SLAB

    def mixed(p):
        rows = slice(p * FFN_SLAB, (p + 1) * FFN_SLAB)
        y = _dot(a_ref[rows, :], w_ref[...])
        return _layer_norm(ALPHA * x_ref[rows, :] + mod_ref[5:6, :] * y, g1_ref[...], b1_ref[...])

    _ffn_slabs(mixed, n_slabs, mod_ref, 6, wi_ref, wo_ref, g2_ref, b2_ref, o_ref, act_ref)


def _outproj_ffn(a2, w, x2, mods, layer, sel, weights, lng, lnb, tm, rider):
    t, k = a2.shape
    return _call_with_rider(
        _outproj_ffn_kernel, "outproj_ffn", t, tm,
        [_rows(tm, k), _layer_spec((k, D), 0), _rows(tm, D), _mod_spec(layer, sel),
         _layer_spec((1, D), 3 * layer + 1), _layer_spec((1, D), 3 * layer + 1)] + _ffn_specs(layer, 1),
        [a2, w, x2, mods, lng, lnb, weights[0], weights[1], lng, lnb], rider)


def _rope_angles(n_tokens, dim):
    rows = n_tokens // GRID_W
    row = np.repeat(np.arange(rows), GRID_W).astype(np.float32)
    col = np.tile(np.arange(GRID_W), rows).astype(np.float32)
    n_freq = dim // 4
    inv = (ROPE_BASE ** (-np.arange(n_freq, dtype=np.float32) / n_freq)).astype(np.float32)
    return np.concatenate([row[:, None] * inv, col[:, None] * inv], axis=-1)


def _da_rope_tables(n_tokens):
    ang = _rope_angles(n_tokens, DA_HEAD_DIM)
    cos, sin = np.cos(ang), np.sin(ang)
    cos128 = np.tile(cos, (1, 4))
    sin128 = np.tile(np.concatenate([-sin, sin], axis=-1), (1, 2))
    return jnp.asarray(cos128, F32), jnp.asarray(sin128, F32)


def _rt_rope_tables(n_tokens):
    ang = _rope_angles(n_tokens, RT_QK)
    return jnp.asarray(np.cos(ang), F32), jnp.asarray(np.sin(ang), F32)


def _proj_da_kernel(*refs, rope):
    if rope:
        x_ref, mod_ref, w_ref, cos_ref, sin_ref, q_ref, k_ref, v_ref = refs
        cos, sin = cos_ref[...], sin_ref[...]
        lane = lax.broadcasted_iota(jnp.int32, (1, LANES), 1)
        first_half = (lane % DA_HEAD_DIM) < (DA_HEAD_DIM // 2)
    else:
        x_ref, mod_ref, w_ref, q_ref, k_ref, v_ref = refs
    h = _modulated(x_ref[...], mod_ref, 3)

    def rot(y):
        if not rope:
            return y
        partner = jnp.where(first_half, pltpu.roll(y, LANES - 32, 1), pltpu.roll(y, 32, 1))
        return y * cos + partner * sin

    q = _dot(h, w_ref[:, 0:D]) * (DA_HEAD_DIM ** -0.5 * LOG2E)
    for j in range(D // LANES):
        q_ref[:, j * LANES:(j + 1) * LANES] = rot(q[:, j * LANES:(j + 1) * LANES]).astype(BF)
    k = _dot(h, w_ref[:, D:2 * D])
    for j in range(D // LANES):
        k_ref[:, j * LANES:(j + 1) * LANES] = rot(k[:, j * LANES:(j + 1) * LANES]).astype(BF)
    v_ref[...] = _dot(h, w_ref[:, 2 * D:3 * D]).T.astype(BF)


def _proj_da(x2, mods, layer, sel, w, rope_tabs, seq, tm):
    t = x2.shape[0]
    rpb = seq // tm
    rope = rope_tabs is not None
    in_specs = [_rows(tm, D), _mod_spec(layer, sel), _layer_spec((D, 3 * D), 0)]
    args = [x2, mods, w]
    if rope:
        in_specs += [pl.BlockSpec((tm, LANES), lambda i: (i % rpb, 0))] * 2
        args += list(rope_tabs)
    return pl.pallas_call(
        functools.partial(_proj_da_kernel, rope=rope),
        grid=(t // tm,),
        in_specs=in_specs,
        out_specs=[_rows(tm, D), _rows(tm, D),
                   pl.BlockSpec((None, D, tm), lambda i: (i // rpb, 0, i % rpb))],
        out_shape=[jax.ShapeDtypeStruct((t, D), BF), jax.ShapeDtypeStruct((t, D), BF),
                   jax.ShapeDtypeStruct((t // seq, D, seq), BF)],
        compiler_params=_params(1),
        name="proj_da",
    )(*args)


def _attn_kernel(*refs, lam_init, n_kv, parts):
    lam_ref, g_ref, q_ref = refs[:3]
    kv_refs = [(refs[3 + 2 * j], refs[4 + 2 * j]) for j in range(n_kv)]
    o_ref = refs[3 + 2 * n_kv]
    width = q_ref.shape[0] // parts
    lane = lax.broadcasted_iota(jnp.int32, (width, LANES), 1)

    def scores(part, head_map):
        q = q_ref[part * width:(part + 1) * width, :]
        keep = (lane < DA_HEAD_DIM) if head_map == 0 else (lane >= DA_HEAD_DIM)
        qm = jnp.where(keep, q, jnp.zeros_like(q))
        return [lax.dot_general(k_ref[...], qm, NT, preferred_element_type=F32) for k_ref, _ in kv_refs]

    def attend(ss):
        m = functools.reduce(jnp.maximum, [jnp.max(s, 0, keepdims=True) for s in ss])
        es = [jnp.exp2(s - m) for s in ss]
        den = functools.reduce(jnp.add, [jnp.sum(e, 0, keepdims=True) for e in es])
        o = functools.reduce(jnp.add, [_dot(vt_ref[...], e.astype(BF))
                                       for e, (_, vt_ref) in zip(es, kv_refs)])
        return o / den

    chains = [(part, head_map) for part in range(parts) for head_map in range(2)]
    result, pending = {}, None
    for chain in chains:
        ss = scores(*chain)
        if pending is not None:
            result[pending[0]] = attend(pending[1])
        pending = (chain, ss)
    result[pending[0]] = attend(pending[1])

    lam = lam_ref[...]
    lam_full = (jnp.exp(jnp.sum(lam[0:1] * lam[1:2], -1, keepdims=True))
                - jnp.exp(jnp.sum(lam[2:3] * lam[3:4], -1, keepdims=True)) + lam_init)
    for part in range(parts):
        o = result[(part, 0)] - lam_full * result[(part, 1)]
        o = o * lax.rsqrt(jnp.mean(o * o, 0, keepdims=True) + LN_EPS)
        o_ref[part * width:(part + 1) * width, :] = (o.T * g_ref[...] * (1.0 - lam_init)).astype(BF)


def _diff_attn(q, kvs, lam, subln_g, lam_init, tq):
    b, n, _ = q.shape
    in_specs = [_layer_spec((4, DA_HEAD_DIM), 0), _layer_spec((1, LANES), 0),
                pl.BlockSpec((None, tq, LANES), lambda bi, h, i: (bi, i, h))]
    args = [lam, subln_g, q]
    for k, vt in kvs:
        n_k = k.shape[1]
        in_specs += [pl.BlockSpec((None, n_k, LANES), lambda bi, h, i: (bi, 0, h)),
                     pl.BlockSpec((None, LANES, n_k), lambda bi, h, i: (bi, h, 0))]
        args += [k, vt]
    return pl.pallas_call(
        functools.partial(_attn_kernel, lam_init=lam_init, n_kv=len(kvs),
                          parts=max(1, tq // ATTN_PART)),
        grid=(b, DA_HEADS, n // tq),
        in_specs=in_specs,
        out_specs=pl.BlockSpec((None, tq, LANES), lambda bi, h, i: (bi, i, h)),
        out_shape=jax.ShapeDtypeStruct((b, n, D), BF),
        compiler_params=_params(3),
        name="diff_attn",
    )(*args)


def _halo_specs(tm, t):
    blocks_per_tile = tm // HALO
    last = t // HALO - 1
    return [pl.BlockSpec((HALO, D), lambda i: (jnp.maximum(i * blocks_per_tile - 1, 0), 0)),
            _rows(tm, D),
            pl.BlockSpec((HALO, D), lambda i: (jnp.minimum((i + 1) * blocks_per_tile, last), 0))]


def _zero_outside(u, tm, rpb):
    pos = pl.program_id(0) % rpb
    keep_prev = jnp.where(pos == 0, 0.0, 1.0)
    keep_next = jnp.where(pos == rpb - 1, 0.0, 1.0)
    return jnp.concatenate([u[:HALO] * keep_prev, u[HALO:HALO + tm], u[HALO + tm:] * keep_next], axis=0)


def _dwconv3_rows(u, w):
    n = u.shape[0]
    y = pltpu.roll(u, 1, 0) * w[0:1, :] + u * w[1:2, :] + pltpu.roll(u, n - 1, 0) * w[2:3, :]
    return y[HALO:n - HALO, :]


def _proj_hy_kernel(xp_ref, x_ref, xn_ref, mod_ref, w_ref, cw_ref, cb_ref, x0_ref, z_ref, *, tm, rpb):
    xe = jnp.concatenate([xp_ref[...], x_ref[...], xn_ref[...]], axis=0)
    h = _modulated(xe, mod_ref, 3)

    def project(j):
        return _dot(h, w_ref[:, j * D:(j + 1) * D])

    def conv(u, j):
        u = _zero_outside(u, tm, rpb)
        return _dwconv3_rows(u, cw_ref[:, j * D:(j + 1) * D]) + cb_ref[:, j * D:(j + 1) * D]

    u0 = project(0)
    u1 = project(1)
    x0_ref[...] = conv(u0, 0)
    u2 = project(2)
    x1 = conv(u1, 1)
    z_ref[...] = x1 * conv(u2, 2)


def _proj_hy(x2, mods, layer, sel, w, conv_w, conv_b, seq, tm):
    t = x2.shape[0]
    return pl.pallas_call(
        functools.partial(_proj_hy_kernel, tm=tm, rpb=seq // tm),
        grid=(t // tm,),
        in_specs=_halo_specs(tm, t) + [
            _mod_spec(layer, sel), _layer_spec((D, 3 * D), 0), _layer_spec((3, 3 * D), 0),
            _layer_spec((1, 3 * D), 0)],
        out_specs=[_rows(tm, D)] * 2,
        out_shape=[jax.ShapeDtypeStruct((t, D), F32)] * 2,
        compiler_params=_params(1),
        name="proj_hy",
    )(x2, x2, x2, mods, w, conv_w, conv_b)


def _dft_tables(n):
    m = n // 2
    k = np.arange(m + HALO, dtype=np.int64)[:, None]
    valid = k <= m
    s_even = 2 * np.arange(m, dtype=np.int64)[None, :]

    def table(fn, offset):
        phase = (k * (s_even + offset)) % (2 * n)
        return np.where(valid, fn(phase.astype(np.float64) * (math.pi / n)), 0.0).astype(np.float32)

    ce, se, co, so = table(np.cos, 0), table(np.sin, 0), table(np.cos, 1), table(np.sin, 1)
    as_bf = lambda a: jnp.asarray(np.ascontiguousarray(a)).astype(BF)
    fwd = (as_bf(np.concatenate([ce, se], 0)), as_bf(np.concatenate([co, so], 0)))
    inv = tuple(as_bf(t[:m].T) for t in (ce, se, co, so))
    return fwd, inv


def _parity_split(v, split_ref):
    n, width = v.shape
    slabs = width // LANES
    for c in range(slabs):
        split_ref[c] = v[:, c * LANES:(c + 1) * LANES]
    take = lambda start: jnp.concatenate(
        [split_ref[c, pl.ds(start, n // 2, stride=2), :] for c in range(slabs)], axis=1)
    return take(0), take(1)


def _parity_merge(even, odd, split_ref):
    m, width = even.shape
    slabs = width // LANES
    for c in range(slabs):
        split_ref[c, pl.ds(0, m, stride=2), :] = even[:, c * LANES:(c + 1) * LANES]
        split_ref[c, pl.ds(1, m, stride=2), :] = odd[:, c * LANES:(c + 1) * LANES]
    return jnp.concatenate([split_ref[c] for c in range(slabs)], axis=1)


def _hy_features(n):
    t = np.linspace(0.0, 1.0, n, dtype=np.float32)[:, None]
    bands = (HY_EMB - 1) // 2
    fr = np.linspace(1e-4, bands - 1, bands, dtype=np.float32)[None, :]
    w = (2.0 * math.pi * np.arange(n, dtype=np.float32)[:, None] / n).astype(np.float32)
    z = np.concatenate([t, np.cos(fr * w), -np.sin(fr * w)], axis=-1).astype(np.float32)
    return jnp.asarray(np.pad(z, ((0, 0), (0, LANES - HY_EMB))), F32)


def _hy_deltas():
    max_decay = math.log(1e-2) / 0.3
    min_decay = math.log(1e-2) / 1.5
    return jnp.asarray(np.abs(np.linspace(min_decay, max_decay, D, dtype=np.float32))[None, :], F32)


def _hy_filter_kernel(z_ref, w1_ref, b1_ref, f1_ref, w2_ref, b2_ref, f2_ref, w3f_ref, w3b_ref,
                      dl_ref, fe_ref, fo_ref, plo_ref, qlo_ref, phi_ref, qhi_ref, split_ref, hid_ref):
    n = z_ref.shape[0]
    mp = n // 2 + HALO
    z = z_ref[...]

    @pl.when(pl.program_id(0) == 0)
    def _():
        h = jnp.sin(f1_ref[...] * (_dot(z.astype(BF), w1_ref[...].astype(BF)) + b1_ref[...]))
        h = jnp.sin(f2_ref[...] * (_dot(h.astype(BF), w2_ref[...].astype(BF)) + b2_ref[...]))
        hid_ref[...] = h.astype(BF)

    hb16 = hid_ref[...]
    decay = jnp.exp(-z[:, 0:1] * dl_ref[...])
    h_f = _dot(hb16, w3f_ref[...].astype(BF)) * decay
    h_b = _dot(hb16, w3b_ref[...].astype(BF)) * decay
    row = lax.broadcasted_iota(jnp.int32, h_f.shape, 0)
    h_b = jnp.where(row == 0, 0.0, h_b)
    sm_even, sm_odd = _parity_split(h_f + h_b, split_ref)
    a_e = _dot(fe_ref[0:mp, :], sm_even.astype(BF))
    a_o = _dot(fo_ref[0:mp, :], sm_odd.astype(BF))
    df_even, df_odd = _parity_split(h_f - h_b, split_ref)
    b_e = _dot(fe_ref[mp:2 * mp, :], df_even.astype(BF))
    b_o = _dot(fo_ref[mp:2 * mp, :], df_odd.astype(BF))
    frow = lax.broadcasted_iota(jnp.int32, a_e.shape, 0)
    w_lo = jnp.where(frow == 0, 1.0, 2.0) * (0.5 / n)
    w_hi = jnp.where(frow == 0, 1.0, jnp.where(frow < n // 2, 2.0, 0.0)) * (0.5 / n)
    plo_ref[...] = (a_e + a_o) * w_lo
    qlo_ref[...] = -(b_e + b_o) * w_lo
    phi_ref[...] = (a_e - a_o) * w_hi
    qhi_ref[...] = (b_e - b_o) * w_hi


def _hy_filter(n, tabs, fw1, fb1, ff1, fw2, fb2, ff2, fw3, tc):
    (fwd_even, fwd_odd), _ = tabs
    w1 = jnp.pad(fw1, ((0, LANES - HY_EMB), (0, 0)))
    nt = D // tc
    m, mp = n // 2, n // 2 + HALO
    small = lambda s: pl.BlockSpec(s, lambda j: (0, 0))
    return pl.pallas_call(
        _hy_filter_kernel,
        grid=(nt,),
        in_specs=[small((n, LANES)), small((LANES, HY_FH)), small((1, HY_FH)), small((1, HY_FH)),
                  small((HY_FH, HY_FH)), small((1, HY_FH)), small((1, HY_FH)),
                  pl.BlockSpec((HY_FH, tc), lambda j: (0, j)),
                  pl.BlockSpec((HY_FH, tc), lambda j: (0, j + nt)),
                  pl.BlockSpec((1, tc), lambda j: (0, j)),
                  _resident((2 * mp, m)), _resident((2 * mp, m))],
        out_specs=[pl.BlockSpec((mp, tc), lambda j: (0, j))] * 4,
        out_shape=[jax.ShapeDtypeStruct((mp, D), F32)] * 4,
        scratch_shapes=[pltpu.VMEM((tc // LANES, n, LANES), F32), pltpu.VMEM((n, HY_FH), BF)],
        compiler_params=pltpu.CompilerParams(dimension_semantics=("arbitrary",),
                                             vmem_limit_bytes=VMEM_LIMIT),
        name="hy_filter",
    )(_hy_features(n), w1, fb1.reshape(1, -1), ff1.reshape(1, -1), fw2, fb2.reshape(1, -1),
      ff2.reshape(1, -1), fw3, fw3, _hy_deltas(), fwd_even, fwd_odd)


def _hy_conv_kernel(z_ref, x0_ref, plo_ref, qlo_ref, phi_ref, qhi_ref, ds_ref, fe_ref, fo_ref,
                    cet_ref, set_ref, cot_ref, sot_ref, o_ref, split_ref):
    n = z_ref.shape[0]
    m, mp = n // 2, n // 2 + HALO
    z = z_ref[...]
    z_even, z_odd = _parity_split(z, split_ref)
    r_e = _dot(fe_ref[...], z_even.astype(BF))
    r_o = _dot(fo_ref[...], z_odd.astype(BF))
    a_e, b_e, a_o, b_o = r_e[0:mp], r_e[mp:2 * mp], r_o[0:mp], r_o[mp:2 * mp]
    a_lo, b_lo, a_hi, b_hi = a_e + a_o, b_e + b_o, a_e - a_o, b_o - b_e
    p, q = plo_ref[...], qlo_ref[...]
    yr_lo, yi_lo = a_lo * p + b_lo * q, a_lo * q - b_lo * p
    p, q = phi_ref[...], qhi_ref[...]
    yr_hi, yi_hi = a_hi * p + b_hi * q, a_hi * q - b_hi * p
    y_even = (_dot(cet_ref[...], (yr_lo + yr_hi)[0:m].astype(BF))
              - _dot(set_ref[...], (yi_lo - yi_hi)[0:m].astype(BF)))
    y_odd = (_dot(cot_ref[...], (yr_lo - yr_hi)[0:m].astype(BF))
             - _dot(sot_ref[...], (yi_lo + yi_hi)[0:m].astype(BF)))
    half = lax.broadcasted_iota(jnp.int32, y_even.shape, 0)
    sign = (1 - 2 * (half % 2)).astype(F32)
    y_even = y_even + sign * yr_lo[m:m + 1, :]
    y_odd = y_odd - sign * yi_lo[m:m + 1, :]
    y = _parity_merge(y_even, y_odd, split_ref)
    o_ref[...] = (x0_ref[...] * (y + z * ds_ref[...])).astype(BF)


def _hy_conv(z, x0, spectra, d_skip, tabs, tc):
    b, n, _ = z.shape
    (fwd_even, fwd_odd), inv = tabs
    m, mp = n // 2, n // 2 + HALO
    sample = pl.BlockSpec((None, n, tc), lambda j, bi: (bi, 0, j))
    return pl.pallas_call(
        _hy_conv_kernel,
        grid=(D // tc, b),
        in_specs=[sample, sample] + [pl.BlockSpec((mp, tc), lambda j, bi: (0, j))] * 4
                 + [pl.BlockSpec((None, 1, tc), lambda j, bi: (0, 0, j)),
                    _resident((2 * mp, m)), _resident((2 * mp, m))] + [_resident((m, m))] * 4,
        out_specs=sample,
        out_shape=jax.ShapeDtypeStruct((b, n, D), BF),
        scratch_shapes=[pltpu.VMEM((tc // LANES, n, LANES), F32)],
        compiler_params=_params(2),
        name="hy_conv",
    )(z, x0, *spectra, d_skip, fwd_even, fwd_odd, *inv)


def _proj_rt_kernel(*refs, rope):
    if rope:
        x_ref, mod_ref, w_ref, cos_ref, sin_ref, q_ref, k_ref, v_ref, g_ref = refs
        cos, sin = cos_ref[...], sin_ref[...]
    else:
        x_ref, mod_ref, w_ref, k_ref, v_ref = refs
    h = _modulated(x_ref[...], mod_ref, 3)
    qk_w = RT_HEADS * RT_QK
    half = RT_QK // 2

    def store_rot(y, ref):
        for hd in range(RT_HEADS):
            lo = hd * RT_QK
            x1, x2 = y[:, lo:lo + half], y[:, lo + half:lo + RT_QK]
            if rope:
                x1, x2 = x1 * cos - x2 * sin, x1 * sin + x2 * cos
            ref[:, lo:lo + half] = x1.astype(BF)
            ref[:, lo + half:lo + RT_QK] = x2.astype(BF)

    if rope:
        store_rot(_dot(h, w_ref[:, 0:qk_w]), q_ref)
    store_rot(_dot(h, w_ref[:, qk_w:2 * qk_w]) * (RT_QK ** -0.5), k_ref)
    v_w = RT_HEADS * RT_V
    v_ref[...] = _dot(h, w_ref[:, 2 * qk_w:2 * qk_w + v_w]).astype(BF)
    if rope:
        g_ref[...] = _silu(_dot(h, w_ref[:, 2 * qk_w + v_w:2 * qk_w + 2 * v_w])).astype(BF)


def _proj_rt(x2, mods, layer, sel, w, rope_tabs, seq, tm):
    t = x2.shape[0]
    rpb = seq // tm
    rope = rope_tabs is not None
    qk_w, v_w = RT_HEADS * RT_QK, RT_HEADS * RT_V
    in_specs = [_rows(tm, D), _mod_spec(layer, sel), _layer_spec((D, 2 * qk_w + 2 * v_w), 0)]
    args = [x2, mods, w]
    if rope:
        in_specs += [pl.BlockSpec((tm, LANES), lambda i: (i % rpb, 0))] * 2
        args += list(rope_tabs)
        out_specs = [_rows(tm, qk_w), _rows(tm, qk_w), _rows(tm, v_w), _rows(tm, v_w)]
        out_shape = [jax.ShapeDtypeStruct((t, qk_w), BF), jax.ShapeDtypeStruct((t, qk_w), BF),
                     jax.ShapeDtypeStruct((t, v_w), BF), jax.ShapeDtypeStruct((t, v_w), BF)]
    else:
        out_specs = [_rows(tm, qk_w), _rows(tm, v_w)]
        out_shape = [jax.ShapeDtypeStruct((t, qk_w), BF), jax.ShapeDtypeStruct((t, v_w), BF)]
    return pl.pallas_call(
        functools.partial(_proj_rt_kernel, rope=rope),
        grid=(t // tm,),
        in_specs=in_specs,
        out_specs=out_specs,
        out_shape=out_shape,
        compiler_params=_params(1),
        name="proj_rt",
    )(*args)


def _ret_kernel(logit_ref, kc_ref, vc_ref, ql_ref, kl_ref, vl_ref, g_ref, gn_ref, o_ref,
                s_ref, acc_ref, *, chunk):
    hd = pl.program_id(1)
    ctx_chunk = min(chunk, kc_ref.shape[0])
    n_ctx = kc_ref.shape[0] // ctx_chunk
    n_lat = ql_ref.shape[0] // chunk
    row = lax.broadcasted_iota(jnp.int32, (chunk, chunk), 0)
    col = lax.broadcasted_iota(jnp.int32, (chunk, chunk), 1)

    def log_gamma(direction, shape):
        return jnp.log(jax.nn.sigmoid(jnp.full(shape, logit_ref[direction, hd], F32)))

    def decay(direction, size, power):
        pos = lax.broadcasted_iota(jnp.int32, (size, 1), 0).astype(F32)
        return jnp.exp(power(pos) * log_gamma(direction, (size, 1)))

    def write_decays(direction, size):
        d_write = decay(direction, size, (lambda p: size - 1.0 - p) if direction == 0 else (lambda p: p))
        return d_write, jnp.exp(size * log_gamma(direction, (1, 1)))

    intra, read = [], []
    for direction in (0, 1):
        lag = (row - col) if direction == 0 else (col - row)
        intra.append(jnp.where(lag >= 0, jnp.exp(jnp.maximum(lag, 0).astype(F32)
                                                 * log_gamma(direction, (chunk, 1))), 0.0))
        read.append(decay(direction, chunk, (lambda p: p + 1.0) if direction == 0 else (lambda p: chunk - p)))
    lat_write = [write_decays(direction, chunk) for direction in (0, 1)]
    ctx_write = [write_decays(direction, ctx_chunk) for direction in (0, 1)]

    s_ref[...] = jnp.zeros_like(s_ref)

    def absorb(direction, k_ref, v_ref, rows, decays):
        d_write, d_block = decays[direction]
        kw = (k_ref[rows, :].astype(F32) * d_write).astype(BF)
        s_ref[direction] = d_block * s_ref[direction] + lax.dot_general(
            kw, v_ref[rows, :], TN, preferred_element_type=F32)

    for direction in (0, 1):
        for c in (range(n_ctx) if direction == 0 else reversed(range(n_ctx))):
            absorb(direction, kc_ref, vc_ref, pl.ds(c * ctx_chunk, ctx_chunk), ctx_write)

    def body(t, carry):
        for direction in (0, 1):
            c = t if direction == 0 else n_lat - 1 - t
            rows = pl.ds(pl.multiple_of(c * chunk, chunk), chunk)
            qc = ql_ref[rows, :]
            scores = (lax.dot_general(qc, kl_ref[rows, :], NT, preferred_element_type=F32)
                      * intra[direction])
            acc_ref[direction, rows, :] = (_dot(scores.astype(BF), vl_ref[rows, :])
                                           + _dot(qc, s_ref[direction].astype(BF)) * read[direction])
            absorb(direction, kl_ref, vl_ref, rows, lat_write)
        return carry

    lax.fori_loop(0, n_lat, body, 0)

    o = acc_ref[0] + acc_ref[1]
    mu = jnp.mean(o, -1, keepdims=True)
    dlt = o - mu
    var = jnp.mean(dlt * dlt, -1, keepdims=True)
    o_ref[...] = (g_ref[...].astype(F32) * (dlt * lax.rsqrt(var + LN_EPS) * gn_ref[...])).astype(BF)


def _retention(kc, vc, ql, kl, vl, g, decay_logit, gn_g):
    b, n, _ = ql.shape
    nc = kc.shape[1]
    qk = lambda m: pl.BlockSpec((None, m, RT_QK), lambda bi, h: (bi, 0, h))
    vv = lambda m: pl.BlockSpec((None, m, RT_V), lambda bi, h: (bi, 0, h))
    return pl.pallas_call(
        functools.partial(_ret_kernel, chunk=RT_CHUNK),
        grid=(b, RT_HEADS),
        in_specs=[pl.BlockSpec(memory_space=pltpu.SMEM),
                  qk(nc), vv(nc), qk(n), qk(n), vv(n), vv(n),
                  pl.BlockSpec((None, 1, RT_V), lambda bi, h: (0, 0, h))],
        out_specs=vv(n),
        out_shape=jax.ShapeDtypeStruct((b, n, RT_HEADS * RT_V), BF),
        scratch_shapes=[pltpu.VMEM((2, RT_QK, RT_V), F32), pltpu.VMEM((2, n, RT_V), F32)],
        compiler_params=_params(2),
        name="retention",
    )(decay_logit, kc, vc, ql, kl, vl, g, gn_g)


def _sc_ffn_kernel(xp_ref, x_ref, xn_ref, mod_ref, wi_ref, cw_ref, wo_ref, g1_ref, b1_ref,
                   fwi_ref, fwo_ref, g2_ref, b2_ref, o_ref, act_ref, *, tm, rpb):
    x = x_ref[...]
    xe = jnp.concatenate([xp_ref[...], x, xn_ref[...]], axis=0)
    h = _modulated(xe, mod_ref, 3)
    cu = _dot(h, wi_ref[:, D:2 * D]) * _dot(h, wi_ref[:, 2 * D:3 * D])
    b_gate = _dot(h[HALO:HALO + tm, :], wi_ref[:, 0:D])
    conv = _dwconv3_rows(_zero_outside(cu, tm, rpb), cw_ref[...])
    y = _dot((b_gate * conv).astype(BF), wo_ref[...])
    x1 = _layer_norm(ALPHA * x + mod_ref[5:6, :] * y, g1_ref[...], b1_ref[...])
    _ffn_slabs(lambda p: x1[p * FFN_SLAB:(p + 1) * FFN_SLAB, :], tm // FFN_SLAB, mod_ref, 6,
               fwi_ref, fwo_ref, g2_ref, b2_ref, o_ref, act_ref)


def _short_conv_ffn(x2, mods, layer, sel, wi, conv_w, wo, fwi, fwo, lng, lnb, seq, tm):
    t = x2.shape[0]
    return pl.pallas_call(
        functools.partial(_sc_ffn_kernel, tm=tm, rpb=seq // tm),
        grid=(t // tm,),
        in_specs=_halo_specs(tm, t) + [
            _mod_spec(layer, sel), _layer_spec((D, 3 * D), 0), _layer_spec((3, D), 0),
            _layer_spec((D, D), 0), _layer_spec((1, D), 3 * layer + 1),
            _layer_spec((1, D), 3 * layer + 1)] + _ffn_specs(layer, 1),
        out_specs=_rows(tm, D),
        out_shape=jax.ShapeDtypeStruct((t, D), F32),
        scratch_shapes=[pltpu.VMEM((tm, D_FF), BF)],
        compiler_params=_params(1),
        name="short_conv_ffn",
    )(x2, x2, x2, mods, wi, conv_w, wo, lng, lnb, fwi, fwo, lng, lnb)


def kernel(x, c, ctx, c_ctx, ada_w, ada_b, ln_g, ln_b, ffa_wi, ffa_wo, ffb_wi, ffb_wo, da_w_qkv, da_w_o, da_lambda, da_subln_g, hy_w_in, hy_conv_w, hy_conv_b, hy_fw1, hy_fb1, hy_ff1, hy_fw2, hy_fb2, hy_ff2, hy_fw3, hy_d_skip, hy_w_o, rt_w_in, rt_decay_logit, rt_gn_g, rt_w_o, sc_w_in, sc_conv_w, sc_w_o):
    bsz, seq, _ = x.shape
    n_ctx = ctx.shape[1]
    assert x.shape[2] == D and ada_w.shape[0] == DEPTH and seq % GRID_W == 0
    tm = min(512, seq)
    tmc = min(512, n_ctx)
    assert seq % tm == 0 and n_ctx % tmc == 0 and seq % RT_CHUNK == 0 and n_ctx % min(RT_CHUNK, n_ctx) == 0

    n_rows = -(-(bsz + 1) // HALO) * HALO
    cond = jnp.zeros((n_rows, D), F32).at[:bsz].set(c).at[bsz].set(c_ctx)
    mods = _modulation_all(cond, ada_w, ada_b)

    rpb = seq // tm
    lat = (lambda i: i // rpb, seq, tm)
    cx = (lambda i: bsz, n_ctx, tmc)

    xl = x.reshape(bsz * seq, D)
    xc = ctx.reshape(bsz * n_ctx, D)
    lng = ln_g.reshape(DEPTH * 3, 1, D)
    lnb = ln_b.reshape(DEPTH * 3, 1, D)
    ffw = {(0, 0): (ffa_wi[0].astype(BF), ffa_wo[0].astype(BF))}
    f32_stacks = {0: (ffa_wi, ffa_wo), 1: (ffb_wi, ffb_wo)}

    def rider_for(i, which, stream):
        nxt = (i, 1) if which == 0 else (i + 1, 0)
        if stream is not lat or nxt[0] >= DEPTH:
            return None, nxt
        return (nxt[0],) + f32_stacks[nxt[1]], nxt

    def row_tile(stream, want):
        rows = seq if stream is lat else bsz * n_ctx
        tile = min(want, rows)
        assert rows % tile == 0
        sel = (lambda j: j // (seq // tile)) if stream is lat else stream[0]
        return tile, sel

    def ffn_a(xx, i, stream):
        tile, sel = row_tile(stream, FFN_TILE)
        rider, nxt = rider_for(i, 0, stream)
        out, cast = _half_ffn(xx, mods, i, 0, sel, ffw[(i, 0)], lng, lnb, tile, rider)
        if cast is not None:
            ffw[nxt] = cast
        return out

    def mix_out(a, w_o, xx, i, stream):
        tile, sel = row_tile(stream, tm)
        rider, nxt = rider_for(i, 1, stream)
        out, cast = _outproj_ffn(a, w_o, xx, mods, i, sel, ffw[(i, 1)], lng, lnb, tile, rider)
        if cast is not None:
            ffw[nxt] = cast
        return out

    def by_sample(a, n):
        return a.reshape(bsz, n, a.shape[-1])

    i = 0
    xl, xc = ffn_a(xl, i, lat), ffn_a(xc, i, cx)
    w_qkv = da_w_qkv.astype(BF)
    ql, kl, vtl = _proj_da(xl, mods, i, lat[0], w_qkv, _da_rope_tables(seq), seq, tm)
    qc, kc, vtc = _proj_da(xc, mods, i, cx[0], w_qkv, None, n_ctx, tmc)
    ql, kl, qc, kc = by_sample(ql, seq), by_sample(kl, seq), by_sample(qc, n_ctx), by_sample(kc, n_ctx)
    lam_init = 0.8 - 0.6 * math.exp(-0.3 * i)
    subln = da_subln_g.reshape(-1, 1, LANES)
    ol = _diff_attn(ql, [(kc, vtc), (kl, vtl)], da_lambda, subln, lam_init, min(ATTN_TILE, seq))
    oc = _diff_attn(qc, [(kc, vtc)], da_lambda, subln, lam_init, min(ATTN_TILE, n_ctx))
    w_o = da_w_o.astype(BF)
    xl = mix_out(ol.reshape(bsz * seq, D), w_o, xl, i, lat)
    xc = mix_out(oc.reshape(bsz * n_ctx, D), w_o, xc, i, cx)

    i = 1
    xl, xc = ffn_a(xl, i, lat), ffn_a(xc, i, cx)
    w_in = hy_w_in.astype(BF)
    w_o = hy_w_o.astype(BF)
    conv_b = hy_conv_b.reshape(-1, 1, 3 * D)
    d_skip = hy_d_skip.reshape(-1, 1, D)
    tc = 256

    def hyena(xx, stream):
        sel, n, tile = stream
        tabs = _dft_tables(n)
        spectra = _hy_filter(n, tabs, hy_fw1[0], hy_fb1[0], hy_ff1[0], hy_fw2[0], hy_fb2[0],
                             hy_ff2[0], hy_fw3[0], tc)
        x0, z = _proj_hy(xx, mods, i, sel, w_in, hy_conv_w, conv_b, n, tile)
        y = _hy_conv(by_sample(z, n), by_sample(x0, n), spectra, d_skip, tabs, tc)
        return mix_out(y.reshape(bsz * n, D), w_o, xx, i, stream)

    xl, xc = hyena(xl, lat), hyena(xc, cx)

    i = 2
    xl, xc = ffn_a(xl, i, lat), ffn_a(xc, i, cx)
    w_in = rt_w_in.astype(BF)
    ql, kl, vl, gl = [by_sample(a, seq) for a in
                      _proj_rt(xl, mods, i, lat[0], w_in, _rt_rope_tables(seq), seq, tm)]
    kc, vc = [by_sample(a, n_ctx) for a in _proj_rt(xc, mods, i, cx[0], w_in, None, n_ctx, tmc)]
    o = _retention(kc, vc, ql, kl, vl, gl, rt_decay_logit[0], rt_gn_g.reshape(-1, 1, RT_HEADS * RT_V))
    xl = mix_out(o.reshape(bsz * seq, RT_HEADS * RT_V), rt_w_o.astype(BF), xl, i, lat)

    i = 3
    xl = ffn_a(xl, i, lat)
    xl = _short_conv_ffn(xl, mods, i, lat[0], sc_w_in.astype(BF), sc_conv_w, sc_w_o.astype(BF),
                         ffw[(i, 1)][0], ffw[(i, 1)][1], lng, lnb, seq, tm)
    return xl.reshape(bsz, seq, D)
```

```python
import functools
import math

import jax
import jax.numpy as jnp
import numpy as np
from jax import lax
from jax.experimental import pallas as pl
from jax.experimental.pallas import tpu as pltpu

D = 1024
N_MOD = 9
D_FF = 2816
LN_EPS = 1e-5
ROPE_BASE = 10000.0
GRID_W = 64
DEPTH = 4
ALPHA = (2.0 * DEPTH) ** 0.25
DA_HEADS = 8
DA_HEAD_DIM = 64
RT_HEADS = 4
RT_QK = 256
RT_V = 512
HY_EMB = 33
HY_FH = 64
LANES = 128
HALO = 8
FF_CHUNK = 256
RT_CHUNK = 256
FFN_TILE = 1024
FFN_SLAB = 256
RIDER_BLOCKS = 16
ATTN_TILE = 2048
ATTN_PART = 512
VMEM_LIMIT = 56 * 1024 * 1024
LOG2E = math.log2(math.e)

F32 = jnp.float32
BF = jnp.bfloat16
NT = (((1,), (1,)), ((), ()))
TN = (((0,), (0,)), ((), ()))


def _dot(a, b):
    return jnp.dot(a, b, preferred_element_type=F32)


def _resident(shape):
    nd = len(shape)
    return pl.BlockSpec(shape, lambda *_: (0,) * nd, pipeline_mode=pl.Buffered(1))


def _layer_spec(shape, layer):
    nd = len(shape)
    return pl.BlockSpec((None,) + tuple(shape), lambda *_: (layer,) + (0,) * nd,
                        pipeline_mode=pl.Buffered(1))


def _mod_spec(layer, sel):
    return pl.BlockSpec((None, None, N_MOD, D), lambda i: (layer, sel(i), 0, 0))


def _rows(tm, width):
    return pl.BlockSpec((tm, width), lambda i: (i, 0))


def _params(n_axes):
    return pltpu.CompilerParams(dimension_semantics=("parallel",) * n_axes,
                                vmem_limit_bytes=VMEM_LIMIT)


def _layer_norm(r, g, b):
    mu = jnp.mean(r, -1, keepdims=True)
    d = r - mu
    var = jnp.mean(d * d, -1, keepdims=True)
    return d * lax.rsqrt(var + LN_EPS) * g + b


def _silu(a):
    return a * jax.nn.sigmoid(a)


def _modulated(x, mod_ref, j):
    return (x * (1.0 + mod_ref[j + 1:j + 2, :]) + mod_ref[j:j + 1, :]).astype(BF)


def _mod_kernel(c_ref, w_ref, b_ref, o_ref):
    s = _silu(c_ref[...]).astype(BF)
    o_ref[...] = _dot(s, w_ref[...].astype(BF)) + b_ref[...]


def _modulation_all(cond, ada_w, ada_b):
    r = cond.shape[0]
    tn = 2304
    n = N_MOD * D
    out = pl.pallas_call(
        _mod_kernel,
        grid=(DEPTH, n // tn),
        in_specs=[pl.BlockSpec((r, D), lambda i, j: (0, 0)),
                  pl.BlockSpec((None, D, tn), lambda i, j: (i, 0, j)),
                  pl.BlockSpec((None, 1, tn), lambda i, j: (i, 0, j))],
        out_specs=pl.BlockSpec((None, r, tn), lambda i, j: (i, 0, j)),
        out_shape=jax.ShapeDtypeStruct((DEPTH, r, n), F32),
        compiler_params=_params(2),
        name="modulation",
    )(cond, ada_w, ada_b.reshape(DEPTH, 1, n))
    return out.reshape(DEPTH, r, N_MOD, D)


def _ffn_slabs(x_of, n_slabs, mod_ref, j0, wi_ref, wo_ref, g_ref, b_ref, o_ref, act_ref):
    rows = o_ref.shape[0] // n_slabs
    xs = {}

    def up(p):
        xs[p] = x_of(p)
        h = _modulated(xs[p], mod_ref, j0)
        for c in range(D_FF // FF_CHUNK):
            lo = c * FF_CHUNK
            a = _dot(h, wi_ref[:, lo:lo + FF_CHUNK])
            u = _dot(h, wi_ref[:, D_FF + lo:D_FF + lo + FF_CHUNK])
            act_ref[p * rows:(p + 1) * rows, lo:lo + FF_CHUNK] = (_silu(a) * u).astype(BF)

    def down(p):
        y = _dot(act_ref[p * rows:(p + 1) * rows, :], wo_ref[...])
        r = ALPHA * xs.pop(p) + (0.5 * mod_ref[j0 + 2:j0 + 3, :]) * y
        o_ref[p * rows:(p + 1) * rows, :] = _layer_norm(r, g_ref[...], b_ref[...])

    for p in range(n_slabs):
        up(p)
        if p > 0:
            down(p - 1)
    down(n_slabs - 1)


def _ffn_specs(layer, which):
    return [_resident((D, 2 * D_FF)), _resident((D_FF, D)),
            _layer_spec((1, D), 3 * layer + 2 * which), _layer_spec((1, D), 3 * layer + 2 * which)]


def _rider_specs(jobs, steps):
    blocks = min(RIDER_BLOCKS, steps)
    reps = steps // blocks
    assert steps % blocks == 0
    in_specs, out_specs, out_shape = [], [], []
    for stack, layer in jobs:
        _, r, c = stack.shape
        in_specs.append(pl.BlockSpec((None, r // blocks, c), lambda i, layer=layer: (layer, i // reps, 0)))
        out_specs.append(pl.BlockSpec((r // blocks, c), lambda i: (i // reps, 0)))
        out_shape.append(jax.ShapeDtypeStruct((r, c), BF))
    return in_specs, out_specs, out_shape


def _split_rider(rest, n_jobs):
    o_ref, act_ref = rest[n_jobs], rest[-1]
    for src_ref, dst_ref in zip(rest[:n_jobs], rest[n_jobs + 1:-1]):
        dst_ref[...] = src_ref[...].astype(BF)
    return o_ref, act_ref


def _ffn_kernel(x_ref, mod_ref, wi_ref, wo_ref, g_ref, b_ref, *rest, j0, n_slabs, n_jobs):
    o_ref, act_ref = _split_rider(rest, n_jobs)
    rows = x_ref.shape[0] // n_slabs
    _ffn_slabs(lambda p: x_ref[p * rows:(p + 1) * rows, :], n_slabs, mod_ref, j0, wi_ref, wo_ref,
               g_ref, b_ref, o_ref, act_ref)


def _call_with_rider(kernel_fn, name, t, tm, in_specs, args, jobs):
    r_in, r_out, r_shape = _rider_specs(jobs, t // tm)
    outs = pl.pallas_call(
        functools.partial(kernel_fn, n_jobs=len(jobs)),
        grid=(t // tm,),
        in_specs=in_specs + r_in,
        out_specs=[_rows(tm, D)] + r_out,
        out_shape=[jax.ShapeDtypeStruct((t, D), F32)] + r_shape,
        scratch_shapes=[pltpu.VMEM((tm, D_FF), BF)],
        compiler_params=_params(1),
        name=name,
    )(*args, *[stack for stack, _ in jobs])
    return outs[0], list(outs[1:])


def _half_ffn(x2, mods, layer, which, sel, weights, lng, lnb, tm, rider):
    return _call_with_rider(
        functools.partial(_ffn_kernel, j0=6 * which, n_slabs=tm // FFN_SLAB), "half_ffn", x2.shape[0], tm,
        [_rows(tm, D), _mod_spec(layer, sel)] + _ffn_specs(layer, which),
        [x2, mods, weights[0], weights[1], lng, lnb], rider)


def _outproj_ffn_kernel(a_ref, w_ref, x_ref, mod_ref, g1_ref, b1_ref, wi_ref, wo_ref, g2_ref, b2_ref,
                        *rest, n_jobs):
    o_ref, act_ref = _split_rider(rest, n_jobs)
    n_slabs = x_ref.shape[0] // FFN_SLAB

    def mixed(p):
        rows = slice(p * FFN_SLAB, (p + 1) * FFN_SLAB)
        y = _dot(a_ref[rows, :], w_ref[...])
        return _layer_norm(ALPHA * x_ref[rows, :] + mod_ref[5:6, :] * y, g1_ref[...], b1_ref[...])

    _ffn_slabs(mixed, n_slabs, mod_ref, 6, wi_ref, wo_ref, g2_ref, b2_ref, o_ref, act_ref)


def _outproj_ffn(a2, w, x2, mods, layer, sel, weights, lng, lnb, tm, rider):
    t, k = a2.shape
    return _call_with_rider(
        _outproj_ffn_kernel, "outproj_ffn", t, tm,
        [_rows(tm, k), _layer_spec((k, D), 0), _rows(tm, D), _mod_spec(layer, sel),
         _layer_spec((1, D), 3 * layer + 1), _layer_spec((1, D), 3 * layer + 1)] + _ffn_specs(layer, 1),
        [a2, w, x2, mods, lng, lnb, weights[0], weights[1], lng, lnb], rider)


def _rope_angles(n_tokens, dim):
    rows = n_tokens // GRID_W
    row = np.repeat(np.arange(rows), GRID_W).astype(np.float32)
    col = np.tile(np.arange(GRID_W), rows).astype(np.float32)
    n_freq = dim // 4
    inv = (ROPE_BASE ** (-np.arange(n_freq, dtype=np.float32) / n_freq)).astype(np.float32)
    return np.concatenate([row[:, None] * inv, col[:, None] * inv], axis=-1)


def _da_rope_tables(n_tokens):
    ang = _rope_angles(n_tokens, DA_HEAD_DIM)
    cos, sin = np.cos(ang), np.sin(ang)
    cos128 = np.tile(cos, (1, 4))
    sin128 = np.tile(np.concatenate([-sin, sin], axis=-1), (1, 2))
    return jnp.asarray(cos128, F32), jnp.asarray(sin128, F32)


def _rt_rope_tables(n_tokens):
    ang = _rope_angles(n_tokens, RT_QK)
    return jnp.asarray(np.cos(ang), F32), jnp.asarray(np.sin(ang), F32)


def _proj_da_kernel(*refs, rope):
    if rope:
        x_ref, mod_ref, w_ref, cos_ref, sin_ref, q_ref, k_ref, v_ref = refs
        cos, sin = cos_ref[...], sin_ref[...]
        lane = lax.broadcasted_iota(jnp.int32, (1, LANES), 1)
        first_half = (lane % DA_HEAD_DIM) < (DA_HEAD_DIM // 2)
    else:
        x_ref, mod_ref, w_ref, q_ref, k_ref, v_ref = refs
    h = _modulated(x_ref[...], mod_ref, 3)

    def rot(y):
        if not rope:
            return y
        partner = jnp.where(first_half, pltpu.roll(y, LANES - 32, 1), pltpu.roll(y, 32, 1))
        return y * cos + partner * sin

    q = _dot(h, w_ref[:, 0:D]) * (DA_HEAD_DIM ** -0.5 * LOG2E)
    for j in range(D // LANES):
        q_ref[:, j * LANES:(j + 1) * LANES] = rot(q[:, j * LANES:(j + 1) * LANES]).astype(BF)
    k = _dot(h, w_ref[:, D:2 * D])
    for j in range(D // LANES):
        k_ref[:, j * LANES:(j + 1) * LANES] = rot(k[:, j * LANES:(j + 1) * LANES]).astype(BF)
    v_ref[...] = _dot(h, w_ref[:, 2 * D:3 * D]).T.astype(BF)


def _proj_da(x2, mods, layer, sel, w, rope_tabs, seq, tm):
    t = x2.shape[0]
    rpb = seq // tm
    rope = rope_tabs is not None
    in_specs = [_rows(tm, D), _mod_spec(layer, sel), _layer_spec((D, 3 * D), 0)]
    args = [x2, mods, w]
    if rope:
        in_specs += [pl.BlockSpec((tm, LANES), lambda i: (i % rpb, 0))] * 2
        args += list(rope_tabs)
    return pl.pallas_call(
        functools.partial(_proj_da_kernel, rope=rope),
        grid=(t // tm,),
        in_specs=in_specs,
        out_specs=[_rows(tm, D), _rows(tm, D),
                   pl.BlockSpec((None, D, tm), lambda i: (i // rpb, 0, i % rpb))],
        out_shape=[jax.ShapeDtypeStruct((t, D), BF), jax.ShapeDtypeStruct((t, D), BF),
                   jax.ShapeDtypeStruct((t // seq, D, seq), BF)],
        compiler_params=_params(1),
        name="proj_da",
    )(*args)


def _attn_kernel(*refs, lam_init, n_kv, parts):
    lam_ref, g_ref, q_ref = refs[:3]
    kv_refs = [(refs[3 + 2 * j], refs[4 + 2 * j]) for j in range(n_kv)]
    o_ref = refs[3 + 2 * n_kv]
    width = q_ref.shape[0] // parts
    lane = lax.broadcasted_iota(jnp.int32, (width, LANES), 1)

    def scores(part, head_map):
        q = q_ref[part * width:(part + 1) * width, :]
        keep = (lane < DA_HEAD_DIM) if head_map == 0 else (lane >= DA_HEAD_DIM)
        qm = jnp.where(keep, q, jnp.zeros_like(q))
        return [lax.dot_general(k_ref[...], qm, NT, preferred_element_type=F32) for k_ref, _ in kv_refs]

    def attend(ss):
        m = functools.reduce(jnp.maximum, [jnp.max(s, 0, keepdims=True) for s in ss])
        es = [jnp.exp2(s - m) for s in ss]
        den = functools.reduce(jnp.add, [jnp.sum(e, 0, keepdims=True) for e in es])
        o = functools.reduce(jnp.add, [_dot(vt_ref[...], e.astype(BF))
                                       for e, (_, vt_ref) in zip(es, kv_refs)])
        return o / den

    chains = [(part, head_map) for part in range(parts) for head_map in range(2)]
    result, pending = {}, None
    for chain in chains:
        ss = scores(*chain)
        if pending is not None:
            result[pending[0]] = attend(pending[1])
        pending = (chain, ss)
    result[pending[0]] = attend(pending[1])

    lam = lam_ref[...]
    lam_full = (jnp.exp(jnp.sum(lam[0:1] * lam[1:2], -1, keepdims=True))
                - jnp.exp(jnp.sum(lam[2:3] * lam[3:4], -1, keepdims=True)) + lam_init)
    for part in range(parts):
        o = result[(part, 0)] - lam_full * result[(part, 1)]
        o = o * lax.rsqrt(jnp.mean(o * o, 0, keepdims=True) + LN_EPS)
        o_ref[part * width:(part + 1) * width, :] = (o.T * g_ref[...] * (1.0 - lam_init)).astype(BF)


def _diff_attn(q, kvs, lam, subln_g, lam_init, tq):
    b, n, _ = q.shape
    in_specs = [_layer_spec((4, DA_HEAD_DIM), 0), _layer_spec((1, LANES), 0),
                pl.BlockSpec((None, tq, LANES), lambda bi, h, i: (bi, i, h))]
    args = [lam, subln_g, q]
    for k, vt in kvs:
        n_k = k.shape[1]
        in_specs += [pl.BlockSpec((None, n_k, LANES), lambda bi, h, i: (bi, 0, h)),
                     pl.BlockSpec((None, LANES, n_k), lambda bi, h, i: (bi, h, 0))]
        args += [k, vt]
    return pl.pallas_call(
        functools.partial(_attn_kernel, lam_init=lam_init, n_kv=len(kvs),
                          parts=max(1, tq // ATTN_PART)),
        grid=(b, DA_HEADS, n // tq),
        in_specs=in_specs,
        out_specs=pl.BlockSpec((None, tq, LANES), lambda bi, h, i: (bi, i, h)),
        out_shape=jax.ShapeDtypeStruct((b, n, D), BF),
        compiler_params=_params(3),
        name="diff_attn",
    )(*args)


def _halo_specs(tm, t):
    blocks_per_tile = tm // HALO
    last = t // HALO - 1
    return [pl.BlockSpec((HALO, D), lambda i: (jnp.maximum(i * blocks_per_tile - 1, 0), 0)),
            _rows(tm, D),
            pl.BlockSpec((HALO, D), lambda i: (jnp.minimum((i + 1) * blocks_per_tile, last), 0))]


def _zero_outside(u, tm, rpb):
    pos = pl.program_id(0) % rpb
    keep_prev = jnp.where(pos == 0, 0.0, 1.0)
    keep_next = jnp.where(pos == rpb - 1, 0.0, 1.0)
    return jnp.concatenate([u[:HALO] * keep_prev, u[HALO:HALO + tm], u[HALO + tm:] * keep_next], axis=0)


def _dwconv3_rows(u, w):
    n = u.shape[0]
    y = pltpu.roll(u, 1, 0) * w[0:1, :] + u * w[1:2, :] + pltpu.roll(u, n - 1, 0) * w[2:3, :]
    return y[HALO:n - HALO, :]


def _proj_hy_kernel(xp_ref, x_ref, xn_ref, mod_ref, w_ref, cw_ref, cb_ref, x0_ref, z_ref, *, tm, rpb):
    xe = jnp.concatenate([xp_ref[...], x_ref[...], xn_ref[...]], axis=0)
    h = _modulated(xe, mod_ref, 3)

    def project(j):
        return _dot(h, w_ref[:, j * D:(j + 1) * D])

    def conv(u, j):
        u = _zero_outside(u, tm, rpb)
        return _dwconv3_rows(u, cw_ref[:, j * D:(j + 1) * D]) + cb_ref[:, j * D:(j + 1) * D]

    u0 = project(0)
    u1 = project(1)
    x0_ref[...] = conv(u0, 0)
    u2 = project(2)
    x1 = conv(u1, 1)
    z_ref[...] = x1 * conv(u2, 2)


def _proj_hy(x2, mods, layer, sel, w, conv_w, conv_b, seq, tm):
    t = x2.shape[0]
    return pl.pallas_call(
        functools.partial(_proj_hy_kernel, tm=tm, rpb=seq // tm),
        grid=(t // tm,),
        in_specs=_halo_specs(tm, t) + [
            _mod_spec(layer, sel), _layer_spec((D, 3 * D), 0), _layer_spec((3, 3 * D), 0),
            _layer_spec((1, 3 * D), 0)],
        out_specs=[_rows(tm, D)] * 2,
        out_shape=[jax.ShapeDtypeStruct((t, D), F32)] * 2,
        compiler_params=_params(1),
        name="proj_hy",
    )(x2, x2, x2, mods, w, conv_w, conv_b)


def _dft_tables(n):
    m = n // 2
    k = np.arange(m + HALO, dtype=np.int64)[:, None]
    valid = k <= m
    s_even = 2 * np.arange(m, dtype=np.int64)[None, :]

    def table(fn, offset):
        phase = (k * (s_even + offset)) % (2 * n)
        return np.where(valid, fn(phase.astype(np.float64) * (math.pi / n)), 0.0).astype(np.float32)

    ce, se, co, so = table(np.cos, 0), table(np.sin, 0), table(np.cos, 1), table(np.sin, 1)
    as_bf = lambda a: jnp.asarray(np.ascontiguousarray(a)).astype(BF)
    fwd = (as_bf(np.concatenate([ce, se], 0)), as_bf(np.concatenate([co, so], 0)))
    inv = tuple(as_bf(t[:m].T) for t in (ce, se, co, so))
    return fwd, inv


def _parity_split(v, split_ref):
    n, width = v.shape
    slabs = width // LANES
    for c in range(slabs):
        split_ref[c] = v[:, c * LANES:(c + 1) * LANES]
    take = lambda start: jnp.concatenate(
        [split_ref[c, pl.ds(start, n // 2, stride=2), :] for c in range(slabs)], axis=1)
    return take(0), take(1)


def _parity_merge(even, odd, split_ref):
    m, width = even.shape
    slabs = width // LANES
    for c in range(slabs):
        split_ref[c, pl.ds(0, m, stride=2), :] = even[:, c * LANES:(c + 1) * LANES]
        split_ref[c, pl.ds(1, m, stride=2), :] = odd[:, c * LANES:(c + 1) * LANES]
    return jnp.concatenate([split_ref[c] for c in range(slabs)], axis=1)


def _hy_features(n):
    t = np.linspace(0.0, 1.0, n, dtype=np.float32)[:, None]
    bands = (HY_EMB - 1) // 2
    fr = np.linspace(1e-4, bands - 1, bands, dtype=np.float32)[None, :]
    w = (2.0 * math.pi * np.arange(n, dtype=np.float32)[:, None] / n).astype(np.float32)
    z = np.concatenate([t, np.cos(fr * w), -np.sin(fr * w)], axis=-1).astype(np.float32)
    return jnp.asarray(np.pad(z, ((0, 0), (0, LANES - HY_EMB))), F32)


def _hy_deltas():
    max_decay = math.log(1e-2) / 0.3
    min_decay = math.log(1e-2) / 1.5
    return jnp.asarray(np.abs(np.linspace(min_decay, max_decay, D, dtype=np.float32))[None, :], F32)


def _hy_filter_kernel(z_ref, w1_ref, b1_ref, f1_ref, w2_ref, b2_ref, f2_ref, w3f_ref, w3b_ref,
                      dl_ref, fe_ref, fo_ref, plo_ref, qlo_ref, phi_ref, qhi_ref, split_ref, hid_ref):
    n = z_ref.shape[0]
    mp = n // 2 + HALO
    z = z_ref[...]

    @pl.when(pl.program_id(0) == 0)
    def _():
        h = jnp.sin(f1_ref[...] * (_dot(z.astype(BF), w1_ref[...].astype(BF)) + b1_ref[...]))
        h = jnp.sin(f2_ref[...] * (_dot(h.astype(BF), w2_ref[...].astype(BF)) + b2_ref[...]))
        hid_ref[...] = h.astype(BF)

    hb16 = hid_ref[...]
    decay = jnp.exp(-z[:, 0:1] * dl_ref[...])
    h_f = _dot(hb16, w3f_ref[...].astype(BF)) * decay
    h_b = _dot(hb16, w3b_ref[...].astype(BF)) * decay
    row = lax.broadcasted_iota(jnp.int32, h_f.shape, 0)
    h_b = jnp.where(row == 0, 0.0, h_b)
    sm_even, sm_odd = _parity_split(h_f + h_b, split_ref)
    a_e = _dot(fe_ref[0:mp, :], sm_even.astype(BF))
    a_o = _dot(fo_ref[0:mp, :], sm_odd.astype(BF))
    df_even, df_odd = _parity_split(h_f - h_b, split_ref)
    b_e = _dot(fe_ref[mp:2 * mp, :], df_even.astype(BF))
    b_o = _dot(fo_ref[mp:2 * mp, :], df_odd.astype(BF))
    frow = lax.broadcasted_iota(jnp.int32, a_e.shape, 0)
    w_lo = jnp.where(frow == 0, 1.0, 2.0) * (0.5 / n)
    w_hi = jnp.where(frow == 0, 1.0, jnp.where(frow < n // 2, 2.0, 0.0)) * (0.5 / n)
    plo_ref[...] = (a_e + a_o) * w_lo
    qlo_ref[...] = -(b_e + b_o) * w_lo
    phi_ref[...] = (a_e - a_o) * w_hi
    qhi_ref[...] = (b_e - b_o) * w_hi


def _hy_filter(n, tabs, fw1, fb1, ff1, fw2, fb2, ff2, fw3, tc):
    (fwd_even, fwd_odd), _ = tabs
    w1 = jnp.pad(fw1, ((0, LANES - HY_EMB), (0, 0)))
    nt = D // tc
    m, mp = n // 2, n // 2 + HALO
    small = lambda s: pl.BlockSpec(s, lambda j: (0, 0))
    return pl.pallas_call(
        _hy_filter_kernel,
        grid=(nt,),
        in_specs=[small((n, LANES)), small((LANES, HY_FH)), small((1, HY_FH)), small((1, HY_FH)),
                  small((HY_FH, HY_FH)), small((1, HY_FH)), small((1, HY_FH)),
                  pl.BlockSpec((HY_FH, tc), lambda j: (0, j)),
                  pl.BlockSpec((HY_FH, tc), lambda j: (0, j + nt)),
                  pl.BlockSpec((1, tc), lambda j: (0, j)),
                  _resident((2 * mp, m)), _resident((2 * mp, m))],
        out_specs=[pl.BlockSpec((mp, tc), lambda j: (0, j))] * 4,
        out_shape=[jax.ShapeDtypeStruct((mp, D), F32)] * 4,
        scratch_shapes=[pltpu.VMEM((tc // LANES, n, LANES), F32), pltpu.VMEM((n, HY_FH), BF)],
        compiler_params=pltpu.CompilerParams(dimension_semantics=("arbitrary",),
                                             vmem_limit_bytes=VMEM_LIMIT),
        name="hy_filter",
    )(_hy_features(n), w1, fb1.reshape(1, -1), ff1.reshape(1, -1), fw2, fb2.reshape(1, -1),
      ff2.reshape(1, -1), fw3, fw3, _hy_deltas(), fwd_even, fwd_odd)


def _hy_conv_kernel(z_ref, x0_ref, plo_ref, qlo_ref, phi_ref, qhi_ref, ds_ref, fe_ref, fo_ref,
                    cet_ref, set_ref, cot_ref, sot_ref, o_ref, split_ref):
    n = z_ref.shape[0]
    m, mp = n // 2, n // 2 + HALO
    z = z_ref[...]
    z_even, z_odd = _parity_split(z, split_ref)
    r_e = _dot(fe_ref[...], z_even.astype(BF))
    r_o = _dot(fo_ref[...], z_odd.astype(BF))
    a_e, b_e, a_o, b_o = r_e[0:mp], r_e[mp:2 * mp], r_o[0:mp], r_o[mp:2 * mp]
    a_lo, b_lo, a_hi, b_hi = a_e + a_o, b_e + b_o, a_e - a_o, b_o - b_e
    p, q = plo_ref[...], qlo_ref[...]
    yr_lo, yi_lo = a_lo * p + b_lo * q, a_lo * q - b_lo * p
    p, q = phi_ref[...], qhi_ref[...]
    yr_hi, yi_hi = a_hi * p + b_hi * q, a_hi * q - b_hi * p
    y_even = (_dot(cet_ref[...], (yr_lo + yr_hi)[0:m].astype(BF))
              - _dot(set_ref[...], (yi_lo - yi_hi)[0:m].astype(BF)))
    y_odd = (_dot(cot_ref[...], (yr_lo - yr_hi)[0:m].astype(BF))
             - _dot(sot_ref[...], (yi_lo + yi_hi)[0:m].astype(BF)))
    half = lax.broadcasted_iota(jnp.int32, y_even.shape, 0)
    sign = (1 - 2 * (half % 2)).astype(F32)
    y_even = y_even + sign * yr_lo[m:m + 1, :]
    y_odd = y_odd - sign * yi_lo[m:m + 1, :]
    y = _parity_merge(y_even, y_odd, split_ref)
    o_ref[...] = (x0_ref[...] * (y + z * ds_ref[...])).astype(BF)


def _hy_conv(z, x0, spectra, d_skip, tabs, tc):
    b, n, _ = z.shape
    (fwd_even, fwd_odd), inv = tabs
    m, mp = n // 2, n // 2 + HALO
    sample = pl.BlockSpec((None, n, tc), lambda j, bi: (bi, 0, j))
    return pl.pallas_call(
        _hy_conv_kernel,
        grid=(D // tc, b),
        in_specs=[sample, sample] + [pl.BlockSpec((mp, tc), lambda j, bi: (0, j))] * 4
                 + [pl.BlockSpec((None, 1, tc), lambda j, bi: (0, 0, j)),
                    _resident((2 * mp, m)), _resident((2 * mp, m))] + [_resident((m, m))] * 4,
        out_specs=sample,
        out_shape=jax.ShapeDtypeStruct((b, n, D), BF),
        scratch_shapes=[pltpu.VMEM((tc // LANES, n, LANES), F32)],
        compiler_params=_params(2),
        name="hy_conv",
    )(z, x0, *spectra, d_skip, fwd_even, fwd_odd, *inv)


def _proj_rt_kernel(*refs, rope):
    if rope:
        x_ref, mod_ref, w_ref, cos_ref, sin_ref, q_ref, k_ref, v_ref, g_ref = refs
        cos, sin = cos_ref[...], sin_ref[...]
    else:
        x_ref, mod_ref, w_ref, k_ref, v_ref = refs
    h = _modulated(x_ref[...], mod_ref, 3)
    qk_w = RT_HEADS * RT_QK
    half = RT_QK // 2

    def store_rot(y, ref):
        for hd in range(RT_HEADS):
            lo = hd * RT_QK
            x1, x2 = y[:, lo:lo + half], y[:, lo + half:lo + RT_QK]
            if rope:
                x1, x2 = x1 * cos - x2 * sin, x1 * sin + x2 * cos
            ref[:, lo:lo + half] = x1.astype(BF)
            ref[:, lo + half:lo + RT_QK] = x2.astype(BF)

    if rope:
        store_rot(_dot(h, w_ref[:, 0:qk_w]), q_ref)
    store_rot(_dot(h, w_ref[:, qk_w:2 * qk_w]) * (RT_QK ** -0.5), k_ref)
    v_w = RT_HEADS * RT_V
    v_ref[...] = _dot(h, w_ref[:, 2 * qk_w:2 * qk_w + v_w]).astype(BF)
    if rope:
        g_ref[...] = _silu(_dot(h, w_ref[:, 2 * qk_w + v_w:2 * qk_w + 2 * v_w])).astype(BF)


def _proj_rt(x2, mods, layer, sel, w, rope_tabs, seq, tm):
    t = x2.shape[0]
    rpb = seq // tm
    rope = rope_tabs is not None
    qk_w, v_w = RT_HEADS * RT_QK, RT_HEADS * RT_V
    in_specs = [_rows(tm, D), _mod_spec(layer, sel), _layer_spec((D, 2 * qk_w + 2 * v_w), 0)]
    args = [x2, mods, w]
    if rope:
        in_specs += [pl.BlockSpec((tm, LANES), lambda i: (i % rpb, 0))] * 2
        args += list(rope_tabs)
        out_specs = [_rows(tm, qk_w), _rows(tm, qk_w), _rows(tm, v_w), _rows(tm, v_w)]
        out_shape = [jax.ShapeDtypeStruct((t, qk_w), BF), jax.ShapeDtypeStruct((t, qk_w), BF),
                     jax.ShapeDtypeStruct((t, v_w), BF), jax.ShapeDtypeStruct((t, v_w), BF)]
    else:
        out_specs = [_rows(tm, qk_w), _rows(tm, v_w)]
        out_shape = [jax.ShapeDtypeStruct((t, qk_w), BF), jax.ShapeDtypeStruct((t, v_w), BF)]
    return pl.pallas_call(
        functools.partial(_proj_rt_kernel, rope=rope),
        grid=(t // tm,),
        in_specs=in_specs,
        out_specs=out_specs,
        out_shape=out_shape,
        compiler_params=_params(1),
        name="proj_rt",
    )(*args)


def _ret_kernel(logit_ref, kc_ref, vc_ref, ql_ref, kl_ref, vl_ref, g_ref, gn_ref, o_ref,
                s_ref, acc_ref, *, chunk):
    hd = pl.program_id(1)
    ctx_chunk = min(chunk, kc_ref.shape[0])
    n_ctx = kc_ref.shape[0] // ctx_chunk
    n_lat = ql_ref.shape[0] // chunk
    row = lax.broadcasted_iota(jnp.int32, (chunk, chunk), 0)
    col = lax.broadcasted_iota(jnp.int32, (chunk, chunk), 1)

    def log_gamma(direction, shape):
        return jnp.log(jax.nn.sigmoid(jnp.full(shape, logit_ref[direction, hd], F32)))

    def decay(direction, size, power):
        pos = lax.broadcasted_iota(jnp.int32, (size, 1), 0).astype(F32)
        return jnp.exp(power(pos) * log_gamma(direction, (size, 1)))

    def write_decays(direction, size):
        d_write = decay(direction, size, (lambda p: size - 1.0 - p) if direction == 0 else (lambda p: p))
        return d_write, jnp.exp(size * log_gamma(direction, (1, 1)))

    intra, read = [], []
    for direction in (0, 1):
        lag = (row - col) if direction == 0 else (col - row)
        intra.append(jnp.where(lag >= 0, jnp.exp(jnp.maximum(lag, 0).astype(F32)
                                                 * log_gamma(direction, (chunk, 1))), 0.0))
        read.append(decay(direction, chunk, (lambda p: p + 1.0) if direction == 0 else (lambda p: chunk - p)))
    lat_write = [write_decays(direction, chunk) for direction in (0, 1)]
    ctx_write = [write_decays(direction, ctx_chunk) for direction in (0, 1)]

    s_ref[...] = jnp.zeros_like(s_ref)

    def absorb(direction, k_ref, v_ref, rows, decays):
        d_write, d_block = decays[direction]
        kw = (k_ref[rows, :].astype(F32) * d_write).astype(BF)
        s_ref[direction] = d_block * s_ref[direction] + lax.dot_general(
            kw, v_ref[rows, :], TN, preferred_element_type=F32)

    for direction in (0, 1):
        for c in (range(n_ctx) if direction == 0 else reversed(range(n_ctx))):
            absorb(direction, kc_ref, vc_ref, pl.ds(c * ctx_chunk, ctx_chunk), ctx_write)

    def chunk_rows(c):
        return pl.ds(pl.multiple_of(c * chunk, chunk), chunk)

    def scan_step(t):
        for direction in (0, 1):
            rows = chunk_rows(t if direction == 0 else n_lat - 1 - t)
            qc = ql_ref[rows, :]
            scores = (lax.dot_general(qc, kl_ref[rows, :], NT, preferred_element_type=F32)
                      * intra[direction])
            acc_ref[direction, rows, :] = (_dot(scores.astype(BF), vl_ref[rows, :])
                                           + _dot(qc, s_ref[direction].astype(BF)) * read[direction])
            absorb(direction, kl_ref, vl_ref, rows, lat_write)

    def finish(c):
        rows = chunk_rows(c)
        o = acc_ref[0, rows, :] + acc_ref[1, rows, :]
        mu = jnp.mean(o, -1, keepdims=True)
        dlt = o - mu
        var = jnp.mean(dlt * dlt, -1, keepdims=True)
        o_ref[rows, :] = (g_ref[rows, :].astype(F32)
                          * (dlt * lax.rsqrt(var + LN_EPS) * gn_ref[...])).astype(BF)

    def first_half(t, carry):
        scan_step(t)
        return carry

    def second_half(t, carry):
        scan_step(t)
        finish(t)
        finish(n_lat - 1 - t)
        return carry

    assert n_lat % 2 == 0
    lax.fori_loop(0, n_lat // 2, first_half, 0)
    lax.fori_loop(n_lat // 2, n_lat, second_half, 0)


def _retention(kc, vc, ql, kl, vl, g, decay_logit, gn_g):
    b, n, _ = ql.shape
    nc = kc.shape[1]
    qk = lambda m: pl.BlockSpec((None, m, RT_QK), lambda bi, h: (bi, 0, h))
    vv = lambda m: pl.BlockSpec((None, m, RT_V), lambda bi, h: (bi, 0, h))
    return pl.pallas_call(
        functools.partial(_ret_kernel, chunk=RT_CHUNK),
        grid=(b, RT_HEADS),
        in_specs=[pl.BlockSpec(memory_space=pltpu.SMEM),
                  qk(nc), vv(nc), qk(n), qk(n), vv(n), vv(n),
                  pl.BlockSpec((None, 1, RT_V), lambda bi, h: (0, 0, h))],
        out_specs=vv(n),
        out_shape=jax.ShapeDtypeStruct((b, n, RT_HEADS * RT_V), BF),
        scratch_shapes=[pltpu.VMEM((2, RT_QK, RT_V), F32), pltpu.VMEM((2, n, RT_V), F32)],
        compiler_params=_params(2),
        name="retention",
    )(decay_logit, kc, vc, ql, kl, vl, g, gn_g)


def _sc_ffn_kernel(xp_ref, x_ref, xn_ref, mod_ref, wi_ref, cw_ref, wo_ref, g1_ref, b1_ref,
                   fwi_ref, fwo_ref, g2_ref, b2_ref, o_ref, act_ref, *, tm, rpb):
    x = x_ref[...]
    xe = jnp.concatenate([xp_ref[...], x, xn_ref[...]], axis=0)
    h = _modulated(xe, mod_ref, 3)
    cu = _dot(h, wi_ref[:, D:2 * D]) * _dot(h, wi_ref[:, 2 * D:3 * D])
    b_gate = _dot(h[HALO:HALO + tm, :], wi_ref[:, 0:D])
    conv = _dwconv3_rows(_zero_outside(cu, tm, rpb), cw_ref[...])
    y = _dot((b_gate * conv).astype(BF), wo_ref[...])
    x1 = _layer_norm(ALPHA * x + mod_ref[5:6, :] * y, g1_ref[...], b1_ref[...])
    _ffn_slabs(lambda p: x1[p * FFN_SLAB:(p + 1) * FFN_SLAB, :], tm // FFN_SLAB, mod_ref, 6,
               fwi_ref, fwo_ref, g2_ref, b2_ref, o_ref, act_ref)


def _short_conv_ffn(x2, mods, layer, sel, wi, conv_w, wo, fwi, fwo, lng, lnb, seq, tm):
    t = x2.shape[0]
    return pl.pallas_call(
        functools.partial(_sc_ffn_kernel, tm=tm, rpb=seq // tm),
        grid=(t // tm,),
        in_specs=_halo_specs(tm, t) + [
            _mod_spec(layer, sel), _layer_spec((D, 3 * D), 0), _layer_spec((3, D), 0),
            _layer_spec((D, D), 0), _layer_spec((1, D), 3 * layer + 1),
            _layer_spec((1, D), 3 * layer + 1)] + _ffn_specs(layer, 1),
        out_specs=_rows(tm, D),
        out_shape=jax.ShapeDtypeStruct((t, D), F32),
        scratch_shapes=[pltpu.VMEM((tm, D_FF), BF)],
        compiler_params=_params(1),
        name="short_conv_ffn",
    )(x2, x2, x2, mods, wi, conv_w, wo, lng, lnb, fwi, fwo, lng, lnb)


def kernel(x, c, ctx, c_ctx, ada_w, ada_b, ln_g, ln_b, ffa_wi, ffa_wo, ffb_wi, ffb_wo, da_w_qkv, da_w_o, da_lambda, da_subln_g, hy_w_in, hy_conv_w, hy_conv_b, hy_fw1, hy_fb1, hy_ff1, hy_fw2, hy_fb2, hy_ff2, hy_fw3, hy_d_skip, hy_w_o, rt_w_in, rt_decay_logit, rt_gn_g, rt_w_o, sc_w_in, sc_conv_w, sc_w_o):
    bsz, seq, _ = x.shape
    n_ctx = ctx.shape[1]
    assert x.shape[2] == D and ada_w.shape[0] == DEPTH and seq % GRID_W == 0
    tm = min(512, seq)
    tmc = min(512, n_ctx)
    assert seq % tm == 0 and n_ctx % tmc == 0 and seq % RT_CHUNK == 0 and n_ctx % min(RT_CHUNK, n_ctx) == 0

    n_rows = -(-(bsz + 1) // HALO) * HALO
    cond = jnp.zeros((n_rows, D), F32).at[:bsz].set(c).at[bsz].set(c_ctx)
    mods = _modulation_all(cond, ada_w, ada_b)

    rpb = seq // tm
    lat = (lambda i: i // rpb, seq, tm)
    cx = (lambda i: bsz, n_ctx, tmc)

    xl = x.reshape(bsz * seq, D)
    xc = ctx.reshape(bsz * n_ctx, D)
    lng = ln_g.reshape(DEPTH * 3, 1, D)
    lnb = ln_b.reshape(DEPTH * 3, 1, D)
    ffw = {(0, 0): (ffa_wi[0].astype(BF), ffa_wo[0].astype(BF))}
    f32_stacks = {0: (ffa_wi, ffa_wo), 1: (ffb_wi, ffb_wo)}
    mixer_jobs = {(0, 0): [("da_qkv", da_w_qkv), ("da_o", da_w_o)],
                  (0, 1): [("hy_in", hy_w_in), ("hy_o", hy_w_o)],
                  (1, 1): [("rt_in", rt_w_in), ("rt_o", rt_w_o)],
                  (2, 1): [("sc_in", sc_w_in), ("sc_o", sc_w_o)]}
    mixw = {}

    def jobs_for(i, which, stream):
        if stream is not lat:
            return [], None
        nxt = (i, 1) if which == 0 else (i + 1, 0)
        jobs = [(s, nxt[0]) for s in f32_stacks[nxt[1]]] if nxt[0] < DEPTH else []
        return jobs + [(stack, 0) for _, stack in mixer_jobs.get((i, which), [])], nxt

    def keep_casts(i, which, nxt, casts):
        if nxt is None:
            return
        if nxt[0] < DEPTH:
            ffw[nxt], casts = (casts[0], casts[1]), casts[2:]
        for (name, _), w in zip(mixer_jobs.get((i, which), []), casts):
            mixw[name] = w[None]

    def row_tile(stream, want):
        rows = seq if stream is lat else bsz * n_ctx
        tile = min(want, rows)
        assert rows % tile == 0
        sel = (lambda j: j // (seq // tile)) if stream is lat else stream[0]
        return tile, sel

    def ffn_a(xx, i, stream):
        tile, sel = row_tile(stream, FFN_TILE)
        jobs, nxt = jobs_for(i, 0, stream)
        out, casts = _half_ffn(xx, mods, i, 0, sel, ffw[(i, 0)], lng, lnb, tile, jobs)
        keep_casts(i, 0, nxt, casts)
        return out

    def mix_out(a, w_o, xx, i, stream):
        tile, sel = row_tile(stream, tm)
        jobs, nxt = jobs_for(i, 1, stream)
        out, casts = _outproj_ffn(a, w_o, xx, mods, i, sel, ffw[(i, 1)], lng, lnb, tile, jobs)
        keep_casts(i, 1, nxt, casts)
        return out

    def by_sample(a, n):
        return a.reshape(bsz, n, a.shape[-1])

    i = 0
    xl, xc = ffn_a(xl, i, lat), ffn_a(xc, i, cx)
    w_qkv = mixw["da_qkv"]
    ql, kl, vtl = _proj_da(xl, mods, i, lat[0], w_qkv, _da_rope_tables(seq), seq, tm)
    qc, kc, vtc = _proj_da(xc, mods, i, cx[0], w_qkv, None, n_ctx, tmc)
    ql, kl, qc, kc = by_sample(ql, seq), by_sample(kl, seq), by_sample(qc, n_ctx), by_sample(kc, n_ctx)
    lam_init = 0.8 - 0.6 * math.exp(-0.3 * i)
    subln = da_subln_g.reshape(-1, 1, LANES)
    ol = _diff_attn(ql, [(kc, vtc), (kl, vtl)], da_lambda, subln, lam_init, min(ATTN_TILE, seq))
    oc = _diff_attn(qc, [(kc, vtc)], da_lambda, subln, lam_init, min(ATTN_TILE, n_ctx))
    w_o = mixw["da_o"]
    xl = mix_out(ol.reshape(bsz * seq, D), w_o, xl, i, lat)
    xc = mix_out(oc.reshape(bsz * n_ctx, D), w_o, xc, i, cx)

    i = 1
    xl, xc = ffn_a(xl, i, lat), ffn_a(xc, i, cx)
    w_in, w_o = mixw["hy_in"], mixw["hy_o"]
    conv_b = hy_conv_b.reshape(-1, 1, 3 * D)
    d_skip = hy_d_skip.reshape(-1, 1, D)
    tc = 256

    def hyena(xx, stream):
        sel, n, tile = stream
        tabs = _dft_tables(n)
        spectra = _hy_filter(n, tabs, hy_fw1[0], hy_fb1[0], hy_ff1[0], hy_fw2[0], hy_fb2[0],
                             hy_ff2[0], hy_fw3[0], tc)
        x0, z = _proj_hy(xx, mods, i, sel, w_in, hy_conv_w, conv_b, n, tile)
        y = _hy_conv(by_sample(z, n), by_sample(x0, n), spectra, d_skip, tabs, tc)
        return mix_out(y.reshape(bsz * n, D), w_o, xx, i, stream)

    xl, xc = hyena(xl, lat), hyena(xc, cx)

    i = 2
    xl, xc = ffn_a(xl, i, lat), ffn_a(xc, i, cx)
    w_in = mixw["rt_in"]
    ql, kl, vl, gl = [by_sample(a, seq) for a in
                      _proj_rt(xl, mods, i, lat[0], w_in, _rt_rope_tables(seq), seq, tm)]
    kc, vc = [by_sample(a, n_ctx) for a in _proj_rt(xc, mods, i, cx[0], w_in, None, n_ctx, tmc)]
    o = _retention(kc, vc, ql, kl, vl, gl, rt_decay_logit[0], rt_gn_g.reshape(-1, 1, RT_HEADS * RT_V))
    xl = mix_out(o.reshape(bsz * seq, RT_HEADS * RT_V), mixw["rt_o"], xl, i, lat)

    i = 3
    xl = ffn_a(xl, i, lat)
    xl = _short_conv_ffn(xl, mods, i, lat[0], mixw["sc_in"], sc_conv_w, mixw["sc_o"],
                         ffw[(i, 1)][0], ffw[(i, 1)][1], lng, lnb, seq, tm)
    return xl.reshape(bsz, seq, D)
```

```python
import functools
import math

import jax
import jax.numpy as jnp
import numpy as np
from jax import lax
from jax.experimental import pallas as pl
from jax.experimental.pallas import tpu as pltpu

D = 1024
N_MOD = 9
D_FF = 2816
LN_EPS = 1e-5
ROPE_BASE = 10000.0
GRID_W = 64
DEPTH = 4
ALPHA = (2.0 * DEPTH) ** 0.25
DA_HEADS = 8
DA_HEAD_DIM = 64
RT_HEADS = 4
RT_QK = 256
RT_V = 512
HY_EMB = 33
HY_FH = 64
LANES = 128
HALO = 8
FF_CHUNK = 256
RT_CHUNK = 256
FFN_TILE = 1024
FFN_SLAB = 256
RIDER_BLOCKS = 16
ATTN_TILE = 2048
ATTN_PART = 512
VMEM_LIMIT = 56 * 1024 * 1024
LOG2E = math.log2(math.e)

F32 = jnp.float32
BF = jnp.bfloat16
NT = (((1,), (1,)), ((), ()))
TN = (((0,), (0,)), ((), ()))


def _dot(a, b):
    return jnp.dot(a, b, preferred_element_type=F32)


def _resident(shape):
    nd = len(shape)
    return pl.BlockSpec(shape, lambda *_: (0,) * nd, pipeline_mode=pl.Buffered(1))


def _layer_spec(shape, layer):
    nd = len(shape)
    return pl.BlockSpec((None,) + tuple(shape), lambda *_: (layer,) + (0,) * nd,
                        pipeline_mode=pl.Buffered(1))


def _mod_spec(layer, sel):
    return pl.BlockSpec((None, None, N_MOD, D), lambda i: (layer, sel(i), 0, 0))


def _rows(tm, width):
    return pl.BlockSpec((tm, width), lambda i: (i, 0))


def _params(n_axes):
    return pltpu.CompilerParams(dimension_semantics=("parallel",) * n_axes,
                                vmem_limit_bytes=VMEM_LIMIT)


def _layer_norm(r, g, b):
    mu = jnp.mean(r, -1, keepdims=True)
    d = r - mu
    var = jnp.mean(d * d, -1, keepdims=True)
    return d * lax.rsqrt(var + LN_EPS) * g + b


def _silu(a):
    return a * jax.nn.sigmoid(a)


def _modulated(x, mod_ref, j):
    return (x * (1.0 + mod_ref[j + 1:j + 2, :]) + mod_ref[j:j + 1, :]).astype(BF)


def _mod_kernel(c_ref, w_ref, b_ref, o_ref):
    s = _silu(c_ref[...]).astype(BF)
    o_ref[...] = _dot(s, w_ref[...].astype(BF)) + b_ref[...]


def _modulation_all(cond, ada_w, ada_b):
    r = cond.shape[0]
    tn = 2304
    n = N_MOD * D
    out = pl.pallas_call(
        _mod_kernel,
        grid=(DEPTH, n // tn),
        in_specs=[pl.BlockSpec((r, D), lambda i, j: (0, 0)),
                  pl.BlockSpec((None, D, tn), lambda i, j: (i, 0, j)),
                  pl.BlockSpec((None, 1, tn), lambda i, j: (i, 0, j))],
        out_specs=pl.BlockSpec((None, r, tn), lambda i, j: (i, 0, j)),
        out_shape=jax.ShapeDtypeStruct((DEPTH, r, n), F32),
        compiler_params=_params(2),
        name="modulation",
    )(cond, ada_w, ada_b.reshape(DEPTH, 1, n))
    return out.reshape(DEPTH, r, N_MOD, D)


def _ffn_slabs(x_of, n_slabs, mod_ref, j0, wi_ref, wo_ref, g_ref, b_ref, o_ref, act_ref):
    rows = o_ref.shape[0] // n_slabs
    xs = {}

    def up(p):
        xs[p] = x_of(p)
        h = _modulated(xs[p], mod_ref, j0)
        for c in range(D_FF // FF_CHUNK):
            lo = c * FF_CHUNK
            a = _dot(h, wi_ref[:, lo:lo + FF_CHUNK])
            u = _dot(h, wi_ref[:, D_FF + lo:D_FF + lo + FF_CHUNK])
            act_ref[p * rows:(p + 1) * rows, lo:lo + FF_CHUNK] = (_silu(a) * u).astype(BF)

    def down(p):
        y = _dot(act_ref[p * rows:(p + 1) * rows, :], wo_ref[...])
        r = ALPHA * xs.pop(p) + (0.5 * mod_ref[j0 + 2:j0 + 3, :]) * y
        o_ref[p * rows:(p + 1) * rows, :] = _layer_norm(r, g_ref[...], b_ref[...])

    for p in range(n_slabs):
        up(p)
        if p > 0:
            down(p - 1)
    down(n_slabs - 1)


def _ffn_specs(layer, which):
    return [_resident((D, 2 * D_FF)), _resident((D_FF, D)),
            _layer_spec((1, D), 3 * layer + 2 * which), _layer_spec((1, D), 3 * layer + 2 * which)]


def _rider_specs(jobs, steps):
    blocks = min(RIDER_BLOCKS, steps)
    reps = steps // blocks
    assert steps % blocks == 0
    in_specs, out_specs, out_shape = [], [], []
    for stack, layer in jobs:
        _, r, c = stack.shape
        in_specs.append(pl.BlockSpec((None, r // blocks, c), lambda i, layer=layer: (layer, i // reps, 0)))
        out_specs.append(pl.BlockSpec((r // blocks, c), lambda i: (i // reps, 0)))
        out_shape.append(jax.ShapeDtypeStruct((r, c), BF))
    return in_specs, out_specs, out_shape


def _split_rider(rest, n_jobs):
    o_ref, act_ref = rest[n_jobs], rest[-1]
    for src_ref, dst_ref in zip(rest[:n_jobs], rest[n_jobs + 1:-1]):
        dst_ref[...] = src_ref[...].astype(BF)
    return o_ref, act_ref


def _ffn_kernel(x_ref, mod_ref, wi_ref, wo_ref, g_ref, b_ref, *rest, j0, n_slabs, n_jobs):
    o_ref, act_ref = _split_rider(rest, n_jobs)
    rows = x_ref.shape[0] // n_slabs
    _ffn_slabs(lambda p: x_ref[p * rows:(p + 1) * rows, :], n_slabs, mod_ref, j0, wi_ref, wo_ref,
               g_ref, b_ref, o_ref, act_ref)


def _call_with_rider(kernel_fn, name, t, tm, in_specs, args, jobs):
    r_in, r_out, r_shape = _rider_specs(jobs, t // tm)
    outs = pl.pallas_call(
        functools.partial(kernel_fn, n_jobs=len(jobs)),
        grid=(t // tm,),
        in_specs=in_specs + r_in,
        out_specs=[_rows(tm, D)] + r_out,
        out_shape=[jax.ShapeDtypeStruct((t, D), F32)] + r_shape,
        scratch_shapes=[pltpu.VMEM((tm, D_FF), BF)],
        compiler_params=_params(1),
        name=name,
    )(*args, *[stack for stack, _ in jobs])
    return outs[0], list(outs[1:])


def _half_ffn(x2, mods, layer, which, sel, weights, lng, lnb, tm, rider):
    return _call_with_rider(
        functools.partial(_ffn_kernel, j0=6 * which, n_slabs=tm // FFN_SLAB), "half_ffn", x2.shape[0], tm,
        [_rows(tm, D), _mod_spec(layer, sel)] + _ffn_specs(layer, which),
        [x2, mods, weights[0], weights[1], lng, lnb], rider)


def _outproj_ffn_kernel(a_ref, w_ref, x_ref, mod_ref, g1_ref, b1_ref, wi_ref, wo_ref, g2_ref, b2_ref,
                        *rest, n_jobs):
    o_ref, act_ref = _split_rider(rest, n_jobs)
    n_slabs = x_ref.shape[0] // FFN_SLAB

    def mixed(p):
        rows = slice(p * FFN_SLAB, (p + 1) * FFN_SLAB)
        y = _dot(a_ref[rows, :], w_ref[...])
        return _layer_norm(ALPHA * x_ref[rows, :] + mod_ref[5:6, :] * y, g1_ref[...], b1_ref[...])

    _ffn_slabs(mixed, n_slabs, mod_ref, 6, wi_ref, wo_ref, g2_ref, b2_ref, o_ref, act_ref)


def _outproj_ffn(a2, w, x2, mods, layer, sel, weights, lng, lnb, tm, rider):
    t, k = a2.shape
    return _call_with_rider(
        _outproj_ffn_kernel, "outproj_ffn", t, tm,
        [_rows(tm, k), _layer_spec((k, D), 0), _rows(tm, D), _mod_spec(layer, sel),
         _layer_spec((1, D), 3 * layer + 1), _layer_spec((1, D), 3 * layer + 1)] + _ffn_specs(layer, 1),
        [a2, w, x2, mods, lng, lnb, weights[0], weights[1], lng, lnb], rider)


def _rope_angles(n_tokens, dim):
    rows = n_tokens // GRID_W
    row = np.repeat(np.arange(rows), GRID_W).astype(np.float32)
    col = np.tile(np.arange(GRID_W), rows).astype(np.float32)
    n_freq = dim // 4
    inv = (ROPE_BASE ** (-np.arange(n_freq, dtype=np.float32) / n_freq)).astype(np.float32)
    return np.concatenate([row[:, None] * inv, col[:, None] * inv], axis=-1)


def _da_rope_tables(n_tokens):
    ang = _rope_angles(n_tokens, DA_HEAD_DIM)
    cos, sin = np.cos(ang), np.sin(ang)
    cos128 = np.tile(cos, (1, 4))
    sin128 = np.tile(np.concatenate([-sin, sin], axis=-1), (1, 2))
    return jnp.asarray(cos128, F32), jnp.asarray(sin128, F32)


def _rt_rope_tables(n_tokens):
    ang = _rope_angles(n_tokens, RT_QK)
    return jnp.asarray(np.cos(ang), F32), jnp.asarray(np.sin(ang), F32)


def _proj_da_kernel(*refs, rope):
    if rope:
        x_ref, mod_ref, w_ref, cos_ref, sin_ref, q_ref, k_ref, v_ref = refs
        cos, sin = cos_ref[...], sin_ref[...]
        lane = lax.broadcasted_iota(jnp.int32, (1, LANES), 1)
        first_half = (lane % DA_HEAD_DIM) < (DA_HEAD_DIM // 2)
    else:
        x_ref, mod_ref, w_ref, q_ref, k_ref, v_ref = refs
    h = _modulated(x_ref[...], mod_ref, 3)

    def rot(y):
        if not rope:
            return y
        partner = jnp.where(first_half, pltpu.roll(y, LANES - 32, 1), pltpu.roll(y, 32, 1))
        return y * cos + partner * sin

    q = _dot(h, w_ref[:, 0:D]) * (DA_HEAD_DIM ** -0.5 * LOG2E)
    for j in range(D // LANES):
        q_ref[j * LANES:(j + 1) * LANES, :] = rot(q[:, j * LANES:(j + 1) * LANES]).T.astype(BF)
    k = _dot(h, w_ref[:, D:2 * D])
    for j in range(D // LANES):
        k_ref[:, j * LANES:(j + 1) * LANES] = rot(k[:, j * LANES:(j + 1) * LANES]).astype(BF)
    v_ref[...] = _dot(h, w_ref[:, 2 * D:3 * D]).T.astype(BF)


def _proj_da(x2, mods, layer, sel, w, rope_tabs, seq, tm):
    t = x2.shape[0]
    rpb = seq // tm
    rope = rope_tabs is not None
    in_specs = [_rows(tm, D), _mod_spec(layer, sel), _layer_spec((D, 3 * D), 0)]
    args = [x2, mods, w]
    if rope:
        in_specs += [pl.BlockSpec((tm, LANES), lambda i: (i % rpb, 0))] * 2
        args += list(rope_tabs)
    return pl.pallas_call(
        functools.partial(_proj_da_kernel, rope=rope),
        grid=(t // tm,),
        in_specs=in_specs,
        out_specs=[pl.BlockSpec((None, D, tm), lambda i: (i // rpb, 0, i % rpb)), _rows(tm, D),
                   pl.BlockSpec((None, D, tm), lambda i: (i // rpb, 0, i % rpb))],
        out_shape=[jax.ShapeDtypeStruct((t // seq, D, seq), BF), jax.ShapeDtypeStruct((t, D), BF),
                   jax.ShapeDtypeStruct((t // seq, D, seq), BF)],
        compiler_params=_params(1),
        name="proj_da",
    )(*args)


def _attn_kernel(*refs, lam_init, n_kv, parts):
    lam_ref, g_ref, q_ref = refs[:3]
    kv_refs = [(refs[3 + 2 * j], refs[4 + 2 * j]) for j in range(n_kv)]
    o_ref = refs[3 + 2 * n_kv]
    width = q_ref.shape[1] // parts
    feature = lax.broadcasted_iota(jnp.int32, (LANES, width), 0)

    def scores(part, head_map):
        q = q_ref[:, part * width:(part + 1) * width]
        keep = (feature < DA_HEAD_DIM) if head_map == 0 else (feature >= DA_HEAD_DIM)
        qm = jnp.where(keep, q, jnp.zeros_like(q))
        return [_dot(k_ref[...], qm) for k_ref, _ in kv_refs]

    def attend(ss):
        m = functools.reduce(jnp.maximum, [jnp.max(s, 0, keepdims=True) for s in ss])
        es = [jnp.exp2(s - m) for s in ss]
        den = functools.reduce(jnp.add, [jnp.sum(e, 0, keepdims=True) for e in es])
        o = functools.reduce(jnp.add, [_dot(vt_ref[...], e.astype(BF))
                                       for e, (_, vt_ref) in zip(es, kv_refs)])
        return o / den

    chains = [(part, head_map) for part in range(parts) for head_map in range(2)]
    result, pending = {}, None
    for chain in chains:
        ss = scores(*chain)
        if pending is not None:
            result[pending[0]] = attend(pending[1])
        pending = (chain, ss)
    result[pending[0]] = attend(pending[1])

    lam = lam_ref[...]
    lam_full = (jnp.exp(jnp.sum(lam[0:1] * lam[1:2], -1, keepdims=True))
                - jnp.exp(jnp.sum(lam[2:3] * lam[3:4], -1, keepdims=True)) + lam_init)
    for part in range(parts):
        o = result[(part, 0)] - lam_full * result[(part, 1)]
        o = o * lax.rsqrt(jnp.mean(o * o, 0, keepdims=True) + LN_EPS)
        o_ref[part * width:(part + 1) * width, :] = (o.T * g_ref[...] * (1.0 - lam_init)).astype(BF)


def _diff_attn(q, kvs, lam, subln_g, lam_init, tq):
    b, _, n = q.shape
    in_specs = [_layer_spec((4, DA_HEAD_DIM), 0), _layer_spec((1, LANES), 0),
                pl.BlockSpec((None, LANES, tq), lambda bi, h, i: (bi, h, i))]
    args = [lam, subln_g, q]
    for k, vt in kvs:
        n_k = k.shape[1]
        in_specs += [pl.BlockSpec((None, n_k, LANES), lambda bi, h, i: (bi, 0, h)),
                     pl.BlockSpec((None, LANES, n_k), lambda bi, h, i: (bi, h, 0))]
        args += [k, vt]
    return pl.pallas_call(
        functools.partial(_attn_kernel, lam_init=lam_init, n_kv=len(kvs),
                          parts=max(1, tq // ATTN_PART)),
        grid=(b, DA_HEADS, n // tq),
        in_specs=in_specs,
        out_specs=pl.BlockSpec((None, tq, LANES), lambda bi, h, i: (bi, i, h)),
        out_shape=jax.ShapeDtypeStruct((b, n, D), BF),
        compiler_params=_params(3),
        name="diff_attn",
    )(*args)


def _halo_specs(tm, t):
    blocks_per_tile = tm // HALO
    last = t // HALO - 1
    return [pl.BlockSpec((HALO, D), lambda i: (jnp.maximum(i * blocks_per_tile - 1, 0), 0)),
            _rows(tm, D),
            pl.BlockSpec((HALO, D), lambda i: (jnp.minimum((i + 1) * blocks_per_tile, last), 0))]


def _zero_outside(u, tm, rpb):
    pos = pl.program_id(0) % rpb
    keep_prev = jnp.where(pos == 0, 0.0, 1.0)
    keep_next = jnp.where(pos == rpb - 1, 0.0, 1.0)
    return jnp.concatenate([u[:HALO] * keep_prev, u[HALO:HALO + tm], u[HALO + tm:] * keep_next], axis=0)


def _dwconv3_rows(u, w):
    n = u.shape[0]
    y = pltpu.roll(u, 1, 0) * w[0:1, :] + u * w[1:2, :] + pltpu.roll(u, n - 1, 0) * w[2:3, :]
    return y[HALO:n - HALO, :]


def _proj_hy_kernel(xp_ref, x_ref, xn_ref, mod_ref, w_ref, cw_ref, cb_ref, x0_ref, z_ref, *, tm, rpb):
    xe = jnp.concatenate([xp_ref[...], x_ref[...], xn_ref[...]], axis=0)
    h = _modulated(xe, mod_ref, 3)

    def project(j):
        return _dot(h, w_ref[:, j * D:(j + 1) * D])

    def conv(u, j):
        u = _zero_outside(u, tm, rpb)
        return _dwconv3_rows(u, cw_ref[:, j * D:(j + 1) * D]) + cb_ref[:, j * D:(j + 1) * D]

    u0 = project(0)
    u1 = project(1)
    x0_ref[...] = conv(u0, 0)
    u2 = project(2)
    x1 = conv(u1, 1)
    z_ref[...] = x1 * conv(u2, 2)


def _proj_hy(x2, mods, layer, sel, w, conv_w, conv_b, seq, tm):
    t = x2.shape[0]
    return pl.pallas_call(
        functools.partial(_proj_hy_kernel, tm=tm, rpb=seq // tm),
        grid=(t // tm,),
        in_specs=_halo_specs(tm, t) + [
            _mod_spec(layer, sel), _layer_spec((D, 3 * D), 0), _layer_spec((3, 3 * D), 0),
            _layer_spec((1, 3 * D), 0)],
        out_specs=[_rows(tm, D)] * 2,
        out_shape=[jax.ShapeDtypeStruct((t, D), F32)] * 2,
        compiler_params=_params(1),
        name="proj_hy",
    )(x2, x2, x2, mods, w, conv_w, conv_b)


def _dft_tables(n):
    m = n // 2
    k = np.arange(m + HALO, dtype=np.int64)[:, None]
    valid = k <= m
    s_even = 2 * np.arange(m, dtype=np.int64)[None, :]

    def table(fn, offset):
        phase = (k * (s_even + offset)) % (2 * n)
        return np.where(valid, fn(phase.astype(np.float64) * (math.pi / n)), 0.0).astype(np.float32)

    ce, se, co, so = table(np.cos, 0), table(np.sin, 0), table(np.cos, 1), table(np.sin, 1)
    as_bf = lambda a: jnp.asarray(np.ascontiguousarray(a)).astype(BF)
    fwd = (as_bf(np.concatenate([ce, se], 0)), as_bf(np.concatenate([co, so], 0)))
    inv = tuple(as_bf(t[:m].T) for t in (ce, se, co, so))
    return fwd, inv


def _parity_split(v, split_ref):
    n, width = v.shape
    slabs = width // LANES
    for c in range(slabs):
        split_ref[c] = v[:, c * LANES:(c + 1) * LANES]
    take = lambda start: jnp.concatenate(
        [split_ref[c, pl.ds(start, n // 2, stride=2), :] for c in range(slabs)], axis=1)
    return take(0), take(1)


def _parity_merge(even, odd, split_ref):
    m, width = even.shape
    slabs = width // LANES
    for c in range(slabs):
        split_ref[c, pl.ds(0, m, stride=2), :] = even[:, c * LANES:(c + 1) * LANES]
        split_ref[c, pl.ds(1, m, stride=2), :] = odd[:, c * LANES:(c + 1) * LANES]
    return jnp.concatenate([split_ref[c] for c in range(slabs)], axis=1)


def _hy_features(n):
    t = np.linspace(0.0, 1.0, n, dtype=np.float32)[:, None]
    bands = (HY_EMB - 1) // 2
    fr = np.linspace(1e-4, bands - 1, bands, dtype=np.float32)[None, :]
    w = (2.0 * math.pi * np.arange(n, dtype=np.float32)[:, None] / n).astype(np.float32)
    z = np.concatenate([t, np.cos(fr * w), -np.sin(fr * w)], axis=-1).astype(np.float32)
    return jnp.asarray(np.pad(z, ((0, 0), (0, LANES - HY_EMB))), F32)


def _hy_deltas():
    max_decay = math.log(1e-2) / 0.3
    min_decay = math.log(1e-2) / 1.5
    return jnp.asarray(np.abs(np.linspace(min_decay, max_decay, D, dtype=np.float32))[None, :], F32)


def _hy_filter_kernel(z_ref, w1_ref, b1_ref, f1_ref, w2_ref, b2_ref, f2_ref, w3f_ref, w3b_ref,
                      dl_ref, fe_ref, fo_ref, plo_ref, qlo_ref, phi_ref, qhi_ref, split_ref, hid_ref):
    n = z_ref.shape[0]
    mp = n // 2 + HALO
    z = z_ref[...]

    @pl.when(pl.program_id(0) == 0)
    def _():
        h = jnp.sin(f1_ref[...] * (_dot(z.astype(BF), w1_ref[...].astype(BF)) + b1_ref[...]))
        h = jnp.sin(f2_ref[...] * (_dot(h.astype(BF), w2_ref[...].astype(BF)) + b2_ref[...]))
        hid_ref[...] = h.astype(BF)

    hb16 = hid_ref[...]
    decay = jnp.exp(-z[:, 0:1] * dl_ref[...])
    h_f = _dot(hb16, w3f_ref[...].astype(BF)) * decay
    h_b = _dot(hb16, w3b_ref[...].astype(BF)) * decay
    row = lax.broadcasted_iota(jnp.int32, h_f.shape, 0)
    h_b = jnp.where(row == 0, 0.0, h_b)
    sm_even, sm_odd = _parity_split(h_f + h_b, split_ref)
    a_e = _dot(fe_ref[0:mp, :], sm_even.astype(BF))
    a_o = _dot(fo_ref[0:mp, :], sm_odd.astype(BF))
    df_even, df_odd = _parity_split(h_f - h_b, split_ref)
    b_e = _dot(fe_ref[mp:2 * mp, :], df_even.astype(BF))
    b_o = _dot(fo_ref[mp:2 * mp, :], df_odd.astype(BF))
    frow = lax.broadcasted_iota(jnp.int32, a_e.shape, 0)
    w_lo = jnp.where(frow == 0, 1.0, 2.0) * (0.5 / n)
    w_hi = jnp.where(frow == 0, 1.0, jnp.where(frow < n // 2, 2.0, 0.0)) * (0.5 / n)
    plo_ref[...] = (a_e + a_o) * w_lo
    qlo_ref[...] = -(b_e + b_o) * w_lo
    phi_ref[...] = (a_e - a_o) * w_hi
    qhi_ref[...] = (b_e - b_o) * w_hi


def _hy_filter(n, tabs, fw1, fb1, ff1, fw2, fb2, ff2, fw3, tc):
    (fwd_even, fwd_odd), _ = tabs
    w1 = jnp.pad(fw1, ((0, LANES - HY_EMB), (0, 0)))
    nt = D // tc
    m, mp = n // 2, n // 2 + HALO
    small = lambda s: pl.BlockSpec(s, lambda j: (0, 0))
    return pl.pallas_call(
        _hy_filter_kernel,
        grid=(nt,),
        in_specs=[small((n, LANES)), small((LANES, HY_FH)), small((1, HY_FH)), small((1, HY_FH)),
                  small((HY_FH, HY_FH)), small((1, HY_FH)), small((1, HY_FH)),
                  pl.BlockSpec((HY_FH, tc), lambda j: (0, j)),
                  pl.BlockSpec((HY_FH, tc), lambda j: (0, j + nt)),
                  pl.BlockSpec((1, tc), lambda j: (0, j)),
                  _resident((2 * mp, m)), _resident((2 * mp, m))],
        out_specs=[pl.BlockSpec((mp, tc), lambda j: (0, j))] * 4,
        out_shape=[jax.ShapeDtypeStruct((mp, D), F32)] * 4,
        scratch_shapes=[pltpu.VMEM((tc // LANES, n, LANES), F32), pltpu.VMEM((n, HY_FH), BF)],
        compiler_params=pltpu.CompilerParams(dimension_semantics=("arbitrary",),
                                             vmem_limit_bytes=VMEM_LIMIT),
        name="hy_filter",
    )(_hy_features(n), w1, fb1.reshape(1, -1), ff1.reshape(1, -1), fw2, fb2.reshape(1, -1),
      ff2.reshape(1, -1), fw3, fw3, _hy_deltas(), fwd_even, fwd_odd)


def _hy_conv_kernel(z_ref, x0_ref, plo_ref, qlo_ref, phi_ref, qhi_ref, ds_ref, fe_ref, fo_ref,
                    cet_ref, set_ref, cot_ref, sot_ref, o_ref, split_ref):
    n = z_ref.shape[0]
    m, mp = n // 2, n // 2 + HALO
    z = z_ref[...]
    z_even, z_odd = _parity_split(z, split_ref)
    r_e = _dot(fe_ref[...], z_even.astype(BF))
    r_o = _dot(fo_ref[...], z_odd.astype(BF))
    a_e, b_e, a_o, b_o = r_e[0:mp], r_e[mp:2 * mp], r_o[0:mp], r_o[mp:2 * mp]
    a_lo, b_lo, a_hi, b_hi = a_e + a_o, b_e + b_o, a_e - a_o, b_o - b_e
    p, q = plo_ref[...], qlo_ref[...]
    yr_lo, yi_lo = a_lo * p + b_lo * q, a_lo * q - b_lo * p
    p, q = phi_ref[...], qhi_ref[...]
    yr_hi, yi_hi = a_hi * p + b_hi * q, a_hi * q - b_hi * p
    y_even = (_dot(cet_ref[...], (yr_lo + yr_hi)[0:m].astype(BF))
              - _dot(set_ref[...], (yi_lo - yi_hi)[0:m].astype(BF)))
    y_odd = (_dot(cot_ref[...], (yr_lo - yr_hi)[0:m].astype(BF))
             - _dot(sot_ref[...], (yi_lo + yi_hi)[0:m].astype(BF)))
    half = lax.broadcasted_iota(jnp.int32, y_even.shape, 0)
    sign = (1 - 2 * (half % 2)).astype(F32)
    y_even = y_even + sign * yr_lo[m:m + 1, :]
    y_odd = y_odd - sign * yi_lo[m:m + 1, :]
    y = _parity_merge(y_even, y_odd, split_ref)
    o_ref[...] = (x0_ref[...] * (y + z * ds_ref[...])).astype(BF)


def _hy_conv(z, x0, spectra, d_skip, tabs, tc):
    b, n, _ = z.shape
    (fwd_even, fwd_odd), inv = tabs
    m, mp = n // 2, n // 2 + HALO
    sample = pl.BlockSpec((None, n, tc), lambda j, bi: (bi, 0, j))
    return pl.pallas_call(
        _hy_conv_kernel,
        grid=(D // tc, b),
        in_specs=[sample, sample] + [pl.BlockSpec((mp, tc), lambda j, bi: (0, j))] * 4
                 + [pl.BlockSpec((None, 1, tc), lambda j, bi: (0, 0, j)),
                    _resident((2 * mp, m)), _resident((2 * mp, m))] + [_resident((m, m))] * 4,
        out_specs=sample,
        out_shape=jax.ShapeDtypeStruct((b, n, D), BF),
        scratch_shapes=[pltpu.VMEM((tc // LANES, n, LANES), F32)],
        compiler_params=_params(2),
        name="hy_conv",
    )(z, x0, *spectra, d_skip, fwd_even, fwd_odd, *inv)


def _proj_rt_kernel(*refs, rope):
    if rope:
        x_ref, mod_ref, w_ref, cos_ref, sin_ref, q_ref, k_ref, v_ref, g_ref = refs
        cos, sin = cos_ref[...], sin_ref[...]
    else:
        x_ref, mod_ref, w_ref, k_ref, v_ref = refs
    h = _modulated(x_ref[...], mod_ref, 3)
    qk_w = RT_HEADS * RT_QK
    half = RT_QK // 2

    def store_rot(y, ref):
        for hd in range(RT_HEADS):
            lo = hd * RT_QK
            x1, x2 = y[:, lo:lo + half], y[:, lo + half:lo + RT_QK]
            if rope:
                x1, x2 = x1 * cos - x2 * sin, x1 * sin + x2 * cos
            ref[:, lo:lo + half] = x1.astype(BF)
            ref[:, lo + half:lo + RT_QK] = x2.astype(BF)

    if rope:
        store_rot(_dot(h, w_ref[:, 0:qk_w]), q_ref)
    store_rot(_dot(h, w_ref[:, qk_w:2 * qk_w]) * (RT_QK ** -0.5), k_ref)
    v_w = RT_HEADS * RT_V
    v_ref[...] = _dot(h, w_ref[:, 2 * qk_w:2 * qk_w + v_w]).astype(BF)
    if rope:
        g_ref[...] = _silu(_dot(h, w_ref[:, 2 * qk_w + v_w:2 * qk_w + 2 * v_w])).astype(BF)


def _proj_rt(x2, mods, layer, sel, w, rope_tabs, seq, tm):
    t = x2.shape[0]
    rpb = seq // tm
    rope = rope_tabs is not None
    qk_w, v_w = RT_HEADS * RT_QK, RT_HEADS * RT_V
    in_specs = [_rows(tm, D), _mod_spec(layer, sel), _layer_spec((D, 2 * qk_w + 2 * v_w), 0)]
    args = [x2, mods, w]
    if rope:
        in_specs += [pl.BlockSpec((tm, LANES), lambda i: (i % rpb, 0))] * 2
        args += list(rope_tabs)
        out_specs = [_rows(tm, qk_w), _rows(tm, qk_w), _rows(tm, v_w), _rows(tm, v_w)]
        out_shape = [jax.ShapeDtypeStruct((t, qk_w), BF), jax.ShapeDtypeStruct((t, qk_w), BF),
                     jax.ShapeDtypeStruct((t, v_w), BF), jax.ShapeDtypeStruct((t, v_w), BF)]
    else:
        out_specs = [_rows(tm, qk_w), _rows(tm, v_w)]
        out_shape = [jax.ShapeDtypeStruct((t, qk_w), BF), jax.ShapeDtypeStruct((t, v_w), BF)]
    return pl.pallas_call(
        functools.partial(_proj_rt_kernel, rope=rope),
        grid=(t // tm,),
        in_specs=in_specs,
        out_specs=out_specs,
        out_shape=out_shape,
        compiler_params=_params(1),
        name="proj_rt",
    )(*args)


def _ret_kernel(logit_ref, kc_ref, vc_ref, ql_ref, kl_ref, vl_ref, g_ref, gn_ref, o_ref,
                s_ref, acc_ref, *, chunk):
    hd = pl.program_id(1)
    ctx_chunk = min(chunk, kc_ref.shape[0])
    n_ctx = kc_ref.shape[0] // ctx_chunk
    n_lat = ql_ref.shape[0] // chunk
    row = lax.broadcasted_iota(jnp.int32, (chunk, chunk), 0)
    col = lax.broadcasted_iota(jnp.int32, (chunk, chunk), 1)

    def log_gamma(direction, shape):
        return jnp.log(jax.nn.sigmoid(jnp.full(shape, logit_ref[direction, hd], F32)))

    def decay(direction, size, power):
        pos = lax.broadcasted_iota(jnp.int32, (size, 1), 0).astype(F32)
        return jnp.exp(power(pos) * log_gamma(direction, (size, 1)))

    def write_decays(direction, size):
        d_write = decay(direction, size, (lambda p: size - 1.0 - p) if direction == 0 else (lambda p: p))
        return d_write, jnp.exp(size * log_gamma(direction, (1, 1)))

    intra, read = [], []
    for direction in (0, 1):
        lag = (row - col) if direction == 0 else (col - row)
        intra.append(jnp.where(lag >= 0, jnp.exp(jnp.maximum(lag, 0).astype(F32)
                                                 * log_gamma(direction, (chunk, 1))), 0.0))
        read.append(decay(direction, chunk, (lambda p: p + 1.0) if direction == 0 else (lambda p: chunk - p)))
    lat_write = [write_decays(direction, chunk) for direction in (0, 1)]
    ctx_write = [write_decays(direction, ctx_chunk) for direction in (0, 1)]

    s_ref[...] = jnp.zeros_like(s_ref)

    def absorb(direction, k_ref, v_ref, rows, decays):
        d_write, d_block = decays[direction]
        kw = (k_ref[rows, :].astype(F32) * d_write).astype(BF)
        s_ref[direction] = d_block * s_ref[direction] + lax.dot_general(
            kw, v_ref[rows, :], TN, preferred_element_type=F32)

    for direction in (0, 1):
        for c in (range(n_ctx) if direction == 0 else reversed(range(n_ctx))):
            absorb(direction, kc_ref, vc_ref, pl.ds(c * ctx_chunk, ctx_chunk), ctx_write)

    def chunk_rows(c):
        return pl.ds(pl.multiple_of(c * chunk, chunk), chunk)

    def scan_step(t):
        for direction in (0, 1):
            rows = chunk_rows(t if direction == 0 else n_lat - 1 - t)
            qc = ql_ref[rows, :]
            scores = (lax.dot_general(qc, kl_ref[rows, :], NT, preferred_element_type=F32)
                      * intra[direction])
            acc_ref[direction, rows, :] = (_dot(scores.astype(BF), vl_ref[rows, :])
                                           + _dot(qc, s_ref[direction].astype(BF)) * read[direction])
            absorb(direction, kl_ref, vl_ref, rows, lat_write)

    def finish(c):
        rows = chunk_rows(c)
        o = acc_ref[0, rows, :] + acc_ref[1, rows, :]
        mu = jnp.mean(o, -1, keepdims=True)
        dlt = o - mu
        var = jnp.mean(dlt * dlt, -1, keepdims=True)
        o_ref[rows, :] = (g_ref[rows, :].astype(F32)
                          * (dlt * lax.rsqrt(var + LN_EPS) * gn_ref[...])).astype(BF)

    def first_half(t, carry):
        scan_step(t)
        return carry

    def second_half(t, carry):
        scan_step(t)
        finish(t)
        finish(n_lat - 1 - t)
        return carry

    assert n_lat % 2 == 0
    lax.fori_loop(0, n_lat // 2, first_half, 0)
    lax.fori_loop(n_lat // 2, n_lat, second_half, 0)


def _retention(kc, vc, ql, kl, vl, g, decay_logit, gn_g):
    b, n, _ = ql.shape
    nc = kc.shape[1]
    qk = lambda m: pl.BlockSpec((None, m, RT_QK), lambda bi, h: (bi, 0, h))
    vv = lambda m: pl.BlockSpec((None, m, RT_V), lambda bi, h: (bi, 0, h))
    return pl.pallas_call(
        functools.partial(_ret_kernel, chunk=RT_CHUNK),
        grid=(b, RT_HEADS),
        in_specs=[pl.BlockSpec(memory_space=pltpu.SMEM),
                  qk(nc), vv(nc), qk(n), qk(n), vv(n), vv(n),
                  pl.BlockSpec((None, 1, RT_V), lambda bi, h: (0, 0, h))],
        out_specs=vv(n),
        out_shape=jax.ShapeDtypeStruct((b, n, RT_HEADS * RT_V), BF),
        scratch_shapes=[pltpu.VMEM((2, RT_QK, RT_V), F32), pltpu.VMEM((2, n, RT_V), F32)],
        compiler_params=_params(2),
        name="retention",
    )(decay_logit, kc, vc, ql, kl, vl, g, gn_g)


def _sc_ffn_kernel(xp_ref, x_ref, xn_ref, mod_ref, wi_ref, cw_ref, wo_ref, g1_ref, b1_ref,
                   fwi_ref, fwo_ref, g2_ref, b2_ref, o_ref, act_ref, *, tm, rpb):
    x = x_ref[...]
    xe = jnp.concatenate([xp_ref[...], x, xn_ref[...]], axis=0)
    h = _modulated(xe, mod_ref, 3)
    cu = _dot(h, wi_ref[:, D:2 * D]) * _dot(h, wi_ref[:, 2 * D:3 * D])
    b_gate = _dot(h[HALO:HALO + tm, :], wi_ref[:, 0:D])
    conv = _dwconv3_rows(_zero_outside(cu, tm, rpb), cw_ref[...])
    y = _dot((b_gate * conv).astype(BF), wo_ref[...])
    x1 = _layer_norm(ALPHA * x + mod_ref[5:6, :] * y, g1_ref[...], b1_ref[...])
    _ffn_slabs(lambda p: x1[p * FFN_SLAB:(p + 1) * FFN_SLAB, :], tm // FFN_SLAB, mod_ref, 6,
               fwi_ref, fwo_ref, g2_ref, b2_ref, o_ref, act_ref)


def _short_conv_ffn(x2, mods, layer, sel, wi, conv_w, wo, fwi, fwo, lng, lnb, seq, tm):
    t = x2.shape[0]
    return pl.pallas_call(
        functools.partial(_sc_ffn_kernel, tm=tm, rpb=seq // tm),
        grid=(t // tm,),
        in_specs=_halo_specs(tm, t) + [
            _mod_spec(layer, sel), _layer_spec((D, 3 * D), 0), _layer_spec((3, D), 0),
            _layer_spec((D, D), 0), _layer_spec((1, D), 3 * layer + 1),
            _layer_spec((1, D), 3 * layer + 1)] + _ffn_specs(layer, 1),
        out_specs=_rows(tm, D),
        out_shape=jax.ShapeDtypeStruct((t, D), F32),
        scratch_shapes=[pltpu.VMEM((tm, D_FF), BF)],
        compiler_params=_params(1),
        name="short_conv_ffn",
    )(x2, x2, x2, mods, wi, conv_w, wo, lng, lnb, fwi, fwo, lng, lnb)


def kernel(x, c, ctx, c_ctx, ada_w, ada_b, ln_g, ln_b, ffa_wi, ffa_wo, ffb_wi, ffb_wo, da_w_qkv, da_w_o, da_lambda, da_subln_g, hy_w_in, hy_conv_w, hy_conv_b, hy_fw1, hy_fb1, hy_ff1, hy_fw2, hy_fb2, hy_ff2, hy_fw3, hy_d_skip, hy_w_o, rt_w_in, rt_decay_logit, rt_gn_g, rt_w_o, sc_w_in, sc_conv_w, sc_w_o):
    bsz, seq, _ = x.shape
    n_ctx = ctx.shape[1]
    assert x.shape[2] == D and ada_w.shape[0] == DEPTH and seq % GRID_W == 0
    tm = min(512, seq)
    tmc = min(512, n_ctx)
    assert seq % tm == 0 and n_ctx % tmc == 0 and seq % RT_CHUNK == 0 and n_ctx % min(RT_CHUNK, n_ctx) == 0

    n_rows = -(-(bsz + 1) // HALO) * HALO
    cond = jnp.zeros((n_rows, D), F32).at[:bsz].set(c).at[bsz].set(c_ctx)
    mods = _modulation_all(cond, ada_w, ada_b)

    rpb = seq // tm
    lat = (lambda i: i // rpb, seq, tm)
    cx = (lambda i: bsz, n_ctx, tmc)

    xl = x.reshape(bsz * seq, D)
    xc = ctx.reshape(bsz * n_ctx, D)
    lng = ln_g.reshape(DEPTH * 3, 1, D)
    lnb = ln_b.reshape(DEPTH * 3, 1, D)
    ffw = {(0, 0): (ffa_wi[0].astype(BF), ffa_wo[0].astype(BF))}
    f32_stacks = {0: (ffa_wi, ffa_wo), 1: (ffb_wi, ffb_wo)}
    mixer_jobs = {(0, 0): [("da_qkv", da_w_qkv), ("da_o", da_w_o)],
                  (0, 1): [("hy_in", hy_w_in), ("hy_o", hy_w_o)],
                  (1, 1): [("rt_in", rt_w_in), ("rt_o", rt_w_o)],
                  (2, 1): [("sc_in", sc_w_in), ("sc_o", sc_w_o)]}
    mixw = {}

    def jobs_for(i, which, stream):
        if stream is not lat:
            return [], None
        nxt = (i, 1) if which == 0 else (i + 1, 0)
        jobs = [(s, nxt[0]) for s in f32_stacks[nxt[1]]] if nxt[0] < DEPTH else []
        return jobs + [(stack, 0) for _, stack in mixer_jobs.get((i, which), [])], nxt

    def keep_casts(i, which, nxt, casts):
        if nxt is None:
            return
        if nxt[0] < DEPTH:
            ffw[nxt], casts = (casts[0], casts[1]), casts[2:]
        for (name, _), w in zip(mixer_jobs.get((i, which), []), casts):
            mixw[name] = w[None]

    def row_tile(stream, want):
        rows = seq if stream is lat else bsz * n_ctx
        tile = min(want, rows)
        assert rows % tile == 0
        sel = (lambda j: j // (seq // tile)) if stream is lat else stream[0]
        return tile, sel

    def ffn_a(xx, i, stream):
        tile, sel = row_tile(stream, FFN_TILE)
        jobs, nxt = jobs_for(i, 0, stream)
        out, casts = _half_ffn(xx, mods, i, 0, sel, ffw[(i, 0)], lng, lnb, tile, jobs)
        keep_casts(i, 0, nxt, casts)
        return out

    def mix_out(a, w_o, xx, i, stream):
        tile, sel = row_tile(stream, tm)
        jobs, nxt = jobs_for(i, 1, stream)
        out, casts = _outproj_ffn(a, w_o, xx, mods, i, sel, ffw[(i, 1)], lng, lnb, tile, jobs)
        keep_casts(i, 1, nxt, casts)
        return out

    def by_sample(a, n):
        return a.reshape(bsz, n, a.shape[-1])

    i = 0
    xl, xc = ffn_a(xl, i, lat), ffn_a(xc, i, cx)
    w_qkv = mixw["da_qkv"]
    ql, kl, vtl = _proj_da(xl, mods, i, lat[0], w_qkv, _da_rope_tables(seq), seq, tm)
    qc, kc, vtc = _proj_da(xc, mods, i, cx[0], w_qkv, None, n_ctx, tmc)
    kl, kc = by_sample(kl, seq), by_sample(kc, n_ctx)
    lam_init = 0.8 - 0.6 * math.exp(-0.3 * i)
    subln = da_subln_g.reshape(-1, 1, LANES)
    ol = _diff_attn(ql, [(kc, vtc), (kl, vtl)], da_lambda, subln, lam_init, min(ATTN_TILE, seq))
    oc = _diff_attn(qc, [(kc, vtc)], da_lambda, subln, lam_init, min(ATTN_TILE, n_ctx))
    w_o = mixw["da_o"]
    xl = mix_out(ol.reshape(bsz * seq, D), w_o, xl, i, lat)
    xc = mix_out(oc.reshape(bsz * n_ctx, D), w_o, xc, i, cx)

    i = 1
    xl, xc = ffn_a(xl, i, lat), ffn_a(xc, i, cx)
    w_in, w_o = mixw["hy_in"], mixw["hy_o"]
    conv_b = hy_conv_b.reshape(-1, 1, 3 * D)
    d_skip = hy_d_skip.reshape(-1, 1, D)
    tc = 256

    def hyena(xx, stream):
        sel, n, tile = stream
        tabs = _dft_tables(n)
        spectra = _hy_filter(n, tabs, hy_fw1[0], hy_fb1[0], hy_ff1[0], hy_fw2[0], hy_fb2[0],
                             hy_ff2[0], hy_fw3[0], tc)
        x0, z = _proj_hy(xx, mods, i, sel, w_in, hy_conv_w, conv_b, n, tile)
        y = _hy_conv(by_sample(z, n), by_sample(x0, n), spectra, d_skip, tabs, tc)
        return mix_out(y.reshape(bsz * n, D), w_o, xx, i, stream)

    xl, xc = hyena(xl, lat), hyena(xc, cx)

    i = 2
    xl, xc = ffn_a(xl, i, lat), ffn_a(xc, i, cx)
    w_in = mixw["rt_in"]
    ql, kl, vl, gl = [by_sample(a, seq) for a in
                      _proj_rt(xl, mods, i, lat[0], w_in, _rt_rope_tables(seq), seq, tm)]
    kc, vc = [by_sample(a, n_ctx) for a in _proj_rt(xc, mods, i, cx[0], w_in, None, n_ctx, tmc)]
    o = _retention(kc, vc, ql, kl, vl, gl, rt_decay_logit[0], rt_gn_g.reshape(-1, 1, RT_HEADS * RT_V))
    xl = mix_out(o.reshape(bsz * seq, RT_HEADS * RT_V), mixw["rt_o"], xl, i, lat)

    i = 3
    xl = ffn_a(xl, i, lat)
    xl = _short_conv_ffn(xl, mods, i, lat[0], mixw["sc_in"], sc_conv_w, mixw["sc_o"],
                         ffw[(i, 1)][0], ffw[(i, 1)][1], lng, lnb, seq, tm)
    return xl.reshape(bsz, seq, D)
```

```python
import functools
import math

import jax
import jax.numpy as jnp
import numpy as np
from jax import lax
from jax.experimental import pallas as pl
from jax.experimental.pallas import tpu as pltpu

D = 1024
N_MOD = 9
D_FF = 2816
LN_EPS = 1e-5
ROPE_BASE = 10000.0
GRID_W = 64
DEPTH = 4
ALPHA = (2.0 * DEPTH) ** 0.25
DA_HEADS = 8
DA_HEAD_DIM = 64
RT_HEADS = 4
RT_QK = 256
RT_V = 512
HY_EMB = 33
HY_FH = 64
LANES = 128
HALO = 8
FF_CHUNK = 256
RT_CHUNK = 256
FFN_TILE = 1024
FFN_SLAB = 256
RIDER_BLOCKS = 16
ATTN_TILE = 2048
ATTN_PART = 512
MOD_TILE = 2304
HY_TILE = 256
V7X_VMEM_BYTES = 64 * 1024 * 1024
VMEM_LIMIT = V7X_VMEM_BYTES - 8 * 1024 * 1024
LOG2E = math.log2(math.e)

F32 = jnp.float32
BF = jnp.bfloat16
NT = (((1,), (1,)), ((), ()))
TN = (((0,), (0,)), ((), ()))


def _dot(a, b):
    return jnp.dot(a, b, preferred_element_type=F32)


def _resident(shape):
    nd = len(shape)
    return pl.BlockSpec(shape, lambda *_: (0,) * nd, pipeline_mode=pl.Buffered(1))


def _layer_spec(shape, layer):
    nd = len(shape)
    return pl.BlockSpec((None,) + tuple(shape), lambda *_: (layer,) + (0,) * nd,
                        pipeline_mode=pl.Buffered(1))


def _mod_spec(layer, sel):
    return pl.BlockSpec((None, None, N_MOD, D), lambda i: (layer, sel(i), 0, 0))


def _rows(tm, width):
    return pl.BlockSpec((tm, width), lambda i: (i, 0))


def _params(n_axes, semantics="parallel"):
    return pltpu.CompilerParams(dimension_semantics=(semantics,) * n_axes,
                                vmem_limit_bytes=VMEM_LIMIT)


def _layer_norm(r, g, b):
    mu = jnp.mean(r, -1, keepdims=True)
    d = r - mu
    var = jnp.mean(d * d, -1, keepdims=True)
    return d * lax.rsqrt(var + LN_EPS) * g + b


def _silu(a):
    return a * jax.nn.sigmoid(a)


def _modulated(x, mod_ref, j):
    return (x * (1.0 + mod_ref[j + 1:j + 2, :]) + mod_ref[j:j + 1, :]).astype(BF)


def _mod_kernel(c_ref, w_ref, b_ref, o_ref):
    s = _silu(c_ref[...]).astype(BF)
    o_ref[...] = _dot(s, w_ref[...].astype(BF)) + b_ref[...]


def _modulation_all(cond, ada_w, ada_b):
    r = cond.shape[0]
    tn = MOD_TILE
    n = N_MOD * D
    out = pl.pallas_call(
        _mod_kernel,
        grid=(DEPTH, n // tn),
        in_specs=[pl.BlockSpec((r, D), lambda i, j: (0, 0)),
                  pl.BlockSpec((None, D, tn), lambda i, j: (i, 0, j)),
                  pl.BlockSpec((None, 1, tn), lambda i, j: (i, 0, j))],
        out_specs=pl.BlockSpec((None, r, tn), lambda i, j: (i, 0, j)),
        out_shape=jax.ShapeDtypeStruct((DEPTH, r, n), F32),
        compiler_params=_params(2),
        name="modulation",
    )(cond, ada_w, ada_b.reshape(DEPTH, 1, n))
    return out.reshape(DEPTH, r, N_MOD, D)


def _ffn_slabs(x_of, n_slabs, mod_ref, j0, wi_ref, wo_ref, g_ref, b_ref, o_ref, act_ref):
    rows = o_ref.shape[0] // n_slabs
    xs = {}

    def up(p):
        xs[p] = x_of(p)
        h = _modulated(xs[p], mod_ref, j0)
        for c in range(D_FF // FF_CHUNK):
            lo = c * FF_CHUNK
            a = _dot(h, wi_ref[:, lo:lo + FF_CHUNK])
            u = _dot(h, wi_ref[:, D_FF + lo:D_FF + lo + FF_CHUNK])
            act_ref[p * rows:(p + 1) * rows, lo:lo + FF_CHUNK] = (_silu(a) * u).astype(BF)

    def down(p):
        y = _dot(act_ref[p * rows:(p + 1) * rows, :], wo_ref[...])
        r = ALPHA * xs.pop(p) + (0.5 * mod_ref[j0 + 2:j0 + 3, :]) * y
        o_ref[p * rows:(p + 1) * rows, :] = _layer_norm(r, g_ref[...], b_ref[...])

    for p in range(n_slabs):
        up(p)
        if p > 0:
            down(p - 1)
    down(n_slabs - 1)


def _ffn_specs(layer, which):
    return [_resident((D, 2 * D_FF)), _resident((D_FF, D)),
            _layer_spec((1, D), 3 * layer + 2 * which), _layer_spec((1, D), 3 * layer + 2 * which)]


def _rider_specs(jobs, steps):
    blocks = min(RIDER_BLOCKS, steps)
    reps = steps // blocks
    assert steps % blocks == 0
    in_specs, out_specs, out_shape = [], [], []
    for stack, layer in jobs:
        _, r, c = stack.shape
        in_specs.append(pl.BlockSpec((None, r // blocks, c), lambda i, layer=layer: (layer, i // reps, 0)))
        out_specs.append(pl.BlockSpec((r // blocks, c), lambda i: (i // reps, 0)))
        out_shape.append(jax.ShapeDtypeStruct((r, c), BF))
    return in_specs, out_specs, out_shape


def _split_rider(rest, n_jobs):
    o_ref, act_ref = rest[n_jobs], rest[-1]
    for src_ref, dst_ref in zip(rest[:n_jobs], rest[n_jobs + 1:-1]):
        dst_ref[...] = src_ref[...].astype(BF)
    return o_ref, act_ref


def _ffn_kernel(x_ref, mod_ref, wi_ref, wo_ref, g_ref, b_ref, *rest, j0, n_slabs, n_jobs):
    o_ref, act_ref = _split_rider(rest, n_jobs)
    rows = x_ref.shape[0] // n_slabs
    _ffn_slabs(lambda p: x_ref[p * rows:(p + 1) * rows, :], n_slabs, mod_ref, j0, wi_ref, wo_ref,
               g_ref, b_ref, o_ref, act_ref)


def _call_with_rider(kernel_fn, name, t, tm, in_specs, args, jobs):
    r_in, r_out, r_shape = _rider_specs(jobs, t // tm)
    outs = pl.pallas_call(
        functools.partial(kernel_fn, n_jobs=len(jobs)),
        grid=(t // tm,),
        in_specs=in_specs + r_in,
        out_specs=[_rows(tm, D)] + r_out,
        out_shape=[jax.ShapeDtypeStruct((t, D), F32)] + r_shape,
        scratch_shapes=[pltpu.VMEM((tm, D_FF), BF)],
        compiler_params=_params(1, "arbitrary" if jobs else "parallel"),
        name=name,
    )(*args, *[stack for stack, _ in jobs])
    return outs[0], list(outs[1:])


def _half_ffn(x2, mods, layer, which, sel, weights, lng, lnb, tm, rider):
    return _call_with_rider(
        functools.partial(_ffn_kernel, j0=6 * which, n_slabs=tm // FFN_SLAB), "half_ffn", x2.shape[0], tm,
        [_rows(tm, D), _mod_spec(layer, sel)] + _ffn_specs(layer, which),
        [x2, mods, weights[0], weights[1], lng, lnb], rider)


def _outproj_ffn_kernel(a_ref, w_ref, x_ref, mod_ref, g1_ref, b1_ref, wi_ref, wo_ref, g2_ref, b2_ref,
                        *rest, n_jobs):
    o_ref, act_ref = _split_rider(rest, n_jobs)
    n_slabs = x_ref.shape[0] // FFN_SLAB

    def mixed(p):
        rows = slice(p * FFN_SLAB, (p + 1) * FFN_SLAB)
        y = _dot(a_ref[rows, :], w_ref[...])
        return _layer_norm(ALPHA * x_ref[rows, :] + mod_ref[5:6, :] * y, g1_ref[...], b1_ref[...])

    _ffn_slabs(mixed, n_slabs, mod_ref, 6, wi_ref, wo_ref, g2_ref, b2_ref, o_ref, act_ref)


def _outproj_ffn(a2, w, x2, mods, layer, sel, weights, lng, lnb, tm, rider):
    t, k = a2.shape
    return _call_with_rider(
        _outproj_ffn_kernel, "outproj_ffn", t, tm,
        [_rows(tm, k), _layer_spec((k, D), 0), _rows(tm, D), _mod_spec(layer, sel),
         _layer_spec((1, D), 3 * layer + 1), _layer_spec((1, D), 3 * layer + 1)] + _ffn_specs(layer, 1),
        [a2, w, x2, mods, lng, lnb, weights[0], weights[1], lng, lnb], rider)


def _rope_angles(n_tokens, dim):
    rows = n_tokens // GRID_W
    row = np.repeat(np.arange(rows), GRID_W).astype(np.float32)
    col = np.tile(np.arange(GRID_W), rows).astype(np.float32)
    n_freq = dim // 4
    inv = (ROPE_BASE ** (-np.arange(n_freq, dtype=np.float32) / n_freq)).astype(np.float32)
    return np.concatenate([row[:, None] * inv, col[:, None] * inv], axis=-1)


def _da_rope_tables(n_tokens):
    ang = _rope_angles(n_tokens, DA_HEAD_DIM)
    cos, sin = np.cos(ang), np.sin(ang)
    cos128 = np.tile(cos, (1, 4))
    sin128 = np.tile(np.concatenate([-sin, sin], axis=-1), (1, 2))
    return jnp.asarray(cos128, F32), jnp.asarray(sin128, F32)


def _rt_rope_tables(n_tokens):
    ang = _rope_angles(n_tokens, RT_QK)
    return jnp.asarray(np.cos(ang), F32), jnp.asarray(np.sin(ang), F32)


def _proj_da_kernel(*refs, rope):
    if rope:
        x_ref, mod_ref, w_ref, cos_ref, sin_ref, q_ref, k_ref, v_ref = refs
        cos, sin = cos_ref[...], sin_ref[...]
        lane = lax.broadcasted_iota(jnp.int32, (1, LANES), 1)
        first_half = (lane % DA_HEAD_DIM) < (DA_HEAD_DIM // 2)
    else:
        x_ref, mod_ref, w_ref, q_ref, k_ref, v_ref = refs
    h = _modulated(x_ref[...], mod_ref, 3)

    def rot(y):
        if not rope:
            return y
        partner = jnp.where(first_half, pltpu.roll(y, LANES - 32, 1), pltpu.roll(y, 32, 1))
        return y * cos + partner * sin

    q = _dot(h, w_ref[:, 0:D]) * (DA_HEAD_DIM ** -0.5 * LOG2E)
    for j in range(D // LANES):
        q_ref[j * LANES:(j + 1) * LANES, :] = rot(q[:, j * LANES:(j + 1) * LANES]).T.astype(BF)
    k = _dot(h, w_ref[:, D:2 * D])
    for j in range(D // LANES):
        k_ref[:, j * LANES:(j + 1) * LANES] = rot(k[:, j * LANES:(j + 1) * LANES]).astype(BF)
    v_ref[...] = _dot(h, w_ref[:, 2 * D:3 * D]).T.astype(BF)


def _proj_da(x2, mods, layer, sel, w, rope_tabs, seq, tm):
    t = x2.shape[0]
    rpb = seq // tm
    rope = rope_tabs is not None
    in_specs = [_rows(tm, D), _mod_spec(layer, sel), _layer_spec((D, 3 * D), 0)]
    args = [x2, mods, w]
    if rope:
        in_specs += [pl.BlockSpec((tm, LANES), lambda i: (i % rpb, 0))] * 2
        args += list(rope_tabs)
    return pl.pallas_call(
        functools.partial(_proj_da_kernel, rope=rope),
        grid=(t // tm,),
        in_specs=in_specs,
        out_specs=[pl.BlockSpec((None, D, tm), lambda i: (i // rpb, 0, i % rpb)), _rows(tm, D),
                   pl.BlockSpec((None, D, tm), lambda i: (i // rpb, 0, i % rpb))],
        out_shape=[jax.ShapeDtypeStruct((t // seq, D, seq), BF), jax.ShapeDtypeStruct((t, D), BF),
                   jax.ShapeDtypeStruct((t // seq, D, seq), BF)],
        compiler_params=_params(1),
        name="proj_da",
    )(*args)


def _attn_kernel(*refs, lam_init, n_kv, parts):
    lam_ref, g_ref, q_ref = refs[:3]
    kv_refs = [(refs[3 + 2 * j], refs[4 + 2 * j]) for j in range(n_kv)]
    o_ref = refs[3 + 2 * n_kv]
    width = q_ref.shape[1] // parts
    feature = lax.broadcasted_iota(jnp.int32, (LANES, width), 0)

    def scores(part, head_map):
        q = q_ref[:, part * width:(part + 1) * width]
        keep = (feature < DA_HEAD_DIM) if head_map == 0 else (feature >= DA_HEAD_DIM)
        qm = jnp.where(keep, q, jnp.zeros_like(q))
        return [_dot(k_ref[...], qm) for k_ref, _ in kv_refs]

    def attend(ss):
        m = functools.reduce(jnp.maximum, [jnp.max(s, 0, keepdims=True) for s in ss])
        es = [jnp.exp2(s - m) for s in ss]
        den = functools.reduce(jnp.add, [jnp.sum(e, 0, keepdims=True) for e in es])
        o = functools.reduce(jnp.add, [_dot(vt_ref[...], e.astype(BF))
                                       for e, (_, vt_ref) in zip(es, kv_refs)])
        return o / den

    chains = [(part, head_map) for part in range(parts) for head_map in range(2)]
    result, pending = {}, None
    for chain in chains:
        ss = scores(*chain)
        if pending is not None:
            result[pending[0]] = attend(pending[1])
        pending = (chain, ss)
    result[pending[0]] = attend(pending[1])

    lam = lam_ref[...]
    lam_full = (jnp.exp(jnp.sum(lam[0:1] * lam[1:2], -1, keepdims=True))
                - jnp.exp(jnp.sum(lam[2:3] * lam[3:4], -1, keepdims=True)) + lam_init)
    for part in range(parts):
        o = result[(part, 0)] - lam_full * result[(part, 1)]
        o = o * lax.rsqrt(jnp.mean(o * o, 0, keepdims=True) + LN_EPS)
        o_ref[part * width:(part + 1) * width, :] = (o.T * g_ref[...] * (1.0 - lam_init)).astype(BF)


def _diff_attn(q, kvs, lam, subln_g, lam_init, tq):
    b, _, n = q.shape
    in_specs = [_layer_spec((4, DA_HEAD_DIM), 0), _layer_spec((1, LANES), 0),
                pl.BlockSpec((None, LANES, tq), lambda bi, h, i: (bi, h, i))]
    args = [lam, subln_g, q]
    for k, vt in kvs:
        n_k = k.shape[1]
        in_specs += [pl.BlockSpec((None, n_k, LANES), lambda bi, h, i: (bi, 0, h)),
                     pl.BlockSpec((None, LANES, n_k), lambda bi, h, i: (bi, h, 0))]
        args += [k, vt]
    return pl.pallas_call(
        functools.partial(_attn_kernel, lam_init=lam_init, n_kv=len(kvs),
                          parts=max(1, tq // ATTN_PART)),
        grid=(b, DA_HEADS, n // tq),
        in_specs=in_specs,
        out_specs=pl.BlockSpec((None, tq, LANES), lambda bi, h, i: (bi, i, h)),
        out_shape=jax.ShapeDtypeStruct((b, n, D), BF),
        compiler_params=_params(3),
        name="diff_attn",
    )(*args)


def _halo_specs(tm, t):
    blocks_per_tile = tm // HALO
    last = t // HALO - 1
    return [pl.BlockSpec((HALO, D), lambda i: (jnp.maximum(i * blocks_per_tile - 1, 0), 0)),
            _rows(tm, D),
            pl.BlockSpec((HALO, D), lambda i: (jnp.minimum((i + 1) * blocks_per_tile, last), 0))]


def _zero_outside(u, tm, rpb):
    pos = pl.program_id(0) % rpb
    keep_prev = jnp.where(pos == 0, 0.0, 1.0)
    keep_next = jnp.where(pos == rpb - 1, 0.0, 1.0)
    return jnp.concatenate([u[:HALO] * keep_prev, u[HALO:HALO + tm], u[HALO + tm:] * keep_next], axis=0)


def _dwconv3_rows(u, w):
    n = u.shape[0]
    y = pltpu.roll(u, 1, 0) * w[0:1, :] + u * w[1:2, :] + pltpu.roll(u, n - 1, 0) * w[2:3, :]
    return y[HALO:n - HALO, :]


def _proj_hy_kernel(xp_ref, x_ref, xn_ref, mod_ref, w_ref, cw_ref, cb_ref, x0_ref, z_ref, *, tm, rpb):
    xe = jnp.concatenate([xp_ref[...], x_ref[...], xn_ref[...]], axis=0)
    h = _modulated(xe, mod_ref, 3)

    def project(j):
        return _dot(h, w_ref[:, j * D:(j + 1) * D])

    def conv(u, j):
        u = _zero_outside(u, tm, rpb)
        return _dwconv3_rows(u, cw_ref[:, j * D:(j + 1) * D]) + cb_ref[:, j * D:(j + 1) * D]

    u0 = project(0)
    u1 = project(1)
    x0_ref[...] = conv(u0, 0)
    u2 = project(2)
    x1 = conv(u1, 1)
    z_ref[...] = x1 * conv(u2, 2)


def _proj_hy(x2, mods, layer, sel, w, conv_w, conv_b, seq, tm):
    t = x2.shape[0]
    return pl.pallas_call(
        functools.partial(_proj_hy_kernel, tm=tm, rpb=seq // tm),
        grid=(t // tm,),
        in_specs=_halo_specs(tm, t) + [
            _mod_spec(layer, sel), _layer_spec((D, 3 * D), 0), _layer_spec((3, 3 * D), 0),
            _layer_spec((1, 3 * D), 0)],
        out_specs=[_rows(tm, D)] * 2,
        out_shape=[jax.ShapeDtypeStruct((t, D), F32)] * 2,
        compiler_params=_params(1),
        name="proj_hy",
    )(x2, x2, x2, mods, w, conv_w, conv_b)


def _dft_tables(n):
    m = n // 2
    k = np.arange(m + HALO, dtype=np.int64)[:, None]
    valid = k <= m
    s_even = 2 * np.arange(m, dtype=np.int64)[None, :]

    def table(fn, offset):
        phase = (k * (s_even + offset)) % (2 * n)
        return np.where(valid, fn(phase.astype(np.float64) * (math.pi / n)), 0.0).astype(np.float32)

    ce, se, co, so = table(np.cos, 0), table(np.sin, 0), table(np.cos, 1), table(np.sin, 1)
    as_bf = lambda a: jnp.asarray(np.ascontiguousarray(a)).astype(BF)
    fwd = (as_bf(np.concatenate([ce, se], 0)), as_bf(np.concatenate([co, so], 0)))
    inv = tuple(as_bf(t[:m].T) for t in (ce, se, co, so))
    return fwd, inv


def _parity_split(v, split_ref):
    n, width = v.shape
    slabs = width // LANES
    for c in range(slabs):
        split_ref[c] = v[:, c * LANES:(c + 1) * LANES]
    take = lambda start: jnp.concatenate(
        [split_ref[c, pl.ds(start, n // 2, stride=2), :] for c in range(slabs)], axis=1)
    return take(0), take(1)


def _parity_merge(even, odd, split_ref):
    m, width = even.shape
    slabs = width // LANES
    for c in range(slabs):
        split_ref[c, pl.ds(0, m, stride=2), :] = even[:, c * LANES:(c + 1) * LANES]
        split_ref[c, pl.ds(1, m, stride=2), :] = odd[:, c * LANES:(c + 1) * LANES]
    return jnp.concatenate([split_ref[c] for c in range(slabs)], axis=1)


def _hy_features(n):
    t = np.linspace(0.0, 1.0, n, dtype=np.float32)[:, None]
    bands = (HY_EMB - 1) // 2
    fr = np.linspace(1e-4, bands - 1, bands, dtype=np.float32)[None, :]
    w = (2.0 * math.pi * np.arange(n, dtype=np.float32)[:, None] / n).astype(np.float32)
    z = np.concatenate([t, np.cos(fr * w), -np.sin(fr * w)], axis=-1).astype(np.float32)
    return jnp.asarray(np.pad(z, ((0, 0), (0, LANES - HY_EMB))), F32)


def _hy_deltas():
    max_decay = math.log(1e-2) / 0.3
    min_decay = math.log(1e-2) / 1.5
    return jnp.asarray(np.abs(np.linspace(min_decay, max_decay, D, dtype=np.float32))[None, :], F32)


def _hy_filter_kernel(z_ref, w1_ref, b1_ref, f1_ref, w2_ref, b2_ref, f2_ref, w3f_ref, w3b_ref,
                      dl_ref, fe_ref, fo_ref, plo_ref, qlo_ref, phi_ref, qhi_ref, split_ref, hid_ref):
    n = z_ref.shape[0]
    mp = n // 2 + HALO
    z = z_ref[...]

    @pl.when(pl.program_id(0) == 0)
    def _():
        h = jnp.sin(f1_ref[...] * (_dot(z.astype(BF), w1_ref[...].astype(BF)) + b1_ref[...]))
        h = jnp.sin(f2_ref[...] * (_dot(h.astype(BF), w2_ref[...].astype(BF)) + b2_ref[...]))
        hid_ref[...] = h.astype(BF)

    hb16 = hid_ref[...]
    decay = jnp.exp(-z[:, 0:1] * dl_ref[...])
    h_f = _dot(hb16, w3f_ref[...].astype(BF)) * decay
    h_b = _dot(hb16, w3b_ref[...].astype(BF)) * decay
    row = lax.broadcasted_iota(jnp.int32, h_f.shape, 0)
    h_b = jnp.where(row == 0, 0.0, h_b)
    sm_even, sm_odd = _parity_split(h_f + h_b, split_ref)
    a_e = _dot(fe_ref[0:mp, :], sm_even.astype(BF))
    a_o = _dot(fo_ref[0:mp, :], sm_odd.astype(BF))
    df_even, df_odd = _parity_split(h_f - h_b, split_ref)
    b_e = _dot(fe_ref[mp:2 * mp, :], df_even.astype(BF))
    b_o = _dot(fo_ref[mp:2 * mp, :], df_odd.astype(BF))
    frow = lax.broadcasted_iota(jnp.int32, a_e.shape, 0)
    w_lo = jnp.where(frow == 0, 1.0, 2.0) * (0.5 / n)
    w_hi = jnp.where(frow == 0, 1.0, jnp.where(frow < n // 2, 2.0, 0.0)) * (0.5 / n)
    plo_ref[...] = (a_e + a_o) * w_lo
    qlo_ref[...] = -(b_e + b_o) * w_lo
    phi_ref[...] = (a_e - a_o) * w_hi
    qhi_ref[...] = (b_e - b_o) * w_hi


def _hy_filter(n, tabs, fw1, fb1, ff1, fw2, fb2, ff2, fw3, tc):
    (fwd_even, fwd_odd), _ = tabs
    w1 = jnp.pad(fw1, ((0, LANES - HY_EMB), (0, 0)))
    nt = D // tc
    m, mp = n // 2, n // 2 + HALO
    small = lambda s: pl.BlockSpec(s, lambda j: (0, 0))
    return pl.pallas_call(
        _hy_filter_kernel,
        grid=(nt,),
        in_specs=[small((n, LANES)), small((LANES, HY_FH)), small((1, HY_FH)), small((1, HY_FH)),
                  small((HY_FH, HY_FH)), small((1, HY_FH)), small((1, HY_FH)),
                  pl.BlockSpec((HY_FH, tc), lambda j: (0, j)),
                  pl.BlockSpec((HY_FH, tc), lambda j: (0, j + nt)),
                  pl.BlockSpec((1, tc), lambda j: (0, j)),
                  _resident((2 * mp, m)), _resident((2 * mp, m))],
        out_specs=[pl.BlockSpec((mp, tc), lambda j: (0, j))] * 4,
        out_shape=[jax.ShapeDtypeStruct((mp, D), F32)] * 4,
        scratch_shapes=[pltpu.VMEM((tc // LANES, n, LANES), F32), pltpu.VMEM((n, HY_FH), BF)],
        compiler_params=pltpu.CompilerParams(dimension_semantics=("arbitrary",),
                                             vmem_limit_bytes=VMEM_LIMIT),
        name="hy_filter",
    )(_hy_features(n), w1, fb1.reshape(1, -1), ff1.reshape(1, -1), fw2, fb2.reshape(1, -1),
      ff2.reshape(1, -1), fw3, fw3, _hy_deltas(), fwd_even, fwd_odd)


def _hy_conv_kernel(z_ref, x0_ref, plo_ref, qlo_ref, phi_ref, qhi_ref, ds_ref, fe_ref, fo_ref,
                    cet_ref, set_ref, cot_ref, sot_ref, o_ref, split_ref):
    n = z_ref.shape[0]
    m, mp = n // 2, n // 2 + HALO
    z = z_ref[...]
    z_even, z_odd = _parity_split(z, split_ref)
    r_e = _dot(fe_ref[...], z_even.astype(BF))
    r_o = _dot(fo_ref[...], z_odd.astype(BF))
    a_e, b_e, a_o, b_o = r_e[0:mp], r_e[mp:2 * mp], r_o[0:mp], r_o[mp:2 * mp]
    a_lo, b_lo, a_hi, b_hi = a_e + a_o, b_e + b_o, a_e - a_o, b_o - b_e
    p, q = plo_ref[...], qlo_ref[...]
    yr_lo, yi_lo = a_lo * p + b_lo * q, a_lo * q - b_lo * p
    p, q = phi_ref[...], qhi_ref[...]
    yr_hi, yi_hi = a_hi * p + b_hi * q, a_hi * q - b_hi * p
    y_even = (_dot(cet_ref[...], (yr_lo + yr_hi)[0:m].astype(BF))
              - _dot(set_ref[...], (yi_lo - yi_hi)[0:m].astype(BF)))
    y_odd = (_dot(cot_ref[...], (yr_lo - yr_hi)[0:m].astype(BF))
             - _dot(sot_ref[...], (yi_lo + yi_hi)[0:m].astype(BF)))
    half = lax.broadcasted_iota(jnp.int32, y_even.shape, 0)
    sign = (1 - 2 * (half % 2)).astype(F32)
    y_even = y_even + sign * yr_lo[m:m + 1, :]
    y_odd = y_odd - sign * yi_lo[m:m + 1, :]
    y = _parity_merge(y_even, y_odd, split_ref)
    o_ref[...] = (x0_ref[...] * (y + z * ds_ref[...])).astype(BF)


def _hy_conv(z, x0, spectra, d_skip, tabs, tc):
    b, n, _ = z.shape
    (fwd_even, fwd_odd), inv = tabs
    m, mp = n // 2, n // 2 + HALO
    sample = pl.BlockSpec((None, n, tc), lambda j, bi: (bi, 0, j))
    return pl.pallas_call(
        _hy_conv_kernel,
        grid=(D // tc, b),
        in_specs=[sample, sample] + [pl.BlockSpec((mp, tc), lambda j, bi: (0, j))] * 4
                 + [pl.BlockSpec((None, 1, tc), lambda j, bi: (0, 0, j)),
                    _resident((2 * mp, m)), _resident((2 * mp, m))] + [_resident((m, m))] * 4,
        out_specs=sample,
        out_shape=jax.ShapeDtypeStruct((b, n, D), BF),
        scratch_shapes=[pltpu.VMEM((tc // LANES, n, LANES), F32)],
        compiler_params=_params(2),
        name="hy_conv",
    )(z, x0, *spectra, d_skip, fwd_even, fwd_odd, *inv)


def _proj_rt_kernel(*refs, rope):
    if rope:
        x_ref, mod_ref, w_ref, cos_ref, sin_ref, q_ref, k_ref, v_ref, g_ref = refs
        cos, sin = cos_ref[...], sin_ref[...]
    else:
        x_ref, mod_ref, w_ref, k_ref, v_ref = refs
    h = _modulated(x_ref[...], mod_ref, 3)
    qk_w = RT_HEADS * RT_QK
    half = RT_QK // 2

    def store_rot(y, ref):
        for hd in range(RT_HEADS):
            lo = hd * RT_QK
            x1, x2 = y[:, lo:lo + half], y[:, lo + half:lo + RT_QK]
            if rope:
                x1, x2 = x1 * cos - x2 * sin, x1 * sin + x2 * cos
            ref[:, lo:lo + half] = x1.astype(BF)
            ref[:, lo + half:lo + RT_QK] = x2.astype(BF)

    if rope:
        store_rot(_dot(h, w_ref[:, 0:qk_w]), q_ref)
    store_rot(_dot(h, w_ref[:, qk_w:2 * qk_w]) * (RT_QK ** -0.5), k_ref)
    v_w = RT_HEADS * RT_V
    v_ref[...] = _dot(h, w_ref[:, 2 * qk_w:2 * qk_w + v_w]).astype(BF)
    if rope:
        g_ref[...] = _silu(_dot(h, w_ref[:, 2 * qk_w + v_w:2 * qk_w + 2 * v_w])).astype(BF)


def _proj_rt(x2, mods, layer, sel, w, rope_tabs, seq, tm):
    t = x2.shape[0]
    rpb = seq // tm
    rope = rope_tabs is not None
    qk_w, v_w = RT_HEADS * RT_QK, RT_HEADS * RT_V
    in_specs = [_rows(tm, D), _mod_spec(layer, sel), _layer_spec((D, 2 * qk_w + 2 * v_w), 0)]
    args = [x2, mods, w]
    if rope:
        in_specs += [pl.BlockSpec((tm, LANES), lambda i: (i % rpb, 0))] * 2
        args += list(rope_tabs)
        out_specs = [_rows(tm, qk_w), _rows(tm, qk_w), _rows(tm, v_w), _rows(tm, v_w)]
        out_shape = [jax.ShapeDtypeStruct((t, qk_w), BF), jax.ShapeDtypeStruct((t, qk_w), BF),
                     jax.ShapeDtypeStruct((t, v_w), BF), jax.ShapeDtypeStruct((t, v_w), BF)]
    else:
        out_specs = [_rows(tm, qk_w), _rows(tm, v_w)]
        out_shape = [jax.ShapeDtypeStruct((t, qk_w), BF), jax.ShapeDtypeStruct((t, v_w), BF)]
    return pl.pallas_call(
        functools.partial(_proj_rt_kernel, rope=rope),
        grid=(t // tm,),
        in_specs=in_specs,
        out_specs=out_specs,
        out_shape=out_shape,
        compiler_params=_params(1),
        name="proj_rt",
    )(*args)


def _ret_kernel(logit_ref, kc_ref, vc_ref, ql_ref, kl_ref, vl_ref, g_ref, gn_ref, o_ref,
                s_ref, acc_ref, *, chunk):
    hd = pl.program_id(1)
    ctx_chunk = min(chunk, kc_ref.shape[0])
    n_ctx = kc_ref.shape[0] // ctx_chunk
    n_lat = ql_ref.shape[0] // chunk
    row = lax.broadcasted_iota(jnp.int32, (chunk, chunk), 0)
    col = lax.broadcasted_iota(jnp.int32, (chunk, chunk), 1)

    def log_gamma(direction, shape):
        return jnp.log(jax.nn.sigmoid(jnp.full(shape, logit_ref[direction, hd], F32)))

    def decay(direction, size, power):
        pos = lax.broadcasted_iota(jnp.int32, (size, 1), 0).astype(F32)
        return jnp.exp(power(pos) * log_gamma(direction, (size, 1)))

    def write_decays(direction, size):
        d_write = decay(direction, size, (lambda p: size - 1.0 - p) if direction == 0 else (lambda p: p))
        return d_write, jnp.exp(size * log_gamma(direction, (1, 1)))

    intra, read = [], []
    for direction in (0, 1):
        lag = (row - col) if direction == 0 else (col - row)
        intra.append(jnp.where(lag >= 0, jnp.exp(jnp.maximum(lag, 0).astype(F32)
                                                 * log_gamma(direction, (chunk, 1))), 0.0))
        read.append(decay(direction, chunk, (lambda p: p + 1.0) if direction == 0 else (lambda p: chunk - p)))
    lat_write = [write_decays(direction, chunk) for direction in (0, 1)]
    ctx_write = [write_decays(direction, ctx_chunk) for direction in (0, 1)]

    s_ref[...] = jnp.zeros_like(s_ref)

    def absorb(direction, k_ref, v_ref, rows, decays):
        d_write, d_block = decays[direction]
        kw = (k_ref[rows, :].astype(F32) * d_write).astype(BF)
        s_ref[direction] = d_block * s_ref[direction] + lax.dot_general(
            kw, v_ref[rows, :], TN, preferred_element_type=F32)

    for direction in (0, 1):
        for c in (range(n_ctx) if direction == 0 else reversed(range(n_ctx))):
            absorb(direction, kc_ref, vc_ref, pl.ds(c * ctx_chunk, ctx_chunk), ctx_write)

    def chunk_rows(c):
        return pl.ds(pl.multiple_of(c * chunk, chunk), chunk)

    def scan_step(t):
        for direction in (0, 1):
            rows = chunk_rows(t if direction == 0 else n_lat - 1 - t)
            qc = ql_ref[rows, :]
            scores = (lax.dot_general(qc, kl_ref[rows, :], NT, preferred_element_type=F32)
                      * intra[direction])
            acc_ref[direction, rows, :] = (_dot(scores.astype(BF), vl_ref[rows, :])
                                           + _dot(qc, s_ref[direction].astype(BF)) * read[direction])
            absorb(direction, kl_ref, vl_ref, rows, lat_write)

    def finish(c):
        rows = chunk_rows(c)
        o = acc_ref[0, rows, :] + acc_ref[1, rows, :]
        mu = jnp.mean(o, -1, keepdims=True)
        dlt = o - mu
        var = jnp.mean(dlt * dlt, -1, keepdims=True)
        o_ref[rows, :] = (g_ref[rows, :].astype(F32)
                          * (dlt * lax.rsqrt(var + LN_EPS) * gn_ref[...])).astype(BF)

    def first_half(t, carry):
        scan_step(t)
        return carry

    def second_half(t, carry):
        scan_step(t)
        finish(t)
        finish(n_lat - 1 - t)
        return carry

    assert n_lat % 2 == 0
    lax.fori_loop(0, n_lat // 2, first_half, 0)
    lax.fori_loop(n_lat // 2, n_lat, second_half, 0)


def _retention(kc, vc, ql, kl, vl, g, decay_logit, gn_g):
    b, n, _ = ql.shape
    nc = kc.shape[1]
    qk = lambda m: pl.BlockSpec((None, m, RT_QK), lambda bi, h: (bi, 0, h))
    vv = lambda m: pl.BlockSpec((None, m, RT_V), lambda bi, h: (bi, 0, h))
    return pl.pallas_call(
        functools.partial(_ret_kernel, chunk=RT_CHUNK),
        grid=(b, RT_HEADS),
        in_specs=[pl.BlockSpec(memory_space=pltpu.SMEM),
                  qk(nc), vv(nc), qk(n), qk(n), vv(n), vv(n),
                  pl.BlockSpec((None, 1, RT_V), lambda bi, h: (0, 0, h))],
        out_specs=vv(n),
        out_shape=jax.ShapeDtypeStruct((b, n, RT_HEADS * RT_V), BF),
        scratch_shapes=[pltpu.VMEM((2, RT_QK, RT_V), F32), pltpu.VMEM((2, n, RT_V), F32)],
        compiler_params=_params(2),
        name="retention",
    )(decay_logit, kc, vc, ql, kl, vl, g, gn_g)


def _sc_ffn_kernel(xp_ref, x_ref, xn_ref, mod_ref, wi_ref, cw_ref, wo_ref, g1_ref, b1_ref,
                   fwi_ref, fwo_ref, g2_ref, b2_ref, o_ref, act_ref, *, tm, rpb):
    x = x_ref[...]
    xe = jnp.concatenate([xp_ref[...], x, xn_ref[...]], axis=0)
    h = _modulated(xe, mod_ref, 3)
    cu = _dot(h, wi_ref[:, D:2 * D]) * _dot(h, wi_ref[:, 2 * D:3 * D])
    b_gate = _dot(h[HALO:HALO + tm, :], wi_ref[:, 0:D])
    conv = _dwconv3_rows(_zero_outside(cu, tm, rpb), cw_ref[...])
    y = _dot((b_gate * conv).astype(BF), wo_ref[...])
    x1 = _layer_norm(ALPHA * x + mod_ref[5:6, :] * y, g1_ref[...], b1_ref[...])
    _ffn_slabs(lambda p: x1[p * FFN_SLAB:(p + 1) * FFN_SLAB, :], tm // FFN_SLAB, mod_ref, 6,
               fwi_ref, fwo_ref, g2_ref, b2_ref, o_ref, act_ref)


def _short_conv_ffn(x2, mods, layer, sel, wi, conv_w, wo, fwi, fwo, lng, lnb, seq, tm):
    t = x2.shape[0]
    return pl.pallas_call(
        functools.partial(_sc_ffn_kernel, tm=tm, rpb=seq // tm),
        grid=(t // tm,),
        in_specs=_halo_specs(tm, t) + [
            _mod_spec(layer, sel), _layer_spec((D, 3 * D), 0), _layer_spec((3, D), 0),
            _layer_spec((D, D), 0), _layer_spec((1, D), 3 * layer + 1),
            _layer_spec((1, D), 3 * layer + 1)] + _ffn_specs(layer, 1),
        out_specs=_rows(tm, D),
        out_shape=jax.ShapeDtypeStruct((t, D), F32),
        scratch_shapes=[pltpu.VMEM((tm, D_FF), BF)],
        compiler_params=_params(1),
        name="short_conv_ffn",
    )(x2, x2, x2, mods, wi, conv_w, wo, lng, lnb, fwi, fwo, lng, lnb)


def kernel(x, c, ctx, c_ctx, ada_w, ada_b, ln_g, ln_b, ffa_wi, ffa_wo, ffb_wi, ffb_wo, da_w_qkv, da_w_o, da_lambda, da_subln_g, hy_w_in, hy_conv_w, hy_conv_b, hy_fw1, hy_fb1, hy_ff1, hy_fw2, hy_fb2, hy_ff2, hy_fw3, hy_d_skip, hy_w_o, rt_w_in, rt_decay_logit, rt_gn_g, rt_w_o, sc_w_in, sc_conv_w, sc_w_o):
    bsz, seq, _ = x.shape
    n_ctx = ctx.shape[1]
    assert x.shape[2] == D and ada_w.shape[0] == DEPTH and seq % GRID_W == 0
    tm = min(512, seq)
    tmc = min(512, n_ctx)
    assert seq % tm == 0 and n_ctx % tmc == 0 and seq % RT_CHUNK == 0 and n_ctx % min(RT_CHUNK, n_ctx) == 0

    n_rows = -(-(bsz + 1) // HALO) * HALO
    cond = jnp.zeros((n_rows, D), F32).at[:bsz].set(c).at[bsz].set(c_ctx)
    mods = _modulation_all(cond, ada_w, ada_b)

    rpb = seq // tm
    lat = (lambda i: i // rpb, seq, tm)
    cx = (lambda i: bsz, n_ctx, tmc)

    xl = x.reshape(bsz * seq, D)
    xc = ctx.reshape(bsz * n_ctx, D)
    lng = ln_g.reshape(DEPTH * 3, 1, D)
    lnb = ln_b.reshape(DEPTH * 3, 1, D)
    ffw = {(0, 0): (ffa_wi[0].astype(BF), ffa_wo[0].astype(BF))}
    f32_stacks = {0: (ffa_wi, ffa_wo), 1: (ffb_wi, ffb_wo)}
    mixer_jobs = {(0, 0): [("da_qkv", da_w_qkv), ("da_o", da_w_o)],
                  (0, 1): [("hy_in", hy_w_in), ("hy_o", hy_w_o)],
                  (1, 1): [("rt_in", rt_w_in), ("rt_o", rt_w_o)],
                  (2, 1): [("sc_in", sc_w_in), ("sc_o", sc_w_o)]}
    mixw = {}

    def jobs_for(i, which, stream):
        if stream is not lat:
            return [], None
        nxt = (i, 1) if which == 0 else (i + 1, 0)
        jobs = [(s, nxt[0]) for s in f32_stacks[nxt[1]]] if nxt[0] < DEPTH else []
        return jobs + [(stack, 0) for _, stack in mixer_jobs.get((i, which), [])], nxt

    def keep_casts(i, which, nxt, casts):
        if nxt is None:
            return
        if nxt[0] < DEPTH:
            ffw[nxt], casts = (casts[0], casts[1]), casts[2:]
        for (name, _), w in zip(mixer_jobs.get((i, which), []), casts):
            mixw[name] = w[None]

    def row_tile(stream, want):
        rows = seq if stream is lat else bsz * n_ctx
        tile = min(want, rows)
        assert rows % tile == 0
        sel = (lambda j: j // (seq // tile)) if stream is lat else stream[0]
        return tile, sel

    def ffn_a(xx, i, stream):
        tile, sel = row_tile(stream, FFN_TILE)
        jobs, nxt = jobs_for(i, 0, stream)
        out, casts = _half_ffn(xx, mods, i, 0, sel, ffw[(i, 0)], lng, lnb, tile, jobs)
        keep_casts(i, 0, nxt, casts)
        return out

    def mix_out(a, w_o, xx, i, stream):
        tile, sel = row_tile(stream, tm)
        jobs, nxt = jobs_for(i, 1, stream)
        out, casts = _outproj_ffn(a, w_o, xx, mods, i, sel, ffw[(i, 1)], lng, lnb, tile, jobs)
        keep_casts(i, 1, nxt, casts)
        return out

    def by_sample(a, n):
        return a.reshape(bsz, n, a.shape[-1])

    i = 0
    xl, xc = ffn_a(xl, i, lat), ffn_a(xc, i, cx)
    w_qkv = mixw["da_qkv"]
    ql, kl, vtl = _proj_da(xl, mods, i, lat[0], w_qkv, _da_rope_tables(seq), seq, tm)
    qc, kc, vtc = _proj_da(xc, mods, i, cx[0], w_qkv, None, n_ctx, tmc)
    kl, kc = by_sample(kl, seq), by_sample(kc, n_ctx)
    lam_init = 0.8 - 0.6 * math.exp(-0.3 * i)
    subln = da_subln_g.reshape(-1, 1, LANES)
    ol = _diff_attn(ql, [(kc, vtc), (kl, vtl)], da_lambda, subln, lam_init, min(ATTN_TILE, seq))
    oc = _diff_attn(qc, [(kc, vtc)], da_lambda, subln, lam_init, min(ATTN_TILE, n_ctx))
    w_o = mixw["da_o"]
    xl = mix_out(ol.reshape(bsz * seq, D), w_o, xl, i, lat)
    xc = mix_out(oc.reshape(bsz * n_ctx, D), w_o, xc, i, cx)

    i = 1
    xl, xc = ffn_a(xl, i, lat), ffn_a(xc, i, cx)
    w_in, w_o = mixw["hy_in"], mixw["hy_o"]
    conv_b = hy_conv_b.reshape(-1, 1, 3 * D)
    d_skip = hy_d_skip.reshape(-1, 1, D)
    tc = HY_TILE

    def hyena(xx, stream):
        sel, n, tile = stream
        tabs = _dft_tables(n)
        spectra = _hy_filter(n, tabs, hy_fw1[0], hy_fb1[0], hy_ff1[0], hy_fw2[0], hy_fb2[0],
                             hy_ff2[0], hy_fw3[0], tc)
        x0, z = _proj_hy(xx, mods, i, sel, w_in, hy_conv_w, conv_b, n, tile)
        y = _hy_conv(by_sample(z, n), by_sample(x0, n), spectra, d_skip, tabs, tc)
        return mix_out(y.reshape(bsz * n, D), w_o, xx, i, stream)

    xl, xc = hyena(xl, lat), hyena(xc, cx)

    i = 2
    xl, xc = ffn_a(xl, i, lat), ffn_a(xc, i, cx)
    w_in = mixw["rt_in"]
    ql, kl, vl, gl = [by_sample(a, seq) for a in
                      _proj_rt(xl, mods, i, lat[0], w_in, _rt_rope_tables(seq), seq, tm)]
    kc, vc = [by_sample(a, n_ctx) for a in _proj_rt(xc, mods, i, cx[0], w_in, None, n_ctx, tmc)]
    o = _retention(kc, vc, ql, kl, vl, gl, rt_decay_logit[0], rt_gn_g.reshape(-1, 1, RT_HEADS * RT_V))
    xl = mix_out(o.reshape(bsz * seq, RT_HEADS * RT_V), mixw["rt_o"], xl, i, lat)

    i = 3
    xl = ffn_a(xl, i, lat)
    xl = _short_conv_ffn(xl, mods, i, lat[0], mixw["sc_in"], sc_conv_w, mixw["sc_o"],
                         ffw[(i, 1)][0], ffw[(i, 1)][1], lng, lnb, seq, tm)
    return xl.reshape(bsz, seq, D)
```

```python
import functools
import math

import jax
import jax.numpy as jnp
import numpy as np
from jax import lax
from jax.experimental import pallas as pl
from jax.experimental.pallas import tpu as pltpu

D = 1024
N_MOD = 9
D_FF = 2816
LN_EPS = 1e-5
ROPE_BASE = 10000.0
GRID_W = 64
DEPTH = 4
ALPHA = (2.0 * DEPTH) ** 0.25
DA_HEADS = 8
DA_HEAD_DIM = 64
RT_HEADS = 4
RT_QK = 256
RT_V = 512
HY_EMB = 33
HY_FH = 64
LANES = 128
HALO = 8
FF_CHUNK = 256
RT_CHUNK = 256
FFN_TILE = 1024
FFN_SLAB = 256
RIDER_BLOCKS = 16
ATTN_TILE = 2048
ATTN_PART = 512
MOD_TILE = 2304
HY_TILE = 256
V7X_VMEM_BYTES = 64 * 1024 * 1024
VMEM_LIMIT = V7X_VMEM_BYTES - 8 * 1024 * 1024
LOG2E = math.log2(math.e)

F32 = jnp.float32
BF = jnp.bfloat16
NT = (((1,), (1,)), ((), ()))
TN = (((0,), (0,)), ((), ()))


def _dot(a, b):
    return jnp.dot(a, b, preferred_element_type=F32)


def _resident(shape):
    nd = len(shape)
    return pl.BlockSpec(shape, lambda *_: (0,) * nd, pipeline_mode=pl.Buffered(1))


def _layer_spec(shape, layer):
    nd = len(shape)
    return pl.BlockSpec((None,) + tuple(shape), lambda *_: (layer,) + (0,) * nd,
                        pipeline_mode=pl.Buffered(1))


def _mod_spec(layer, sel):
    return pl.BlockSpec((None, None, N_MOD, D), lambda i: (layer, sel(i), 0, 0))


def _rows(tm, width):
    return pl.BlockSpec((tm, width), lambda i: (i, 0))


def _params(n_axes, semantics="parallel"):
    return pltpu.CompilerParams(dimension_semantics=(semantics,) * n_axes,
                                vmem_limit_bytes=VMEM_LIMIT)


def _layer_norm(r, g, b):
    mu = jnp.mean(r, -1, keepdims=True)
    d = r - mu
    var = jnp.mean(d * d, -1, keepdims=True)
    return d * lax.rsqrt(var + LN_EPS) * g + b


def _silu(a):
    return a * jax.nn.sigmoid(a)


def _modulated(x, mod_ref, j):
    return (x * (1.0 + mod_ref[j + 1:j + 2, :]) + mod_ref[j:j + 1, :]).astype(BF)


def _mod_kernel(c_ref, w_ref, b_ref, o_ref):
    s = _silu(c_ref[...]).astype(BF)
    o_ref[...] = _dot(s, w_ref[...].astype(BF)) + b_ref[...]


def _modulation_all(cond, ada_w, ada_b):
    r = cond.shape[0]
    tn = MOD_TILE
    n = N_MOD * D
    out = pl.pallas_call(
        _mod_kernel,
        grid=(DEPTH, n // tn),
        in_specs=[pl.BlockSpec((r, D), lambda i, j: (0, 0)),
                  pl.BlockSpec((None, D, tn), lambda i, j: (i, 0, j)),
                  pl.BlockSpec((None, 1, tn), lambda i, j: (i, 0, j))],
        out_specs=pl.BlockSpec((None, r, tn), lambda i, j: (i, 0, j)),
        out_shape=jax.ShapeDtypeStruct((DEPTH, r, n), F32),
        compiler_params=_params(2),
        name="modulation",
    )(cond, ada_w, ada_b.reshape(DEPTH, 1, n))
    return out.reshape(DEPTH, r, N_MOD, D)


def _ffn_slabs(x_of, n_slabs, mod_ref, j0, wi_ref, wo_ref, g_ref, b_ref, o_ref, act_ref):
    rows = o_ref.shape[0] // n_slabs
    xs = {}

    def up(p):
        xs[p] = x_of(p)
        h = _modulated(xs[p], mod_ref, j0)
        for c in range(D_FF // FF_CHUNK):
            lo = c * FF_CHUNK
            a = _dot(h, wi_ref[:, lo:lo + FF_CHUNK])
            u = _dot(h, wi_ref[:, D_FF + lo:D_FF + lo + FF_CHUNK])
            act_ref[p * rows:(p + 1) * rows, lo:lo + FF_CHUNK] = (_silu(a) * u).astype(BF)

    def down(p):
        y = _dot(act_ref[p * rows:(p + 1) * rows, :], wo_ref[...])
        r = ALPHA * xs.pop(p) + (0.5 * mod_ref[j0 + 2:j0 + 3, :]) * y
        o_ref[p * rows:(p + 1) * rows, :] = _layer_norm(r, g_ref[...], b_ref[...])

    for p in range(n_slabs):
        up(p)
        if p > 0:
            down(p - 1)
    down(n_slabs - 1)


def _ffn_specs(layer, which):
    return [_resident((D, 2 * D_FF)), _resident((D_FF, D)),
            _layer_spec((1, D), 3 * layer + 2 * which), _layer_spec((1, D), 3 * layer + 2 * which)]


def _rider_specs(jobs, steps):
    blocks = min(RIDER_BLOCKS, steps)
    reps = steps // blocks
    assert steps % blocks == 0
    in_specs, out_specs, out_shape = [], [], []
    for stack, layer in jobs:
        _, r, c = stack.shape
        in_specs.append(pl.BlockSpec((None, r // blocks, c), lambda i, layer=layer: (layer, i // reps, 0)))
        out_specs.append(pl.BlockSpec((r // blocks, c), lambda i: (i // reps, 0)))
        out_shape.append(jax.ShapeDtypeStruct((r, c), BF))
    return in_specs, out_specs, out_shape


def _split_rider(rest, n_jobs):
    o_ref, act_ref = rest[n_jobs], rest[-1]
    for src_ref, dst_ref in zip(rest[:n_jobs], rest[n_jobs + 1:-1]):
        dst_ref[...] = src_ref[...].astype(BF)
    return o_ref, act_ref


def _ffn_kernel(x_ref, mod_ref, wi_ref, wo_ref, g_ref, b_ref, *rest, j0, n_slabs, n_jobs):
    o_ref, act_ref = _split_rider(rest, n_jobs)
    rows = x_ref.shape[0] // n_slabs
    _ffn_slabs(lambda p: x_ref[p * rows:(p + 1) * rows, :], n_slabs, mod_ref, j0, wi_ref, wo_ref,
               g_ref, b_ref, o_ref, act_ref)


def _call_with_rider(kernel_fn, name, t, tm, in_specs, args, jobs):
    r_in, r_out, r_shape = _rider_specs(jobs, t // tm)
    outs = pl.pallas_call(
        functools.partial(kernel_fn, n_jobs=len(jobs)),
        grid=(t // tm,),
        in_specs=in_specs + r_in,
        out_specs=[_rows(tm, D)] + r_out,
        out_shape=[jax.ShapeDtypeStruct((t, D), F32)] + r_shape,
        scratch_shapes=[pltpu.VMEM((tm, D_FF), BF)],
        compiler_params=_params(1, "arbitrary" if jobs else "parallel"),
        name=name,
    )(*args, *[stack for stack, _ in jobs])
    return outs[0], list(outs[1:])


def _half_ffn(x2, mods, layer, which, sel, weights, lng, lnb, tm, rider):
    return _call_with_rider(
        functools.partial(_ffn_kernel, j0=6 * which, n_slabs=tm // FFN_SLAB), "half_ffn", x2.shape[0], tm,
        [_rows(tm, D), _mod_spec(layer, sel)] + _ffn_specs(layer, which),
        [x2, mods, weights[0], weights[1], lng, lnb], rider)


def _outproj_ffn_kernel(a_ref, w_ref, x_ref, mod_ref, g1_ref, b1_ref, wi_ref, wo_ref, g2_ref, b2_ref,
                        *rest, n_jobs):
    o_ref, act_ref = _split_rider(rest, n_jobs)
    n_slabs = x_ref.shape[0] // FFN_SLAB

    def mixed(p):
        rows = slice(p * FFN_SLAB, (p + 1) * FFN_SLAB)
        y = _dot(a_ref[rows, :], w_ref[...])
        return _layer_norm(ALPHA * x_ref[rows, :] + mod_ref[5:6, :] * y, g1_ref[...], b1_ref[...])

    _ffn_slabs(mixed, n_slabs, mod_ref, 6, wi_ref, wo_ref, g2_ref, b2_ref, o_ref, act_ref)


def _outproj_ffn(a2, w, x2, mods, layer, sel, weights, lng, lnb, tm, rider):
    t, k = a2.shape
    return _call_with_rider(
        _outproj_ffn_kernel, "outproj_ffn", t, tm,
        [_rows(tm, k), _layer_spec((k, D), 0), _rows(tm, D), _mod_spec(layer, sel),
         _layer_spec((1, D), 3 * layer + 1), _layer_spec((1, D), 3 * layer + 1)] + _ffn_specs(layer, 1),
        [a2, w, x2, mods, lng, lnb, weights[0], weights[1], lng, lnb], rider)


def _rope_angles(n_tokens, dim):
    rows = n_tokens // GRID_W
    row = np.repeat(np.arange(rows), GRID_W).astype(np.float32)
    col = np.tile(np.arange(GRID_W), rows).astype(np.float32)
    n_freq = dim // 4
    inv = (ROPE_BASE ** (-np.arange(n_freq, dtype=np.float32) / n_freq)).astype(np.float32)
    return np.concatenate([row[:, None] * inv, col[:, None] * inv], axis=-1)


def _da_rope_tables(n_tokens):
    ang = _rope_angles(n_tokens, DA_HEAD_DIM)
    cos, sin = np.cos(ang), np.sin(ang)
    cos128 = np.tile(cos, (1, 4))
    sin128 = np.tile(np.concatenate([-sin, sin], axis=-1), (1, 2))
    return jnp.asarray(cos128, F32), jnp.asarray(sin128, F32)


def _rt_rope_tables(n_tokens):
    ang = _rope_angles(n_tokens, RT_QK)
    return jnp.asarray(np.cos(ang), F32), jnp.asarray(np.sin(ang), F32)


def _proj_da_kernel(*refs, rope):
    if rope:
        x_ref, mod_ref, w_ref, cos_ref, sin_ref, q_ref, k_ref, v_ref = refs
        cos, sin = cos_ref[...], sin_ref[...]
        lane = lax.broadcasted_iota(jnp.int32, (1, LANES), 1)
        first_half = (lane % DA_HEAD_DIM) < (DA_HEAD_DIM // 2)
    else:
        x_ref, mod_ref, w_ref, q_ref, k_ref, v_ref = refs
    h = _modulated(x_ref[...], mod_ref, 3)

    def rot(y):
        if not rope:
            return y
        partner = jnp.where(first_half, pltpu.roll(y, LANES - 32, 1), pltpu.roll(y, 32, 1))
        return y * cos + partner * sin

    q = _dot(h, w_ref[:, 0:D]) * (DA_HEAD_DIM ** -0.5 * LOG2E)
    for j in range(D // LANES):
        q_ref[j * LANES:(j + 1) * LANES, :] = rot(q[:, j * LANES:(j + 1) * LANES]).T.astype(BF)
    k = _dot(h, w_ref[:, D:2 * D])
    for j in range(D // LANES):
        k_ref[:, j * LANES:(j + 1) * LANES] = rot(k[:, j * LANES:(j + 1) * LANES]).astype(BF)
    v_ref[...] = _dot(h, w_ref[:, 2 * D:3 * D]).T.astype(BF)


def _proj_da(x2, mods, layer, sel, w, rope_tabs, seq, tm):
    t = x2.shape[0]
    rpb = seq // tm
    rope = rope_tabs is not None
    in_specs = [_rows(tm, D), _mod_spec(layer, sel), _layer_spec((D, 3 * D), 0)]
    args = [x2, mods, w]
    if rope:
        in_specs += [pl.BlockSpec((tm, LANES), lambda i: (i % rpb, 0))] * 2
        args += list(rope_tabs)
    return pl.pallas_call(
        functools.partial(_proj_da_kernel, rope=rope),
        grid=(t // tm,),
        in_specs=in_specs,
        out_specs=[pl.BlockSpec((None, D, tm), lambda i: (i // rpb, 0, i % rpb)), _rows(tm, D),
                   pl.BlockSpec((None, D, tm), lambda i: (i // rpb, 0, i % rpb))],
        out_shape=[jax.ShapeDtypeStruct((t // seq, D, seq), BF), jax.ShapeDtypeStruct((t, D), BF),
                   jax.ShapeDtypeStruct((t // seq, D, seq), BF)],
        compiler_params=_params(1),
        name="proj_da",
    )(*args)


def _attn_kernel(*refs, lam_init, n_kv, parts):
    lam_ref, g_ref, q_ref = refs[:3]
    kv_refs = [(refs[3 + 2 * j], refs[4 + 2 * j]) for j in range(n_kv)]
    o_ref = refs[3 + 2 * n_kv]
    width = q_ref.shape[1] // parts
    feature = lax.broadcasted_iota(jnp.int32, (LANES, width), 0)

    def scores(part, head_map):
        q = q_ref[:, part * width:(part + 1) * width]
        keep = (feature < DA_HEAD_DIM) if head_map == 0 else (feature >= DA_HEAD_DIM)
        qm = jnp.where(keep, q, jnp.zeros_like(q))
        return [_dot(k_ref[...], qm) for k_ref, _ in kv_refs]

    def attend(ss):
        m = functools.reduce(jnp.maximum, [jnp.max(s, 0, keepdims=True) for s in ss])
        es = [jnp.exp2(s - m) for s in ss]
        den = functools.reduce(jnp.add, [jnp.sum(e, 0, keepdims=True) for e in es])
        o = functools.reduce(jnp.add, [_dot(vt_ref[...], e.astype(BF))
                                       for e, (_, vt_ref) in zip(es, kv_refs)])
        return o / den

    chains = [(part, head_map) for part in range(parts) for head_map in range(2)]
    result, pending = {}, None
    for chain in chains:
        ss = scores(*chain)
        if pending is not None:
            result[pending[0]] = attend(pending[1])
        pending = (chain, ss)
    result[pending[0]] = attend(pending[1])

    lam = lam_ref[...]
    lam_full = (jnp.exp(jnp.sum(lam[0:1] * lam[1:2], -1, keepdims=True))
                - jnp.exp(jnp.sum(lam[2:3] * lam[3:4], -1, keepdims=True)) + lam_init)
    for part in range(parts):
        o = result[(part, 0)] - lam_full * result[(part, 1)]
        o = o * lax.rsqrt(jnp.mean(o * o, 0, keepdims=True) + LN_EPS)
        o_ref[part * width:(part + 1) * width, :] = (o.T * g_ref[...] * (1.0 - lam_init)).astype(BF)


def _diff_attn(q, kvs, lam, subln_g, lam_init, tq):
    b, _, n = q.shape
    in_specs = [_layer_spec((4, DA_HEAD_DIM), 0), _layer_spec((1, LANES), 0),
                pl.BlockSpec((None, LANES, tq), lambda bi, h, i: (bi, h, i))]
    args = [lam, subln_g, q]
    for k, vt in kvs:
        n_k = k.shape[1]
        in_specs += [pl.BlockSpec((None, n_k, LANES), lambda bi, h, i: (bi, 0, h)),
                     pl.BlockSpec((None, LANES, n_k), lambda bi, h, i: (bi, h, 0))]
        args += [k, vt]
    return pl.pallas_call(
        functools.partial(_attn_kernel, lam_init=lam_init, n_kv=len(kvs),
                          parts=max(1, tq // ATTN_PART)),
        grid=(b, DA_HEADS, n // tq),
        in_specs=in_specs,
        out_specs=pl.BlockSpec((None, tq, LANES), lambda bi, h, i: (bi, i, h)),
        out_shape=jax.ShapeDtypeStruct((b, n, D), BF),
        compiler_params=_params(3),
        name="diff_attn",
    )(*args)


def _halo_specs(tm, t):
    blocks_per_tile = tm // HALO
    last = t // HALO - 1
    return [pl.BlockSpec((HALO, D), lambda i: (jnp.maximum(i * blocks_per_tile - 1, 0), 0)),
            _rows(tm, D),
            pl.BlockSpec((HALO, D), lambda i: (jnp.minimum((i + 1) * blocks_per_tile, last), 0))]


def _zero_outside(u, tm, rpb):
    pos = pl.program_id(0) % rpb
    keep_prev = jnp.where(pos == 0, 0.0, 1.0)
    keep_next = jnp.where(pos == rpb - 1, 0.0, 1.0)
    return jnp.concatenate([u[:HALO] * keep_prev, u[HALO:HALO + tm], u[HALO + tm:] * keep_next], axis=0)


def _dwconv3_rows(u, w):
    n = u.shape[0]
    y = pltpu.roll(u, 1, 0) * w[0:1, :] + u * w[1:2, :] + pltpu.roll(u, n - 1, 0) * w[2:3, :]
    return y[HALO:n - HALO, :]


def _proj_hy_kernel(xp_ref, x_ref, xn_ref, mod_ref, w_ref, cw_ref, cb_ref, x0_ref, z_ref, *, tm, rpb):
    xe = jnp.concatenate([xp_ref[...], x_ref[...], xn_ref[...]], axis=0)
    h = _modulated(xe, mod_ref, 3)

    def project(j):
        return _dot(h, w_ref[:, j * D:(j + 1) * D])

    def conv(u, j):
        u = _zero_outside(u, tm, rpb)
        return _dwconv3_rows(u, cw_ref[:, j * D:(j + 1) * D]) + cb_ref[:, j * D:(j + 1) * D]

    u0 = project(0)
    u1 = project(1)
    x0_ref[...] = conv(u0, 0)
    u2 = project(2)
    x1 = conv(u1, 1)
    z_ref[...] = x1 * conv(u2, 2)


def _proj_hy(x2, mods, layer, sel, w, conv_w, conv_b, seq, tm):
    t = x2.shape[0]
    return pl.pallas_call(
        functools.partial(_proj_hy_kernel, tm=tm, rpb=seq // tm),
        grid=(t // tm,),
        in_specs=_halo_specs(tm, t) + [
            _mod_spec(layer, sel), _layer_spec((D, 3 * D), 0), _layer_spec((3, 3 * D), 0),
            _layer_spec((1, 3 * D), 0)],
        out_specs=[_rows(tm, D)] * 2,
        out_shape=[jax.ShapeDtypeStruct((t, D), F32)] * 2,
        compiler_params=_params(1),
        name="proj_hy",
    )(x2, x2, x2, mods, w, conv_w, conv_b)


def _dft_tables(n):
    m = n // 2
    k = np.arange(m + HALO, dtype=np.int64)[:, None]
    valid = k <= m
    s_even = 2 * np.arange(m, dtype=np.int64)[None, :]

    def table(fn, offset):
        phase = (k * (s_even + offset)) % (2 * n)
        return np.where(valid, fn(phase.astype(np.float64) * (math.pi / n)), 0.0).astype(np.float32)

    ce, se, co, so = table(np.cos, 0), table(np.sin, 0), table(np.cos, 1), table(np.sin, 1)
    as_bf = lambda a: jnp.asarray(np.ascontiguousarray(a)).astype(BF)
    fwd = (as_bf(np.concatenate([ce, se], 0)), as_bf(np.concatenate([co, so], 0)))
    inv = tuple(as_bf(t[:m].T) for t in (ce, se, co, so))
    return fwd, inv


def _parity_split(v, split_ref):
    n, width = v.shape
    slabs = width // LANES
    for c in range(slabs):
        split_ref[c] = v[:, c * LANES:(c + 1) * LANES]
    take = lambda start: jnp.concatenate(
        [split_ref[c, pl.ds(start, n // 2, stride=2), :] for c in range(slabs)], axis=1)
    return take(0), take(1)


def _parity_merge(even, odd, split_ref):
    m, width = even.shape
    slabs = width // LANES
    for c in range(slabs):
        split_ref[c, pl.ds(0, m, stride=2), :] = even[:, c * LANES:(c + 1) * LANES]
        split_ref[c, pl.ds(1, m, stride=2), :] = odd[:, c * LANES:(c + 1) * LANES]
    return jnp.concatenate([split_ref[c] for c in range(slabs)], axis=1)


def _hy_features(n):
    t = np.linspace(0.0, 1.0, n, dtype=np.float32)[:, None]
    bands = (HY_EMB - 1) // 2
    fr = np.linspace(1e-4, bands - 1, bands, dtype=np.float32)[None, :]
    w = (2.0 * math.pi * np.arange(n, dtype=np.float32)[:, None] / n).astype(np.float32)
    z = np.concatenate([t, np.cos(fr * w), -np.sin(fr * w)], axis=-1).astype(np.float32)
    return jnp.asarray(np.pad(z, ((0, 0), (0, LANES - HY_EMB))), F32)


def _hy_deltas():
    max_decay = math.log(1e-2) / 0.3
    min_decay = math.log(1e-2) / 1.5
    return jnp.asarray(np.abs(np.linspace(min_decay, max_decay, D, dtype=np.float32))[None, :], F32)


def _hy_filter_kernel(z_ref, w1_ref, b1_ref, f1_ref, w2_ref, b2_ref, f2_ref, w3f_ref, w3b_ref,
                      dl_ref, fe_ref, fo_ref, plo_ref, qlo_ref, phi_ref, qhi_ref, split_ref, hid_ref):
    n = z_ref.shape[0]
    mp = n // 2 + HALO
    z = z_ref[...]

    @pl.when(pl.program_id(0) == 0)
    def _():
        h = jnp.sin(f1_ref[...] * (_dot(z.astype(BF), w1_ref[...].astype(BF)) + b1_ref[...]))
        h = jnp.sin(f2_ref[...] * (_dot(h.astype(BF), w2_ref[...].astype(BF)) + b2_ref[...]))
        hid_ref[...] = h.astype(BF)

    hb16 = hid_ref[...]
    decay = jnp.exp(-z[:, 0:1] * dl_ref[...])
    h_f = _dot(hb16, w3f_ref[...].astype(BF)) * decay
    h_b = _dot(hb16, w3b_ref[...].astype(BF)) * decay
    row = lax.broadcasted_iota(jnp.int32, h_f.shape, 0)
    h_b = jnp.where(row == 0, 0.0, h_b)
    sm_even, sm_odd = _parity_split(h_f + h_b, split_ref)
    a_e = _dot(fe_ref[0:mp, :], sm_even.astype(BF))
    a_o = _dot(fo_ref[0:mp, :], sm_odd.astype(BF))
    df_even, df_odd = _parity_split(h_f - h_b, split_ref)
    b_e = _dot(fe_ref[mp:2 * mp, :], df_even.astype(BF))
    b_o = _dot(fo_ref[mp:2 * mp, :], df_odd.astype(BF))
    frow = lax.broadcasted_iota(jnp.int32, a_e.shape, 0)
    w_lo = jnp.where(frow == 0, 1.0, 2.0) * (0.5 / n)
    w_hi = jnp.where(frow == 0, 1.0, jnp.where(frow < n // 2, 2.0, 0.0)) * (0.5 / n)
    plo_ref[...] = (a_e + a_o) * w_lo
    qlo_ref[...] = -(b_e + b_o) * w_lo
    phi_ref[...] = (a_e - a_o) * w_hi
    qhi_ref[...] = (b_e - b_o) * w_hi


def _hy_filter(n, tabs, fw1, fb1, ff1, fw2, fb2, ff2, fw3, tc):
    (fwd_even, fwd_odd), _ = tabs
    w1 = jnp.pad(fw1, ((0, LANES - HY_EMB), (0, 0)))
    nt = D // tc
    m, mp = n // 2, n // 2 + HALO
    small = lambda s: pl.BlockSpec(s, lambda j: (0, 0))
    return pl.pallas_call(
        _hy_filter_kernel,
        grid=(nt,),
        in_specs=[small((n, LANES)), small((LANES, HY_FH)), small((1, HY_FH)), small((1, HY_FH)),
                  small((HY_FH, HY_FH)), small((1, HY_FH)), small((1, HY_FH)),
                  pl.BlockSpec((HY_FH, tc), lambda j: (0, j)),
                  pl.BlockSpec((HY_FH, tc), lambda j: (0, j + nt)),
                  pl.BlockSpec((1, tc), lambda j: (0, j)),
                  _resident((2 * mp, m)), _resident((2 * mp, m))],
        out_specs=[pl.BlockSpec((mp, tc), lambda j: (0, j))] * 4,
        out_shape=[jax.ShapeDtypeStruct((mp, D), F32)] * 4,
        scratch_shapes=[pltpu.VMEM((tc // LANES, n, LANES), F32), pltpu.VMEM((n, HY_FH), BF)],
        compiler_params=pltpu.CompilerParams(dimension_semantics=("arbitrary",),
                                             vmem_limit_bytes=VMEM_LIMIT),
        name="hy_filter",
    )(_hy_features(n), w1, fb1.reshape(1, -1), ff1.reshape(1, -1), fw2, fb2.reshape(1, -1),
      ff2.reshape(1, -1), fw3, fw3, _hy_deltas(), fwd_even, fwd_odd)


def _hy_conv_kernel(z_ref, x0_ref, plo_ref, qlo_ref, phi_ref, qhi_ref, ds_ref, fe_ref, fo_ref,
                    cet_ref, set_ref, cot_ref, sot_ref, o_ref, split_ref):
    n = z_ref.shape[0]
    m, mp = n // 2, n // 2 + HALO
    z = z_ref[...]
    z_even, z_odd = _parity_split(z, split_ref)
    r_e = _dot(fe_ref[...], z_even.astype(BF))
    r_o = _dot(fo_ref[...], z_odd.astype(BF))
    a_e, b_e, a_o, b_o = r_e[0:mp], r_e[mp:2 * mp], r_o[0:mp], r_o[mp:2 * mp]
    a_lo, b_lo, a_hi, b_hi = a_e + a_o, b_e + b_o, a_e - a_o, b_o - b_e
    p, q = plo_ref[...], qlo_ref[...]
    yr_lo, yi_lo = a_lo * p + b_lo * q, a_lo * q - b_lo * p
    p, q = phi_ref[...], qhi_ref[...]
    yr_hi, yi_hi = a_hi * p + b_hi * q, a_hi * q - b_hi * p
    y_even = (_dot(cet_ref[...], (yr_lo + yr_hi)[0:m].astype(BF))
              - _dot(set_ref[...], (yi_lo - yi_hi)[0:m].astype(BF)))
    y_odd = (_dot(cot_ref[...], (yr_lo - yr_hi)[0:m].astype(BF))
             - _dot(sot_ref[...], (yi_lo + yi_hi)[0:m].astype(BF)))
    half = lax.broadcasted_iota(jnp.int32, y_even.shape, 0)
    sign = (1 - 2 * (half % 2)).astype(F32)
    y_even = y_even + sign * yr_lo[m:m + 1, :]
    y_odd = y_odd - sign * yi_lo[m:m + 1, :]
    y = _parity_merge(y_even, y_odd, split_ref)
    o_ref[...] = (x0_ref[...] * (y + z * ds_ref[...])).astype(BF)


def _hy_conv(z, x0, spectra, d_skip, tabs, tc):
    b, n, _ = z.shape
    (fwd_even, fwd_odd), inv = tabs
    m, mp = n // 2, n // 2 + HALO
    sample = pl.BlockSpec((None, n, tc), lambda j, bi: (bi, 0, j))
    return pl.pallas_call(
        _hy_conv_kernel,
        grid=(D // tc, b),
        in_specs=[sample, sample] + [pl.BlockSpec((mp, tc), lambda j, bi: (0, j))] * 4
                 + [pl.BlockSpec((None, 1, tc), lambda j, bi: (0, 0, j)),
                    _resident((2 * mp, m)), _resident((2 * mp, m))] + [_resident((m, m))] * 4,
        out_specs=sample,
        out_shape=jax.ShapeDtypeStruct((b, n, D), BF),
        scratch_shapes=[pltpu.VMEM((tc // LANES, n, LANES), F32)],
        compiler_params=_params(2),
        name="hy_conv",
    )(z, x0, *spectra, d_skip, fwd_even, fwd_odd, *inv)


def _proj_rt_kernel(*refs, rope):
    if rope:
        x_ref, mod_ref, w_ref, cos_ref, sin_ref, q_ref, k_ref, v_ref, g_ref = refs
        cos, sin = cos_ref[...], sin_ref[...]
    else:
        x_ref, mod_ref, w_ref, k_ref, v_ref = refs
    h = _modulated(x_ref[...], mod_ref, 3)
    qk_w = RT_HEADS * RT_QK
    half = RT_QK // 2

    def store_rot(y, ref):
        for hd in range(RT_HEADS):
            lo = hd * RT_QK
            x1, x2 = y[:, lo:lo + half], y[:, lo + half:lo + RT_QK]
            if rope:
                x1, x2 = x1 * cos - x2 * sin, x1 * sin + x2 * cos
            ref[:, lo:lo + half] = x1.astype(BF)
            ref[:, lo + half:lo + RT_QK] = x2.astype(BF)

    if rope:
        store_rot(_dot(h, w_ref[:, 0:qk_w]), q_ref)
    store_rot(_dot(h, w_ref[:, qk_w:2 * qk_w]) * (RT_QK ** -0.5), k_ref)
    v_w = RT_HEADS * RT_V
    v_ref[...] = _dot(h, w_ref[:, 2 * qk_w:2 * qk_w + v_w]).astype(BF)
    if rope:
        g_ref[...] = _silu(_dot(h, w_ref[:, 2 * qk_w + v_w:2 * qk_w + 2 * v_w])).astype(BF)


def _proj_rt(x2, mods, layer, sel, w, rope_tabs, seq, tm):
    t = x2.shape[0]
    rpb = seq // tm
    rope = rope_tabs is not None
    qk_w, v_w = RT_HEADS * RT_QK, RT_HEADS * RT_V
    in_specs = [_rows(tm, D), _mod_spec(layer, sel), _layer_spec((D, 2 * qk_w + 2 * v_w), 0)]
    args = [x2, mods, w]
    if rope:
        in_specs += [pl.BlockSpec((tm, LANES), lambda i: (i % rpb, 0))] * 2
        args += list(rope_tabs)
        out_specs = [_rows(tm, qk_w), _rows(tm, qk_w), _rows(tm, v_w), _rows(tm, v_w)]
        out_shape = [jax.ShapeDtypeStruct((t, qk_w), BF), jax.ShapeDtypeStruct((t, qk_w), BF),
                     jax.ShapeDtypeStruct((t, v_w), BF), jax.ShapeDtypeStruct((t, v_w), BF)]
    else:
        out_specs = [_rows(tm, qk_w), _rows(tm, v_w)]
        out_shape = [jax.ShapeDtypeStruct((t, qk_w), BF), jax.ShapeDtypeStruct((t, v_w), BF)]
    return pl.pallas_call(
        functools.partial(_proj_rt_kernel, rope=rope),
        grid=(t // tm,),
        in_specs=in_specs,
        out_specs=out_specs,
        out_shape=out_shape,
        compiler_params=_params(1),
        name="proj_rt",
    )(*args)


def _ret_kernel(logit_ref, kc_ref, vc_ref, ql_ref, kl_ref, vl_ref, g_ref, gn_ref, o_ref,
                s_ref, acc_ref, *, chunk):
    hd = pl.program_id(1)
    ctx_chunk = min(chunk, kc_ref.shape[0])
    n_ctx = kc_ref.shape[0] // ctx_chunk
    n_lat = ql_ref.shape[0] // chunk
    row = lax.broadcasted_iota(jnp.int32, (chunk, chunk), 0)
    col = lax.broadcasted_iota(jnp.int32, (chunk, chunk), 1)

    def log_gamma(direction, shape):
        return jnp.log(jax.nn.sigmoid(jnp.full(shape, logit_ref[direction, hd], F32)))

    def decay(direction, size, power):
        pos = lax.broadcasted_iota(jnp.int32, (size, 1), 0).astype(F32)
        return jnp.exp(power(pos) * log_gamma(direction, (size, 1)))

    def write_decays(direction, size):
        d_write = decay(direction, size, (lambda p: size - 1.0 - p) if direction == 0 else (lambda p: p))
        return d_write, jnp.exp(size * log_gamma(direction, (1, 1)))

    intra, read = [], []
    for direction in (0, 1):
        lag = (row - col) if direction == 0 else (col - row)
        intra.append(jnp.where(lag >= 0, jnp.exp(jnp.maximum(lag, 0).astype(F32)
                                                 * log_gamma(direction, (chunk, 1))), 0.0))
        read.append(decay(direction, chunk, (lambda p: p + 1.0) if direction == 0 else (lambda p: chunk - p)))
    lat_write = [write_decays(direction, chunk) for direction in (0, 1)]
    ctx_write = [write_decays(direction, ctx_chunk) for direction in (0, 1)]

    s_ref[...] = jnp.zeros_like(s_ref)

    def absorb(direction, k_ref, v_ref, rows, decays):
        d_write, d_block = decays[direction]
        kw = (k_ref[rows, :].astype(F32) * d_write).astype(BF)
        s_ref[direction] = d_block * s_ref[direction] + lax.dot_general(
            kw, v_ref[rows, :], TN, preferred_element_type=F32)

    for direction in (0, 1):
        for c in (range(n_ctx) if direction == 0 else reversed(range(n_ctx))):
            absorb(direction, kc_ref, vc_ref, pl.ds(c * ctx_chunk, ctx_chunk), ctx_write)

    def chunk_rows(c):
        return pl.ds(pl.multiple_of(c * chunk, chunk), chunk)

    def scan_step(t):
        for direction in (0, 1):
            rows = chunk_rows(t if direction == 0 else n_lat - 1 - t)
            qc = ql_ref[rows, :]
            scores = (lax.dot_general(qc, kl_ref[rows, :], NT, preferred_element_type=F32)
                      * intra[direction])
            acc_ref[direction, rows, :] = (_dot(scores.astype(BF), vl_ref[rows, :])
                                           + _dot(qc, s_ref[direction].astype(BF)) * read[direction])
            absorb(direction, kl_ref, vl_ref, rows, lat_write)

    def finish(c):
        rows = chunk_rows(c)
        o = acc_ref[0, rows, :] + acc_ref[1, rows, :]
        mu = jnp.mean(o, -1, keepdims=True)
        dlt = o - mu
        var = jnp.mean(dlt * dlt, -1, keepdims=True)
        o_ref[rows, :] = (g_ref[rows, :].astype(F32)
                          * (dlt * lax.rsqrt(var + LN_EPS) * gn_ref[...])).astype(BF)

    def first_half(t, carry):
        scan_step(t)
        return carry

    def second_half(t, carry):
        scan_step(t)
        finish(t)
        finish(n_lat - 1 - t)
        return carry

    assert n_lat % 2 == 0
    lax.fori_loop(0, n_lat // 2, first_half, 0, unroll=True)
    lax.fori_loop(n_lat // 2, n_lat, second_half, 0, unroll=True)


def _retention(kc, vc, ql, kl, vl, g, decay_logit, gn_g):
    b, n, _ = ql.shape
    nc = kc.shape[1]
    qk = lambda m: pl.BlockSpec((None, m, RT_QK), lambda bi, h: (bi, 0, h))
    vv = lambda m: pl.BlockSpec((None, m, RT_V), lambda bi, h: (bi, 0, h))
    return pl.pallas_call(
        functools.partial(_ret_kernel, chunk=RT_CHUNK),
        grid=(b, RT_HEADS),
        in_specs=[pl.BlockSpec(memory_space=pltpu.SMEM),
                  qk(nc), vv(nc), qk(n), qk(n), vv(n), vv(n),
                  pl.BlockSpec((None, 1, RT_V), lambda bi, h: (0, 0, h))],
        out_specs=vv(n),
        out_shape=jax.ShapeDtypeStruct((b, n, RT_HEADS * RT_V), BF),
        scratch_shapes=[pltpu.VMEM((2, RT_QK, RT_V), F32), pltpu.VMEM((2, n, RT_V), F32)],
        compiler_params=_params(2),
        name="retention",
    )(decay_logit, kc, vc, ql, kl, vl, g, gn_g)


def _sc_ffn_kernel(xp_ref, x_ref, xn_ref, mod_ref, wi_ref, cw_ref, wo_ref, g1_ref, b1_ref,
                   fwi_ref, fwo_ref, g2_ref, b2_ref, o_ref, act_ref, *, tm, rpb):
    x = x_ref[...]
    xe = jnp.concatenate([xp_ref[...], x, xn_ref[...]], axis=0)
    h = _modulated(xe, mod_ref, 3)
    cu = _dot(h, wi_ref[:, D:2 * D]) * _dot(h, wi_ref[:, 2 * D:3 * D])
    b_gate = _dot(h[HALO:HALO + tm, :], wi_ref[:, 0:D])
    conv = _dwconv3_rows(_zero_outside(cu, tm, rpb), cw_ref[...])
    y = _dot((b_gate * conv).astype(BF), wo_ref[...])
    x1 = _layer_norm(ALPHA * x + mod_ref[5:6, :] * y, g1_ref[...], b1_ref[...])
    _ffn_slabs(lambda p: x1[p * FFN_SLAB:(p + 1) * FFN_SLAB, :], tm // FFN_SLAB, mod_ref, 6,
               fwi_ref, fwo_ref, g2_ref, b2_ref, o_ref, act_ref)


def _short_conv_ffn(x2, mods, layer, sel, wi, conv_w, wo, fwi, fwo, lng, lnb, seq, tm):
    t = x2.shape[0]
    return pl.pallas_call(
        functools.partial(_sc_ffn_kernel, tm=tm, rpb=seq // tm),
        grid=(t // tm,),
        in_specs=_halo_specs(tm, t) + [
            _mod_spec(layer, sel), _layer_spec((D, 3 * D), 0), _layer_spec((3, D), 0),
            _layer_spec((D, D), 0), _layer_spec((1, D), 3 * layer + 1),
            _layer_spec((1, D), 3 * layer + 1)] + _ffn_specs(layer, 1),
        out_specs=_rows(tm, D),
        out_shape=jax.ShapeDtypeStruct((t, D), F32),
        scratch_shapes=[pltpu.VMEM((tm, D_FF), BF)],
        compiler_params=_params(1),
        name="short_conv_ffn",
    )(x2, x2, x2, mods, wi, conv_w, wo, lng, lnb, fwi, fwo, lng, lnb)


def kernel(x, c, ctx, c_ctx, ada_w, ada_b, ln_g, ln_b, ffa_wi, ffa_wo, ffb_wi, ffb_wo, da_w_qkv, da_w_o, da_lambda, da_subln_g, hy_w_in, hy_conv_w, hy_conv_b, hy_fw1, hy_fb1, hy_ff1, hy_fw2, hy_fb2, hy_ff2, hy_fw3, hy_d_skip, hy_w_o, rt_w_in, rt_decay_logit, rt_gn_g, rt_w_o, sc_w_in, sc_conv_w, sc_w_o):
    bsz, seq, _ = x.shape
    n_ctx = ctx.shape[1]
    assert x.shape[2] == D and ada_w.shape[0] == DEPTH and seq % GRID_W == 0
    tm = min(512, seq)
    tmc = min(512, n_ctx)
    assert seq % tm == 0 and n_ctx % tmc == 0 and seq % RT_CHUNK == 0 and n_ctx % min(RT_CHUNK, n_ctx) == 0

    n_rows = -(-(bsz + 1) // HALO) * HALO
    cond = jnp.zeros((n_rows, D), F32).at[:bsz].set(c).at[bsz].set(c_ctx)
    mods = _modulation_all(cond, ada_w, ada_b)

    rpb = seq // tm
    lat = (lambda i: i // rpb, seq, tm)
    cx = (lambda i: bsz, n_ctx, tmc)

    xl = x.reshape(bsz * seq, D)
    xc = ctx.reshape(bsz * n_ctx, D)
    lng = ln_g.reshape(DEPTH * 3, 1, D)
    lnb = ln_b.reshape(DEPTH * 3, 1, D)
    ffw = {(0, 0): (ffa_wi[0].astype(BF), ffa_wo[0].astype(BF))}
    f32_stacks = {0: (ffa_wi, ffa_wo), 1: (ffb_wi, ffb_wo)}
    mixer_jobs = {(0, 0): [("da_qkv", da_w_qkv), ("da_o", da_w_o)],
                  (0, 1): [("hy_in", hy_w_in), ("hy_o", hy_w_o)],
                  (1, 1): [("rt_in", rt_w_in), ("rt_o", rt_w_o)],
                  (2, 1): [("sc_in", sc_w_in), ("sc_o", sc_w_o)]}
    mixw = {}

    def jobs_for(i, which, stream):
        if stream is not lat:
            return [], None
        nxt = (i, 1) if which == 0 else (i + 1, 0)
        jobs = [(s, nxt[0]) for s in f32_stacks[nxt[1]]] if nxt[0] < DEPTH else []
        return jobs + [(stack, 0) for _, stack in mixer_jobs.get((i, which), [])], nxt

    def keep_casts(i, which, nxt, casts):
        if nxt is None:
            return
        if nxt[0] < DEPTH:
            ffw[nxt], casts = (casts[0], casts[1]), casts[2:]
        for (name, _), w in zip(mixer_jobs.get((i, which), []), casts):
            mixw[name] = w[None]

    def row_tile(stream, want):
        rows = seq if stream is lat else bsz * n_ctx
        tile = min(want, rows)
        assert rows % tile == 0
        sel = (lambda j: j // (seq // tile)) if stream is lat else stream[0]
        return tile, sel

    def ffn_a(xx, i, stream):
        tile, sel = row_tile(stream, FFN_TILE)
        jobs, nxt = jobs_for(i, 0, stream)
        out, casts = _half_ffn(xx, mods, i, 0, sel, ffw[(i, 0)], lng, lnb, tile, jobs)
        keep_casts(i, 0, nxt, casts)
        return out

    def mix_out(a, w_o, xx, i, stream):
        tile, sel = row_tile(stream, tm)
        jobs, nxt = jobs_for(i, 1, stream)
        out, casts = _outproj_ffn(a, w_o, xx, mods, i, sel, ffw[(i, 1)], lng, lnb, tile, jobs)
        keep_casts(i, 1, nxt, casts)
        return out

    def by_sample(a, n):
        return a.reshape(bsz, n, a.shape[-1])

    i = 0
    xl, xc = ffn_a(xl, i, lat), ffn_a(xc, i, cx)
    w_qkv = mixw["da_qkv"]
    ql, kl, vtl = _proj_da(xl, mods, i, lat[0], w_qkv, _da_rope_tables(seq), seq, tm)
    qc, kc, vtc = _proj_da(xc, mods, i, cx[0], w_qkv, None, n_ctx, tmc)
    kl, kc = by_sample(kl, seq), by_sample(kc, n_ctx)
    lam_init = 0.8 - 0.6 * math.exp(-0.3 * i)
    subln = da_subln_g.reshape(-1, 1, LANES)
    ol = _diff_attn(ql, [(kc, vtc), (kl, vtl)], da_lambda, subln, lam_init, min(ATTN_TILE, seq))
    oc = _diff_attn(qc, [(kc, vtc)], da_lambda, subln, lam_init, min(ATTN_TILE, n_ctx))
    w_o = mixw["da_o"]
    xl = mix_out(ol.reshape(bsz * seq, D), w_o, xl, i, lat)
    xc = mix_out(oc.reshape(bsz * n_ctx, D), w_o, xc, i, cx)

    i = 1
    xl, xc = ffn_a(xl, i, lat), ffn_a(xc, i, cx)
    w_in, w_o = mixw["hy_in"], mixw["hy_o"]
    conv_b = hy_conv_b.reshape(-1, 1, 3 * D)
    d_skip = hy_d_skip.reshape(-1, 1, D)
    tc = HY_TILE

    def hyena(xx, stream):
        sel, n, tile = stream
        tabs = _dft_tables(n)
        spectra = _hy_filter(n, tabs, hy_fw1[0], hy_fb1[0], hy_ff1[0], hy_fw2[0], hy_fb2[0],
                             hy_ff2[0], hy_fw3[0], tc)
        x0, z = _proj_hy(xx, mods, i, sel, w_in, hy_conv_w, conv_b, n, tile)
        y = _hy_conv(by_sample(z, n), by_sample(x0, n), spectra, d_skip, tabs, tc)
        return mix_out(y.reshape(bsz * n, D), w_o, xx, i, stream)

    xl, xc = hyena(xl, lat), hyena(xc, cx)

    i = 2
    xl, xc = ffn_a(xl, i, lat), ffn_a(xc, i, cx)
    w_in = mixw["rt_in"]
    ql, kl, vl, gl = [by_sample(a, seq) for a in
                      _proj_rt(xl, mods, i, lat[0], w_in, _rt_rope_tables(seq), seq, tm)]
    kc, vc = [by_sample(a, n_ctx) for a in _proj_rt(xc, mods, i, cx[0], w_in, None, n_ctx, tmc)]
    o = _retention(kc, vc, ql, kl, vl, gl, rt_decay_logit[0], rt_gn_g.reshape(-1, 1, RT_HEADS * RT_V))
    xl = mix_out(o.reshape(bsz * seq, RT_HEADS * RT_V), mixw["rt_o"], xl, i, lat)

    i = 3
    xl = ffn_a(xl, i, lat)
    xl = _short_conv_ffn(xl, mods, i, lat[0], mixw["sc_in"], sc_conv_w, mixw["sc_o"],
                         ffw[(i, 1)][0], ffw[(i, 1)][1], lng, lnb, seq, tm)
    return xl.reshape(bsz, seq, D)
```
